```python
import math
import jax, jax.numpy as jnp
from jax import lax
import numpy as np

D_MODEL = 2048
BATCH = 1
SEQ = 16384
DEPTH = 1

CHUNK = 64
MIX_WIDTH = D_MODEL
SSM_WIDTH = MIX_WIDTH // 2
POOL_WIDTH = MIX_WIDTH - SSM_WIDTH
SSM_GROUP = 16
SSM_GROUPS = SSM_WIDTH // SSM_GROUP
SSM_STATE = 64
POOL_WINDOWS = (2, 4, 8, 16)
POOL_GROUPS = len(POOL_WINDOWS)
POOL_GROUP = POOL_WIDTH // POOL_GROUPS
D_FF = ((8 * D_MODEL // 3 + 255) // 256) * 256
EPS = 1e-6
DT_MIN = 1e-3
DT_MAX = 1e-1

kernel_name = "hybrid_s5_multiscale_pool_block"


def rms_norm(x, gain):
    xf = x.astype(jnp.float32)
    y = xf * lax.rsqrt(jnp.mean(xf * xf, axis=-1, keepdims=True) + EPS)
    return y * gain.astype(jnp.float32)


def s5_mixer(u, lam_re, lam_im, log_step, b_re, b_im, c_re, c_im, d_skip, w_glu, b_glu):
    bsz, L, _ = u.shape
    f32 = jnp.float32
    ug = u.astype(f32).reshape(bsz, L, SSM_GROUPS, SSM_GROUP)
    lam = lax.complex(lam_re.astype(f32), lam_im.astype(f32))
    step = jnp.exp(log_step.astype(f32))[:, None]
    lam_bar = jnp.exp(lam * step)
    bmat = lax.complex(b_re.astype(f32), b_im.astype(f32))
    b_bar = ((lam_bar - 1.0) / lam)[..., None] * bmat
    bu = jnp.einsum('blgh,gph->blgp', ug, b_bar)
    a = jnp.broadcast_to(lam_bar, bu.shape)

    def combine(left, right):
        a_l, b_l = left
        a_r, b_r = right
        return a_r * a_l, a_r * b_l + b_r

    _, states = lax.associative_scan(combine, (a, bu), axis=1)
    cmat = lax.complex(c_re.astype(f32), c_im.astype(f32))
    y = jnp.real(jnp.einsum('blgp,ghp->blgh', states, cmat)) + d_skip.astype(f32) * ug
    y = y.reshape(bsz, L, SSM_WIDTH)
    g = jax.nn.gelu(y, approximate=False)
    return g * jax.nn.sigmoid(g @ w_glu.astype(f32) + b_glu.astype(f32))


def pool_mixer(v, w_pool, b_pool, pool_scale):
    bsz, L, _ = v.shape
    f32 = jnp.float32
    vg = v.astype(f32).reshape(bsz, L, POOL_GROUPS, POOL_GROUP)
    cs = jnp.pad(jnp.cumsum(vg, axis=1), ((0, 0), (1, 0), (0, 0), (0, 0)))
    pos1 = jnp.arange(1, L + 1)
    outs = []
    for k, w in enumerate(POOL_WINDOWS):
        c = cs[:, :, k]
        hi = c[:, 1:]
        lo = jnp.pad(c, ((0, 0), (w, 0), (0, 0)))[:, 1:L + 1]
        count = jnp.minimum(pos1, w).astype(f32)[None, :, None]
        outs.append((hi - lo) / count - vg[:, :, k])
    pooled = jnp.stack(outs, axis=2)
    mixed = jnp.einsum('blkc,kcd->blkd', pooled, w_pool.astype(f32)) + b_pool.astype(f32)
    return mixed.reshape(bsz, L, POOL_WIDTH) * pool_scale.astype(f32)


def setup_inputs(seed: int = 0) -> dict:
    key = jax.random.key(seed)
    ks = jax.random.split(key, 24)
    f32 = jnp.float32
    nrm = lambda k, shape, s: jax.random.normal(k, shape, f32) * s
    gain = lambda k, shape: 1.0 + 0.02 * jax.random.normal(k, shape, f32)
    G, P, H = SSM_GROUPS, SSM_STATE, SSM_GROUP
    lam_im0 = jnp.pi * jnp.arange(P, dtype=f32)
    return {
        "x": jax.random.normal(ks[0], (BATCH, SEQ, D_MODEL), f32),
        "norm_mix": gain(ks[1], (DEPTH, D_MODEL)),
        "w_in": nrm(ks[2], (DEPTH, D_MODEL, MIX_WIDTH), D_MODEL ** -0.5),
        "lambda_re": -0.5 + 0.01 * jax.random.normal(ks[3], (DEPTH, G, P), f32),
        "lambda_im": lam_im0 + 0.01 * jax.random.normal(ks[4], (DEPTH, G, P), f32),
        "log_step": jax.random.uniform(ks[5], (DEPTH, G), f32, math.log(DT_MIN), math.log(DT_MAX)),
        "b_re": nrm(ks[6], (DEPTH, G, P, H), (2.0 * H) ** -0.5),
        "b_im": nrm(ks[7], (DEPTH, G, P, H), (2.0 * H) ** -0.5),
        "c_re": nrm(ks[8], (DEPTH, G, H, P), (2.0 * P) ** -0.5),
        "c_im": nrm(ks[9], (DEPTH, G, H, P), (2.0 * P) ** -0.5),
        "d_skip": nrm(ks[10], (DEPTH, G, H), 1.0),
        "w_glu": nrm(ks[11], (DEPTH, SSM_WIDTH, SSM_WIDTH), SSM_WIDTH ** -0.5),
        "b_glu": nrm(ks[12], (DEPTH, SSM_WIDTH), 0.01),
        "w_pool": nrm(ks[13], (DEPTH, POOL_GROUPS, POOL_GROUP, POOL_GROUP), POOL_GROUP ** -0.5),
        "b_pool": nrm(ks[14], (DEPTH, POOL_GROUPS, POOL_GROUP), 0.01),
        "pool_scale": gain(ks[15], (DEPTH, POOL_WIDTH)),
        "w_out": nrm(ks[16], (DEPTH, MIX_WIDTH, D_MODEL), MIX_WIDTH ** -0.5),
        "norm_ffn": gain(ks[17], (DEPTH, D_MODEL)),
        "w_gate": nrm(ks[18], (DEPTH, D_MODEL, D_FF), D_MODEL ** -0.5),
        "w_up": nrm(ks[19], (DEPTH, D_MODEL, D_FF), D_MODEL ** -0.5),
        "w_down": nrm(ks[20], (DEPTH, D_FF, D_MODEL), D_FF ** -0.5),
        "norm_final": gain(ks[21], (D_MODEL,)),
    }


def reference(x, norm_mix, w_in, lambda_re, lambda_im, log_step, b_re, b_im, c_re, c_im,
              d_skip, w_glu, b_glu, w_pool, b_pool, pool_scale, w_out, norm_ffn,
              w_gate, w_up, w_down, norm_final):
    out_dtype = x.dtype
    h_res = x.astype(jnp.float32)
    for l in range(DEPTH):
        h = rms_norm(h_res, norm_mix[l])
        proj = h @ w_in[l].astype(jnp.float32)
        u = proj[..., :SSM_WIDTH]
        v = proj[..., SSM_WIDTH:]
        y_ssm = s5_mixer(u, lambda_re[l], lambda_im[l], log_step[l], b_re[l], b_im[l],
                         c_re[l], c_im[l], d_skip[l], w_glu[l], b_glu[l])
        y_pool = pool_mixer(v, w_pool[l], b_pool[l], pool_scale[l])
        mixed = jnp.concatenate([y_ssm, y_pool], axis=-1)
        h_res = h_res + mixed @ w_out[l].astype(jnp.float32)
        h = rms_norm(h_res, norm_ffn[l])
        ff = jax.nn.silu(h @ w_gate[l].astype(jnp.float32)) * (h @ w_up[l].astype(jnp.float32))
        h_res = h_res + ff @ w_down[l].astype(jnp.float32)
    return rms_norm(h_res, norm_final).astype(out_dtype)
```

```python
import functools
import math

import jax
import jax.numpy as jnp
from jax import lax
from jax.experimental import pallas as pl
from jax.experimental.pallas import tpu as pltpu

F32 = jnp.float32
BF16 = jnp.bfloat16

D_MODEL = 2048
SSM_WIDTH = 1024
POOL_WIDTH = 1024
SSM_GROUP = 16
SSM_GROUPS = 64
SSM_STATE = 64
POOL_WINDOWS = (2, 4, 8, 16)
POOL_GROUP = 256
EPS = 1e-6

CHUNK = 16
CHUNK_LANES = CHUNK * SSM_GROUP
HALO = 16

VMEM_LIMIT_BYTES = 60 * 1024 * 1024

HIGHEST = lax.Precision.HIGHEST


def _rms_norm(x, gain):
    ms = jnp.mean(x * x, axis=-1, keepdims=True)
    return x * lax.rsqrt(ms + EPS) * gain


def _ssm_prep_kernel(lre_c_ref, lim_c_ref, lre_r_ref, lim_r_ref, ls_ref,
                     b_re_ref, b_im_ref, bt_re_ref, bt_im_ref,
                     ct_re_ref, ct_im_ref, d_ref,
                     m_ref, bw_ref, cw_ref):
    P, H, T = SSM_STATE, SSM_GROUP, CHUNK
    step = jnp.exp(ls_ref[...])

    def discretise(lre, lim):
        zr, zi = lre * step, lim * step
        mag = jnp.exp(zr)
        lbr, lbi = mag * jnp.cos(zi), mag * jnp.sin(zi)
        nr, ni = lbr - 1.0, lbi
        den = lre * lre + lim * lim
        cr = (nr * lre + ni * lim) / den
        ci = (ni * lre - nr * lim) / den
        return zr, zi, lbr, lbi, cr, ci

    zr, zi, lbr, lbi, cr, ci = discretise(lre_c_ref[...], lim_c_ref[...])
    b_re, b_im = b_re_ref[...], b_im_ref[...]
    bb_re = cr * b_re - ci * b_im
    bb_im = cr * b_im + ci * b_re
    _, _, _, _, cr_r, ci_r = discretise(lre_r_ref[...], lim_r_ref[...])
    bt_re, bt_im = bt_re_ref[...], bt_im_ref[...]
    bbt_re = cr_r * bt_re - ci_r * bt_im
    bbt_im = cr_r * bt_im + ci_r * bt_re

    lane = lax.broadcasted_iota(jnp.int32, (1, CHUNK_LANES), 1)
    k_f = (lane // H).astype(F32)
    sub = lax.broadcasted_iota(jnp.int32, (H, CHUNK_LANES), 0)
    rep = (lax.broadcasted_iota(jnp.int32, (H, CHUNK_LANES), 1) % H == sub).astype(F32)

    def tile(a):
        return jnp.dot(a, rep, precision=HIGHEST, preferred_element_type=F32)

    def powers(kf):
        mag = jnp.exp(kf * zr)
        return mag * jnp.cos(kf * zi), mag * jnp.sin(kf * zi)

    pr, pi = powers((T - 1.0) - k_f)
    tb_re, tb_im = tile(bb_re), tile(bb_im)
    bw_ref[0:P, :] = (pr * tb_re - pi * tb_im).astype(bw_ref.dtype)
    bw_ref[P:2 * P, :] = (pr * tb_im + pi * tb_re).astype(bw_ref.dtype)

    pr, pi = powers(k_f)
    tc_re, tc_im = tile(ct_re_ref[...]), tile(ct_im_ref[...])
    g_re = pr * tc_re - pi * tc_im
    g_im = pr * tc_im + pi * tc_re
    g1_re = lbr * g_re - lbi * g_im
    g1_im = lbr * g_im + lbi * g_re
    cw_ref[0:P, :] = g1_re.astype(cw_ref.dtype)
    cw_ref[P:2 * P, :] = (-g1_im).astype(cw_ref.dtype)

    kt = (jnp.dot(bbt_re, g_re, precision=HIGHEST, preferred_element_type=F32)
          - jnp.dot(bbt_im, g_im, precision=HIGHEST, preferred_element_type=F32))
    lane_h = lax.broadcasted_iota(jnp.int32, (H, CHUNK_LANES), 1)
    kt = kt + jnp.where(lane_h == sub, d_ref[...], 0.0)
    for tau in range(T):
        if tau == 0:
            blk = kt
        else:
            blk = jnp.where(lane_h >= tau * H, pltpu.roll(kt, tau * H, axis=1), 0.0)
        m_ref[tau * H:(tau + 1) * H, :] = blk.astype(m_ref.dtype)


def _ssm_prep(lambda_re, lambda_im, log_step, b_re, b_im, c_re, c_im, d_skip):
    G, P, H = SSM_GROUPS, SSM_STATE, SSM_GROUP
    per_group = lambda *shape: pl.BlockSpec((None,) + shape, lambda g: (g,) + (0,) * len(shape))
    return pl.pallas_call(
        _ssm_prep_kernel,
        grid=(G,),
        in_specs=[per_group(P, 1), per_group(P, 1), per_group(1, P), per_group(1, P), per_group(1, 1),
                  per_group(P, H), per_group(P, H), per_group(H, P), per_group(H, P),
                  per_group(P, H), per_group(P, H), per_group(H, 1)],
        out_specs=[per_group(CHUNK_LANES, CHUNK_LANES), per_group(2 * P, CHUNK_LANES),
                   per_group(2 * P, CHUNK_LANES)],
        out_shape=[jax.ShapeDtypeStruct((G, CHUNK_LANES, CHUNK_LANES), BF16),
                   jax.ShapeDtypeStruct((G, 2 * P, CHUNK_LANES), BF16),
                   jax.ShapeDtypeStruct((G, 2 * P, CHUNK_LANES), BF16)],
        name="ssm_prep",
    )(lambda_re.reshape(G, P, 1), lambda_im.reshape(G, P, 1),
      lambda_re.reshape(G, 1, P), lambda_im.reshape(G, 1, P), log_step.reshape(G, 1, 1),
      b_re, b_im, jnp.swapaxes(b_re, 1, 2), jnp.swapaxes(b_im, 1, 2),
      jnp.swapaxes(c_re, 1, 2), jnp.swapaxes(c_im, 1, 2), d_skip.reshape(G, H, 1))


def _pair_blocks(w):
    G, P = SSM_GROUPS, SSM_STATE
    w = w.reshape(G // 2, 2, 2, P, CHUNK_LANES)
    z = jnp.zeros_like(w[:, 0, 0])
    rows = [jnp.concatenate([w[:, 0, 0], z], axis=-1), jnp.concatenate([z, w[:, 1, 0]], axis=-1),
            jnp.concatenate([w[:, 0, 1], z], axis=-1), jnp.concatenate([z, w[:, 1, 1]], axis=-1)]
    return jnp.concatenate(rows, axis=1)


def _in_proj_kernel(x_ref, gain_ref, w_ref, u_ref, v_ref):
    h = _rms_norm(x_ref[...], gain_ref[...]).astype(BF16)
    p = jnp.dot(h, w_ref[...], preferred_element_type=F32)
    u_ref[...] = p[:, :SSM_WIDTH].astype(u_ref.dtype)
    v_ref[...] = p[:, SSM_WIDTH:].astype(v_ref.dtype)


def _in_proj(x, gain, w_in, tm):
    L = x.shape[0]
    return pl.pallas_call(
        _in_proj_kernel,
        grid=(L // tm,),
        in_specs=[pl.BlockSpec((tm, D_MODEL), lambda i: (i, 0)),
                  pl.BlockSpec((1, D_MODEL), lambda i: (0, 0)),
                  pl.BlockSpec((D_MODEL, D_MODEL), lambda i: (0, 0))],
        out_specs=[pl.BlockSpec((tm, SSM_WIDTH), lambda i: (i, 0)),
                   pl.BlockSpec((tm, POOL_WIDTH), lambda i: (i, 0))],
        out_shape=[jax.ShapeDtypeStruct((L, SSM_WIDTH), BF16),
                   jax.ShapeDtypeStruct((L, POOL_WIDTH), BF16)],
        compiler_params=pltpu.CompilerParams(dimension_semantics=("parallel",),
                                             vmem_limit_bytes=VMEM_LIMIT_BYTES),
        name="in_proj",
    )(x, gain, w_in)


def _ssm_kernel(x_ref, m_ref, bw_ref, cw_ref, lre_ref, lim_ref, ls_ref, y_ref):
    n_chunks = x_ref.shape[1]
    x0, x1 = x_ref[0], x_ref[1]
    xp = jnp.concatenate([x0, x1], axis=1)
    e = lax.dot_general(xp, bw_ref[...], (((1,), (1,)), ((), ())), preferred_element_type=F32)
    row = lax.broadcasted_iota(jnp.int32, (n_chunks, 2 * SSM_STATE), 0)
    sr = jnp.where(row >= 1, pltpu.roll(e[:, :2 * SSM_STATE], 1, axis=0), 0.0)
    si = jnp.where(row >= 1, pltpu.roll(e[:, 2 * SSM_STATE:], 1, axis=0), 0.0)
    step = jnp.exp(ls_ref[...])
    zr, zi = lre_ref[...] * step * CHUNK, lim_ref[...] * step * CHUNK
    mag = jnp.exp(zr)
    ar, ai = mag * jnp.cos(zi), mag * jnp.sin(zi)
    d = 1
    while d < n_chunks:
        tr = jnp.where(row >= d, pltpu.roll(sr, d, axis=0), 0.0)
        ti = jnp.where(row >= d, pltpu.roll(si, d, axis=0), 0.0)
        sr, si = sr + ar * tr - ai * ti, si + ar * ti + ai * tr
        ar, ai = ar * ar - ai * ai, 2.0 * ar * ai
        d *= 2
    s = jnp.concatenate([sr, si], axis=1).astype(BF16)
    yc = jnp.dot(s, cw_ref[...], preferred_element_type=F32)
    y0 = jnp.dot(x0, m_ref[0], preferred_element_type=F32) + yc[:, :CHUNK_LANES]
    y1 = jnp.dot(x1, m_ref[1], preferred_element_type=F32) + yc[:, CHUNK_LANES:]
    y_ref[0] = y0.astype(y_ref.dtype)
    y_ref[1] = y1.astype(y_ref.dtype)


def _ssm(xg, m, bw_pair, cw_pair, lre_row, lim_row, ls_row):
    G, n_chunks, _ = xg.shape
    n_pairs = G // 2
    return pl.pallas_call(
        _ssm_kernel,
        grid=(n_pairs,),
        in_specs=[pl.BlockSpec((2, n_chunks, CHUNK_LANES), lambda g: (g, 0, 0)),
                  pl.BlockSpec((2, CHUNK_LANES, CHUNK_LANES), lambda g: (g, 0, 0)),
                  pl.BlockSpec((None, 4 * SSM_STATE, 2 * CHUNK_LANES), lambda g: (g, 0, 0)),
                  pl.BlockSpec((None, 4 * SSM_STATE, 2 * CHUNK_LANES), lambda g: (g, 0, 0)),
                  pl.BlockSpec((None, 1, 2 * SSM_STATE), lambda g: (g, 0, 0)),
                  pl.BlockSpec((None, 1, 2 * SSM_STATE), lambda g: (g, 0, 0)),
                  pl.BlockSpec((None, 1, 2 * SSM_STATE), lambda g: (g, 0, 0))],
        out_specs=pl.BlockSpec((2, n_chunks, CHUNK_LANES), lambda g: (g, 0, 0)),
        out_shape=jax.ShapeDtypeStruct((G, n_chunks, CHUNK_LANES), BF16),
        compiler_params=pltpu.CompilerParams(dimension_semantics=("parallel",),
                                             vmem_limit_bytes=VMEM_LIMIT_BYTES),
        name="ssm",
    )(xg, m, bw_pair, cw_pair, lre_row, lim_row, ls_row)


def _out_proj_kernel(y_ref, v_ref, vh_ref, x_ref, wglu_ref, bglu_ref, wpool_ref, bpool_ref,
                     pscale_ref, wtop_ref, wbot_ref, o_ref, vbuf_ref):
    i = pl.program_id(0)
    tm = y_ref.shape[0]
    y = y_ref[...].astype(F32)
    g = 0.5 * y * (1.0 + lax.erf(y * (1.0 / math.sqrt(2.0))))
    z = jnp.dot(g.astype(BF16), wglu_ref[...], preferred_element_type=F32) + bglu_ref[...]
    ys = g * (1.0 / (1.0 + jnp.exp(-z)))

    vbuf_ref[0:HALO, :] = jnp.where(i > 0, vh_ref[...].astype(F32), 0.0)
    vbuf_ref[HALO:, :] = v_ref[...].astype(F32)
    t1 = (lax.broadcasted_iota(jnp.int32, (tm, 1), 0) + (i * tm + 1)).astype(F32)
    pooled = []
    for k, w in enumerate(POOL_WINDOWS):
        cols = slice(k * POOL_GROUP, (k + 1) * POOL_GROUP)
        vk = vbuf_ref[:, cols]
        acc, span = vk, 1
        while span < w:
            acc = acc + pltpu.roll(acc, span, axis=0)
            span *= 2
        mean = acc[HALO:, :] / jnp.minimum(t1, float(w))
        pk = (mean - vk[HALO:, :]).astype(BF16)
        mk = jnp.dot(pk, wpool_ref[k], preferred_element_type=F32)
        pooled.append(mk)
    yp = (jnp.concatenate(pooled, axis=1) + bpool_ref[...]) * pscale_ref[...]

    o_ref[...] = (x_ref[...]
                  + jnp.dot(ys.astype(BF16), wtop_ref[...], preferred_element_type=F32)
                  + jnp.dot(yp.astype(BF16), wbot_ref[...], preferred_element_type=F32))


def _out_proj(y, v, x, w_glu, b_glu, w_pool, b_pool, pool_scale, w_top, w_bot, tm):
    L = x.shape[0]
    halo_blocks = tm // HALO
    const = lambda *shape: pl.BlockSpec(shape, lambda i: (0,) * len(shape))
    return pl.pallas_call(
        _out_proj_kernel,
        grid=(L // tm,),
        in_specs=[pl.BlockSpec((tm, SSM_WIDTH), lambda i: (i, 0)),
                  pl.BlockSpec((tm, POOL_WIDTH), lambda i: (i, 0)),
                  pl.BlockSpec((HALO, POOL_WIDTH), lambda i: (jnp.maximum(i * halo_blocks - 1, 0), 0)),
                  pl.BlockSpec((tm, D_MODEL), lambda i: (i, 0)),
                  const(SSM_WIDTH, SSM_WIDTH), const(1, SSM_WIDTH),
                  const(len(POOL_WINDOWS), POOL_GROUP, POOL_GROUP), const(1, POOL_WIDTH),
                  const(1, POOL_WIDTH),
                  const(SSM_WIDTH, D_MODEL), const(POOL_WIDTH, D_MODEL)],
        out_specs=pl.BlockSpec((tm, D_MODEL), lambda i: (i, 0)),
        out_shape=jax.ShapeDtypeStruct((L, D_MODEL), F32),
        scratch_shapes=[pltpu.VMEM((HALO + tm, POOL_WIDTH), F32)],
        compiler_params=pltpu.CompilerParams(dimension_semantics=("parallel",),
                                             vmem_limit_bytes=VMEM_LIMIT_BYTES),
        name="out_proj",
    )(y, v, v, x, w_glu, b_glu, w_pool, b_pool, pool_scale, w_top, w_bot)


def _ffn_kernel(x_ref, gain_ref, wg_ref, wu_ref, wd_ref, gfin_ref, o_ref, h_ref):
    j = pl.program_id(1)

    @pl.when(j == 0)
    def _():
        x = x_ref[...]
        h_ref[...] = _rms_norm(x, gain_ref[...]).astype(BF16)
        o_ref[...] = x

    h = h_ref[...]
    a = jnp.dot(h, wg_ref[...], preferred_element_type=F32)
    b = jnp.dot(h, wu_ref[...], preferred_element_type=F32)
    ff = (a * (1.0 / (1.0 + jnp.exp(-a))) * b).astype(BF16)
    o_ref[...] += jnp.dot(ff, wd_ref[...], preferred_element_type=F32)

    @pl.when(j == pl.num_programs(1) - 1)
    def _():
        o_ref[...] = _rms_norm(o_ref[...], gfin_ref[...])


def _ffn(x, gain, w_gate, w_up, w_down, gain_final, tm, tf):
    L = x.shape[0]
    d_ff = w_gate.shape[1]
    return pl.pallas_call(
        _ffn_kernel,
        grid=(L // tm, d_ff // tf),
        in_specs=[pl.BlockSpec((tm, D_MODEL), lambda i, j: (i, 0)),
                  pl.BlockSpec((1, D_MODEL), lambda i, j: (0, 0)),
                  pl.BlockSpec((D_MODEL, tf), lambda i, j: (0, j)),
                  pl.BlockSpec((D_MODEL, tf), lambda i, j: (0, j)),
                  pl.BlockSpec((tf, D_MODEL), lambda i, j: (j, 0)),
                  pl.BlockSpec((1, D_MODEL), lambda i, j: (0, 0))],
        out_specs=pl.BlockSpec((tm, D_MODEL), lambda i, j: (i, 0)),
        out_shape=jax.ShapeDtypeStruct((L, D_MODEL), F32),
        scratch_shapes=[pltpu.VMEM((tm, D_MODEL), BF16)],
        compiler_params=pltpu.CompilerParams(dimension_semantics=("parallel", "arbitrary"),
                                             vmem_limit_bytes=VMEM_LIMIT_BYTES),
        name="ffn",
    )(x, gain, w_gate, w_up, w_down, gain_final)


def kernel(x, norm_mix, w_in, lambda_re, lambda_im, log_step, b_re, b_im, c_re, c_im, d_skip,
           w_glu, b_glu, w_pool, b_pool, pool_scale, w_out, norm_ffn, w_gate, w_up, w_down,
           norm_final):
    bsz, L, D = x.shape
    assert bsz == 1 and D == D_MODEL and w_in.shape[0] == 1 and L % 1024 == 0
    G, P, H = SSM_GROUPS, SSM_STATE, SSM_GROUP
    n_chunks = L // CHUNK
    xs = x.reshape(L, D).astype(F32)

    m, bw, cw = _ssm_prep(lambda_re[0].astype(F32), lambda_im[0].astype(F32), log_step[0].astype(F32),
                          b_re[0].astype(F32), b_im[0].astype(F32), c_re[0].astype(F32),
                          c_im[0].astype(F32), d_skip[0].astype(F32))
    bw_pair, cw_pair = _pair_blocks(bw), _pair_blocks(cw)
    lre_row = lambda_re[0].astype(F32).reshape(G // 2, 1, 2 * P)
    lim_row = lambda_im[0].astype(F32).reshape(G // 2, 1, 2 * P)
    ls_row = jnp.repeat(log_step[0].astype(F32), P).reshape(G // 2, 1, 2 * P)

    u, v = _in_proj(xs, norm_mix[0].astype(F32).reshape(1, D), w_in[0].astype(BF16), tm=512)

    xg = u.reshape(n_chunks, CHUNK, G, H).transpose(2, 0, 1, 3).reshape(G, n_chunks, CHUNK_LANES)
    yg = _ssm(xg, m, bw_pair, cw_pair, lre_row, lim_row, ls_row)
    y = yg.reshape(G, n_chunks, CHUNK, H).transpose(1, 2, 0, 3).reshape(L, SSM_WIDTH)

    w_out_b = w_out[0].astype(BF16)
    x1 = _out_proj(y, v, xs, w_glu[0].astype(BF16), b_glu[0].astype(F32).reshape(1, SSM_WIDTH),
                   w_pool[0].astype(BF16), b_pool[0].astype(F32).reshape(1, POOL_WIDTH),
                   pool_scale[0].astype(F32).reshape(1, POOL_WIDTH),
                   w_out_b[:SSM_WIDTH], w_out_b[SSM_WIDTH:], tm=512)

    out = _ffn(x1, norm_ffn[0].astype(F32).reshape(1, D), w_gate[0].astype(BF16),
               w_up[0].astype(BF16), w_down[0].astype(BF16),
               norm_final.astype(F32).reshape(1, D), tm=1024, tf=512)
    return out.reshape(bsz, L, D).astype(x.dtype)
```

```python
import functools
import math

import jax
import jax.numpy as jnp
from jax import lax
from jax.experimental import pallas as pl
from jax.experimental.pallas import tpu as pltpu

F32 = jnp.float32
BF16 = jnp.bfloat16

D_MODEL = 2048
SSM_WIDTH = 1024
POOL_WIDTH = 1024
SSM_GROUP = 16
SSM_GROUPS = 64
SSM_STATE = 64
POOL_WINDOWS = (2, 4, 8, 16)
POOL_GROUP = 256
EPS = 1e-6

CHUNK = 16
CHUNK_LANES = CHUNK * SSM_GROUP
HALO = 16
LANES = 128

VMEM_LIMIT_BYTES = 60 * 1024 * 1024

HIGHEST = lax.Precision.HIGHEST


def _rms_norm(x, gain):
    ms = jnp.mean(x * x, axis=-1, keepdims=True)
    return x * lax.rsqrt(ms + EPS) * gain


def _ssm_prep_kernel(lre_c_ref, lim_c_ref, lre_r_ref, lim_r_ref, ls_ref,
                     b_re_ref, b_im_ref, bt_re_ref, bt_im_ref,
                     ct_re_ref, ct_im_ref, d_ref,
                     m_ref, bw_ref, cw_ref):
    P, H, T = SSM_STATE, SSM_GROUP, CHUNK
    step = jnp.exp(ls_ref[...])

    def discretise(lre, lim):
        zr, zi = lre * step, lim * step
        mag = jnp.exp(zr)
        lbr, lbi = mag * jnp.cos(zi), mag * jnp.sin(zi)
        nr, ni = lbr - 1.0, lbi
        den = lre * lre + lim * lim
        cr = (nr * lre + ni * lim) / den
        ci = (ni * lre - nr * lim) / den
        return zr, zi, lbr, lbi, cr, ci

    zr, zi, lbr, lbi, cr, ci = discretise(lre_c_ref[...], lim_c_ref[...])
    b_re, b_im = b_re_ref[...], b_im_ref[...]
    bb_re = cr * b_re - ci * b_im
    bb_im = cr * b_im + ci * b_re
    _, _, _, _, cr_r, ci_r = discretise(lre_r_ref[...], lim_r_ref[...])
    bt_re, bt_im = bt_re_ref[...], bt_im_ref[...]
    bbt_re = cr_r * bt_re - ci_r * bt_im
    bbt_im = cr_r * bt_im + ci_r * bt_re

    lane = lax.broadcasted_iota(jnp.int32, (1, CHUNK_LANES), 1)
    k_f = (lane // H).astype(F32)
    sub = lax.broadcasted_iota(jnp.int32, (H, CHUNK_LANES), 0)
    rep = (lax.broadcasted_iota(jnp.int32, (H, CHUNK_LANES), 1) % H == sub).astype(F32)

    def tile(a):
        return jnp.dot(a, rep, precision=HIGHEST, preferred_element_type=F32)

    def powers(kf):
        mag = jnp.exp(kf * zr)
        return mag * jnp.cos(kf * zi), mag * jnp.sin(kf * zi)

    pr, pi = powers((T - 1.0) - k_f)
    tb_re, tb_im = tile(bb_re), tile(bb_im)
    bw_ref[0:P, :] = (pr * tb_re - pi * tb_im).astype(bw_ref.dtype)
    bw_ref[P:2 * P, :] = (pr * tb_im + pi * tb_re).astype(bw_ref.dtype)

    pr, pi = powers(k_f)
    tc_re, tc_im = tile(ct_re_ref[...]), tile(ct_im_ref[...])
    g_re = pr * tc_re - pi * tc_im
    g_im = pr * tc_im + pi * tc_re
    g1_re = lbr * g_re - lbi * g_im
    g1_im = lbr * g_im + lbi * g_re
    cw_ref[0:P, :] = g1_re.astype(cw_ref.dtype)
    cw_ref[P:2 * P, :] = (-g1_im).astype(cw_ref.dtype)

    kt = (jnp.dot(bbt_re, g_re, precision=HIGHEST, preferred_element_type=F32)
          - jnp.dot(bbt_im, g_im, precision=HIGHEST, preferred_element_type=F32))
    lane_h = lax.broadcasted_iota(jnp.int32, (H, CHUNK_LANES), 1)
    kt = kt + jnp.where(lane_h == sub, d_ref[...], 0.0)
    for tau in range(T):
        if tau == 0:
            blk = kt
        else:
            blk = jnp.where(lane_h >= tau * H, pltpu.roll(kt, tau * H, axis=1), 0.0)
        m_ref[tau * H:(tau + 1) * H, :] = blk.astype(m_ref.dtype)


def _ssm_prep(lambda_re, lambda_im, log_step, b_re, b_im, c_re, c_im, d_skip):
    G, P, H = SSM_GROUPS, SSM_STATE, SSM_GROUP
    per_group = lambda *shape: pl.BlockSpec((None,) + shape, lambda g: (g,) + (0,) * len(shape))
    return pl.pallas_call(
        _ssm_prep_kernel,
        grid=(G,),
        in_specs=[per_group(P, 1), per_group(P, 1), per_group(1, P), per_group(1, P), per_group(1, 1),
                  per_group(P, H), per_group(P, H), per_group(H, P), per_group(H, P),
                  per_group(P, H), per_group(P, H), per_group(H, 1)],
        out_specs=[per_group(CHUNK_LANES, CHUNK_LANES), per_group(2 * P, CHUNK_LANES),
                   per_group(2 * P, CHUNK_LANES)],
        out_shape=[jax.ShapeDtypeStruct((G, CHUNK_LANES, CHUNK_LANES), BF16),
                   jax.ShapeDtypeStruct((G, 2 * P, CHUNK_LANES), BF16),
                   jax.ShapeDtypeStruct((G, 2 * P, CHUNK_LANES), BF16)],
        name="ssm_prep",
    )(lambda_re.reshape(G, P, 1), lambda_im.reshape(G, P, 1),
      lambda_re.reshape(G, 1, P), lambda_im.reshape(G, 1, P), log_step.reshape(G, 1, 1),
      b_re, b_im, jnp.swapaxes(b_re, 1, 2), jnp.swapaxes(b_im, 1, 2),
      jnp.swapaxes(c_re, 1, 2), jnp.swapaxes(c_im, 1, 2), d_skip.reshape(G, H, 1))


def _pair_blocks(w):
    G, P = SSM_GROUPS, SSM_STATE
    w = w.reshape(G // 2, 2, 2, P, CHUNK_LANES)
    z = jnp.zeros_like(w[:, 0, 0])
    rows = [jnp.concatenate([w[:, 0, 0], z], axis=-1), jnp.concatenate([z, w[:, 1, 0]], axis=-1),
            jnp.concatenate([w[:, 0, 1], z], axis=-1), jnp.concatenate([z, w[:, 1, 1]], axis=-1)]
    return jnp.concatenate(rows, axis=1)


def _in_proj_kernel(x_ref, gain_ref, w_ref, u_ref, v_ref):
    h = _rms_norm(x_ref[...], gain_ref[...]).astype(BF16)
    p = jnp.dot(h, w_ref[...], preferred_element_type=F32)
    u_ref[...] = p[:, :SSM_WIDTH].astype(u_ref.dtype)
    v_ref[...] = p[:, SSM_WIDTH:].astype(v_ref.dtype)


def _in_proj(x, gain, w_in, tm):
    L = x.shape[0]
    return pl.pallas_call(
        _in_proj_kernel,
        grid=(L // tm,),
        in_specs=[pl.BlockSpec((tm, D_MODEL), lambda i: (i, 0)),
                  pl.BlockSpec((1, D_MODEL), lambda i: (0, 0)),
                  pl.BlockSpec((D_MODEL, D_MODEL), lambda i: (0, 0))],
        out_specs=[pl.BlockSpec((tm, SSM_WIDTH), lambda i: (i, 0)),
                   pl.BlockSpec((tm, POOL_WIDTH), lambda i: (i, 0))],
        out_shape=[jax.ShapeDtypeStruct((L, SSM_WIDTH), F32),
                   jax.ShapeDtypeStruct((L, POOL_WIDTH), BF16)],
        compiler_params=pltpu.CompilerParams(dimension_semantics=("parallel",),
                                             vmem_limit_bytes=VMEM_LIMIT_BYTES),
        name="in_proj",
    )(x, gain, w_in)


GROUPS_PER_BLOCK = LANES // SSM_GROUP
RELAYOUT_ROWS = 32


def _granule_transpose(a):
    lane = lax.broadcasted_iota(jnp.int32, a[0].shape, 1)
    a = list(a)
    for s in (4, 2, 1):
        upper = ((lane // SSM_GROUP) & s) != 0
        nxt = list(a)
        for i in range(GROUPS_PER_BLOCK):
            if i & s:
                continue
            lo, hi = a[i], a[i + s]
            nxt[i] = jnp.where(upper, pltpu.roll(hi, s * SSM_GROUP, axis=1), lo)
            nxt[i + s] = jnp.where(upper, hi, pltpu.roll(lo, LANES - s * SSM_GROUP, axis=1))
        a = nxt
    return a


def _ssm_kernel(u_ref, m_ref, bw_ref, cw_ref, lre_ref, lim_ref, ls_ref, y_ref, xg_ref, yg_ref):
    n_chunks = xg_ref.shape[1]
    half = CHUNK // 2
    rb = RELAYOUT_ROWS

    for cb in range(n_chunks // rb):
        for hf in range(2):
            a = [u_ref[pl.ds(cb * rb * CHUNK + hf * half + t, rb, stride=CHUNK), :] for t in range(half)]
            b = _granule_transpose(a)
            for g in range(GROUPS_PER_BLOCK):
                xg_ref[g, cb * rb:(cb + 1) * rb, hf * LANES:(hf + 1) * LANES] = b[g].astype(xg_ref.dtype)

    row = lax.broadcasted_iota(jnp.int32, (n_chunks, 2 * SSM_STATE), 0)
    for q in range(GROUPS_PER_BLOCK // 2):
        x0, x1 = xg_ref[2 * q], xg_ref[2 * q + 1]
        xp = jnp.concatenate([x0, x1], axis=1)
        e = lax.dot_general(xp, bw_ref[q], (((1,), (1,)), ((), ())), preferred_element_type=F32)
        sr = jnp.where(row >= 1, pltpu.roll(e[:, :2 * SSM_STATE], 1, axis=0), 0.0)
        si = jnp.where(row >= 1, pltpu.roll(e[:, 2 * SSM_STATE:], 1, axis=0), 0.0)
        step = jnp.exp(ls_ref[q])
        zr, zi = lre_ref[q] * step * CHUNK, lim_ref[q] * step * CHUNK
        mag = jnp.exp(zr)
        ar, ai = mag * jnp.cos(zi), mag * jnp.sin(zi)
        d = 1
        while d < n_chunks:
            tr = jnp.where(row >= d, pltpu.roll(sr, d, axis=0), 0.0)
            ti = jnp.where(row >= d, pltpu.roll(si, d, axis=0), 0.0)
            sr, si = sr + ar * tr - ai * ti, si + ar * ti + ai * tr
            ar, ai = ar * ar - ai * ai, 2.0 * ar * ai
            d *= 2
        s = jnp.concatenate([sr, si], axis=1).astype(BF16)
        yc = jnp.dot(s, cw_ref[q], preferred_element_type=F32)
        yg_ref[2 * q] = jnp.dot(x0, m_ref[2 * q], preferred_element_type=F32) + yc[:, :CHUNK_LANES]
        yg_ref[2 * q + 1] = jnp.dot(x1, m_ref[2 * q + 1], preferred_element_type=F32) + yc[:, CHUNK_LANES:]

    for cb in range(n_chunks // rb):
        for hf in range(2):
            b = [yg_ref[g, cb * rb:(cb + 1) * rb, hf * LANES:(hf + 1) * LANES]
                 for g in range(GROUPS_PER_BLOCK)]
            a = _granule_transpose(b)
            for t in range(half):
                y_ref[pl.ds(cb * rb * CHUNK + hf * half + t, rb, stride=CHUNK), :] = a[t]


def _ssm(u, m, bw_pair, cw_pair, lre_row, lim_row, ls_row):
    L = u.shape[0]
    n_chunks = L // CHUNK
    n_blocks = SSM_WIDTH // LANES
    ppb = GROUPS_PER_BLOCK // 2
    return pl.pallas_call(
        _ssm_kernel,
        grid=(n_blocks,),
        in_specs=[pl.BlockSpec((L, LANES), lambda b: (0, b)),
                  pl.BlockSpec((GROUPS_PER_BLOCK, CHUNK_LANES, CHUNK_LANES), lambda b: (b, 0, 0)),
                  pl.BlockSpec((ppb, 4 * SSM_STATE, 2 * CHUNK_LANES), lambda b: (b, 0, 0)),
                  pl.BlockSpec((ppb, 4 * SSM_STATE, 2 * CHUNK_LANES), lambda b: (b, 0, 0)),
                  pl.BlockSpec((ppb, 1, 2 * SSM_STATE), lambda b: (b, 0, 0)),
                  pl.BlockSpec((ppb, 1, 2 * SSM_STATE), lambda b: (b, 0, 0)),
                  pl.BlockSpec((ppb, 1, 2 * SSM_STATE), lambda b: (b, 0, 0))],
        out_specs=pl.BlockSpec((L, LANES), lambda b: (0, b)),
        out_shape=jax.ShapeDtypeStruct((L, SSM_WIDTH), F32),
        scratch_shapes=[pltpu.VMEM((GROUPS_PER_BLOCK, n_chunks, CHUNK_LANES), BF16),
                        pltpu.VMEM((GROUPS_PER_BLOCK, n_chunks, CHUNK_LANES), F32)],
        compiler_params=pltpu.CompilerParams(dimension_semantics=("parallel",),
                                             vmem_limit_bytes=VMEM_LIMIT_BYTES),
        name="ssm",
    )(u, m, bw_pair, cw_pair, lre_row, lim_row, ls_row)


def _out_proj_kernel(y_ref, v_ref, vh_ref, x_ref, wglu_ref, bglu_ref, wpool_ref, bpool_ref,
                     pscale_ref, wtop_ref, wbot_ref, o_ref, vbuf_ref):
    i = pl.program_id(0)
    tm = y_ref.shape[0]
    y = y_ref[...].astype(F32)
    g = 0.5 * y * (1.0 + lax.erf(y * (1.0 / math.sqrt(2.0))))
    z = jnp.dot(g.astype(BF16), wglu_ref[...], preferred_element_type=F32) + bglu_ref[...]
    ys = g * (1.0 / (1.0 + jnp.exp(-z)))

    vbuf_ref[0:HALO, :] = jnp.where(i > 0, vh_ref[...].astype(F32), 0.0)
    vbuf_ref[HALO:, :] = v_ref[...].astype(F32)
    t1 = (lax.broadcasted_iota(jnp.int32, (tm, 1), 0) + (i * tm + 1)).astype(F32)
    pooled = []
    for k, w in enumerate(POOL_WINDOWS):
        cols = slice(k * POOL_GROUP, (k + 1) * POOL_GROUP)
        vk = vbuf_ref[:, cols]
        acc, span = vk, 1
        while span < w:
            acc = acc + pltpu.roll(acc, span, axis=0)
            span *= 2
        mean = acc[HALO:, :] / jnp.minimum(t1, float(w))
        pk = (mean - vk[HALO:, :]).astype(BF16)
        mk = jnp.dot(pk, wpool_ref[k], preferred_element_type=F32)
        pooled.append(mk)
    yp = (jnp.concatenate(pooled, axis=1) + bpool_ref[...]) * pscale_ref[...]

    o_ref[...] = (x_ref[...]
                  + jnp.dot(ys.astype(BF16), wtop_ref[...], preferred_element_type=F32)
                  + jnp.dot(yp.astype(BF16), wbot_ref[...], preferred_element_type=F32))


def _out_proj(y, v, x, w_glu, b_glu, w_pool, b_pool, pool_scale, w_top, w_bot, tm):
    L = x.shape[0]
    halo_blocks = tm // HALO
    const = lambda *shape: pl.BlockSpec(shape, lambda i: (0,) * len(shape))
    return pl.pallas_call(
        _out_proj_kernel,
        grid=(L // tm,),
        in_specs=[pl.BlockSpec((tm, SSM_WIDTH), lambda i: (i, 0)),
                  pl.BlockSpec((tm, POOL_WIDTH), lambda i: (i, 0)),
                  pl.BlockSpec((HALO, POOL_WIDTH), lambda i: (jnp.maximum(i * halo_blocks - 1, 0), 0)),
                  pl.BlockSpec((tm, D_MODEL), lambda i: (i, 0)),
                  const(SSM_WIDTH, SSM_WIDTH), const(1, SSM_WIDTH),
                  const(len(POOL_WINDOWS), POOL_GROUP, POOL_GROUP), const(1, POOL_WIDTH),
                  const(1, POOL_WIDTH),
                  const(SSM_WIDTH, D_MODEL), const(POOL_WIDTH, D_MODEL)],
        out_specs=pl.BlockSpec((tm, D_MODEL), lambda i: (i, 0)),
        out_shape=jax.ShapeDtypeStruct((L, D_MODEL), F32),
        scratch_shapes=[pltpu.VMEM((HALO + tm, POOL_WIDTH), F32)],
        compiler_params=pltpu.CompilerParams(dimension_semantics=("parallel",),
                                             vmem_limit_bytes=VMEM_LIMIT_BYTES),
        name="out_proj",
    )(y, v, v, x, w_glu, b_glu, w_pool, b_pool, pool_scale, w_top, w_bot)


def _ffn_kernel(x_ref, gain_ref, wg_ref, wu_ref, wd_ref, gfin_ref, o_ref, h_ref):
    j = pl.program_id(1)

    @pl.when(j == 0)
    def _():
        x = x_ref[...]
        h_ref[...] = _rms_norm(x, gain_ref[...]).astype(BF16)
        o_ref[...] = x

    h = h_ref[...]
    a = jnp.dot(h, wg_ref[...], preferred_element_type=F32)
    b = jnp.dot(h, wu_ref[...], preferred_element_type=F32)
    ff = (a * (1.0 / (1.0 + jnp.exp(-a))) * b).astype(BF16)
    o_ref[...] += jnp.dot(ff, wd_ref[...], preferred_element_type=F32)

    @pl.when(j == pl.num_programs(1) - 1)
    def _():
        o_ref[...] = _rms_norm(o_ref[...], gfin_ref[...])


def _ffn(x, gain, w_gate, w_up, w_down, gain_final, tm, tf):
    L = x.shape[0]
    d_ff = w_gate.shape[1]
    return pl.pallas_call(
        _ffn_kernel,
        grid=(L // tm, d_ff // tf),
        in_specs=[pl.BlockSpec((tm, D_MODEL), lambda i, j: (i, 0)),
                  pl.BlockSpec((1, D_MODEL), lambda i, j: (0, 0)),
                  pl.BlockSpec((D_MODEL, tf), lambda i, j: (0, j)),
                  pl.BlockSpec((D_MODEL, tf), lambda i, j: (0, j)),
                  pl.BlockSpec((tf, D_MODEL), lambda i, j: (j, 0)),
                  pl.BlockSpec((1, D_MODEL), lambda i, j: (0, 0))],
        out_specs=pl.BlockSpec((tm, D_MODEL), lambda i, j: (i, 0)),
        out_shape=jax.ShapeDtypeStruct((L, D_MODEL), F32),
        scratch_shapes=[pltpu.VMEM((tm, D_MODEL), BF16)],
        compiler_params=pltpu.CompilerParams(dimension_semantics=("parallel", "arbitrary"),
                                             vmem_limit_bytes=VMEM_LIMIT_BYTES),
        name="ffn",
    )(x, gain, w_gate, w_up, w_down, gain_final)


def kernel(x, norm_mix, w_in, lambda_re, lambda_im, log_step, b_re, b_im, c_re, c_im, d_skip,
           w_glu, b_glu, w_pool, b_pool, pool_scale, w_out, norm_ffn, w_gate, w_up, w_down,
           norm_final):
    bsz, L, D = x.shape
    assert bsz == 1 and D == D_MODEL and w_in.shape[0] == 1 and L % 1024 == 0
    G, P, H = SSM_GROUPS, SSM_STATE, SSM_GROUP
    n_chunks = L // CHUNK
    xs = x.reshape(L, D).astype(F32)

    m, bw, cw = _ssm_prep(lambda_re[0].astype(F32), lambda_im[0].astype(F32), log_step[0].astype(F32),
                          b_re[0].astype(F32), b_im[0].astype(F32), c_re[0].astype(F32),
                          c_im[0].astype(F32), d_skip[0].astype(F32))
    bw_pair, cw_pair = _pair_blocks(bw), _pair_blocks(cw)
    lre_row = lambda_re[0].astype(F32).reshape(G // 2, 1, 2 * P)
    lim_row = lambda_im[0].astype(F32).reshape(G // 2, 1, 2 * P)
    ls_row = jnp.repeat(log_step[0].astype(F32), P).reshape(G // 2, 1, 2 * P)

    u, v = _in_proj(xs, norm_mix[0].astype(F32).reshape(1, D), w_in[0].astype(BF16), tm=512)

    y = _ssm(u, m, bw_pair, cw_pair, lre_row, lim_row, ls_row)

    w_out_b = w_out[0].astype(BF16)
    x1 = _out_proj(y, v, xs, w_glu[0].astype(BF16), b_glu[0].astype(F32).reshape(1, SSM_WIDTH),
                   w_pool[0].astype(BF16), b_pool[0].astype(F32).reshape(1, POOL_WIDTH),
                   pool_scale[0].astype(F32).reshape(1, POOL_WIDTH),
                   w_out_b[:SSM_WIDTH], w_out_b[SSM_WIDTH:], tm=512)

    out = _ffn(x1, norm_ffn[0].astype(F32).reshape(1, D), w_gate[0].astype(BF16),
               w_up[0].astype(BF16), w_down[0].astype(BF16),
               norm_final.astype(F32).reshape(1, D), tm=1024, tf=512)
    return out.reshape(bsz, L, D).astype(x.dtype)
```

```python
import math

import jax
import jax.numpy as jnp
from jax import lax
from jax.experimental import pallas as pl
from jax.experimental.pallas import tpu as pltpu

F32 = jnp.float32
BF16 = jnp.bfloat16

D_MODEL = 2048
SSM_WIDTH = 1024
POOL_WIDTH = 1024
SSM_GROUP = 16
SSM_GROUPS = 64
SSM_STATE = 64
POOL_WINDOWS = (2, 4, 8, 16)
POOL_GROUP = 256
EPS = 1e-6

CHUNK = 16
CHUNK_LANES = CHUNK * SSM_GROUP
HALO = 16
LANES = 128
GROUPS_PER_BLOCK = LANES // SSM_GROUP
RELAYOUT_ROWS = 32

VMEM_LIMIT_BYTES = 60 * 1024 * 1024

HIGHEST = lax.Precision.HIGHEST


def _rms_norm(x, gain):
    ms = jnp.mean(x * x, axis=-1, keepdims=True)
    return x * lax.rsqrt(ms + EPS) * gain


def _ssm_prep_kernel(lre_ref, lim_ref, ls_ref, b_re_ref, b_im_ref, bt_re_ref, bt_im_ref,
                     ct_re_ref, ct_im_ref, d_ref, m_ref, bw_ref, cw_ref):
    P, H, T, GB = SSM_STATE, SSM_GROUP, CHUNK, GROUPS_PER_BLOCK
    lre, lim = lre_ref[...], lim_ref[...]
    step = jnp.exp(ls_ref[...])
    mag = jnp.exp(lre * step)
    lbr, lbi = mag * jnp.cos(lim * step), mag * jnp.sin(lim * step)
    den = lre * lre + lim * lim
    cfr = ((lbr - 1.0) * lre + lbi * lim) / den
    cfi = (lbi * lre - (lbr - 1.0) * lim) / den
    eye = (lax.broadcasted_iota(jnp.int32, (P, P), 0)
           == lax.broadcasted_iota(jnp.int32, (P, P), 1)).astype(F32)
    rows = jnp.concatenate([lbr, lbi, cfr, cfi], axis=0)
    cols = lax.dot_general(eye, rows, (((1,), (1,)), ((), ())), precision=HIGHEST,
                           preferred_element_type=F32)

    lane = lax.broadcasted_iota(jnp.int32, (1, CHUNK_LANES), 1)
    k_idx = lane // H
    sub = lax.broadcasted_iota(jnp.int32, (H, CHUNK_LANES), 0)
    lane_h = lax.broadcasted_iota(jnp.int32, (H, CHUNK_LANES), 1)
    rep = (lane_h % H == sub).astype(F32)

    def tile(a):
        return jnp.dot(a, rep, precision=HIGHEST, preferred_element_type=F32)

    def cmul(ar, ai, br, bi):
        return ar * br - ai * bi, ar * bi + ai * br

    bw_ref[...] = jnp.zeros_like(bw_ref)
    cw_ref[...] = jnp.zeros_like(cw_ref)
    for g in range(GB):
        col = lambda quantity: cols[:, quantity * GB + g:quantity * GB + g + 1]
        l1r, l1i, cr, ci = col(0), col(1), col(2), col(3)
        sq = [(l1r, l1i)]
        for _ in range(3):
            sq.append(cmul(*sq[-1], *sq[-1]))
        pkr = pki = prr = pri = None
        for bit, (fr, fi) in enumerate(sq):
            on = ((k_idx >> bit) & 1) == 1
            ur, ui = jnp.where(on, fr, 1.0), jnp.where(on, fi, 0.0)
            dr, di = jnp.where(on, 1.0, fr), jnp.where(on, 0.0, fi)
            if bit == 0:
                pkr, pki, prr, pri = ur, ui, dr, di
            else:
                pkr, pki = cmul(pkr, pki, ur, ui)
                prr, pri = cmul(prr, pri, dr, di)

        bb_re, bb_im = cmul(cr, ci, b_re_ref[g], b_im_ref[g])
        bbt_re, bbt_im = cmul(cfr[g:g + 1, :], cfi[g:g + 1, :], bt_re_ref[g], bt_im_ref[g])

        q, j = g // 2, g % 2
        r_re = slice(j * P, (j + 1) * P)
        r_im = slice((2 + j) * P, (3 + j) * P)
        c_g = slice(j * CHUNK_LANES, (j + 1) * CHUNK_LANES)
        w_re, w_im = cmul(prr, pri, tile(bb_re), tile(bb_im))
        bw_ref[q, r_re, c_g] = w_re.astype(bw_ref.dtype)
        bw_ref[q, r_im, c_g] = w_im.astype(bw_ref.dtype)
        g_re, g_im = cmul(pkr, pki, tile(ct_re_ref[g]), tile(ct_im_ref[g]))
        g1_re, g1_im = cmul(l1r, l1i, g_re, g_im)
        cw_ref[q, r_re, c_g] = g1_re.astype(cw_ref.dtype)
        cw_ref[q, r_im, c_g] = (-g1_im).astype(cw_ref.dtype)
        kt = (jnp.dot(bbt_re, g_re, precision=HIGHEST, preferred_element_type=F32)
              - jnp.dot(bbt_im, g_im, precision=HIGHEST, preferred_element_type=F32))
        kt = kt + jnp.where(lane_h == sub, d_ref[g], 0.0)
        for tau in range(T):
            blk = kt if tau == 0 else jnp.where(lane_h >= tau * H, pltpu.roll(kt, tau * H, axis=1), 0.0)
            m_ref[g, tau * H:(tau + 1) * H, :] = blk.astype(m_ref.dtype)


def _ssm_prep(lambda_re, lambda_im, log_step, b_re, b_im, c_re, c_im, d_skip):
    G, P, H, GB = SSM_GROUPS, SSM_STATE, SSM_GROUP, GROUPS_PER_BLOCK
    blk = lambda *shape: pl.BlockSpec(shape, lambda i: (i,) + (0,) * (len(shape) - 1))
    return pl.pallas_call(
        _ssm_prep_kernel,
        grid=(G // GB,),
        in_specs=[blk(GB, P), blk(GB, P), blk(GB, 1),
                  blk(GB, P, H), blk(GB, P, H), blk(GB, H, P), blk(GB, H, P),
                  blk(GB, P, H), blk(GB, P, H), blk(GB, H, 1)],
        out_specs=[blk(GB, CHUNK_LANES, CHUNK_LANES), blk(GB // 2, 4 * P, 2 * CHUNK_LANES),
                   blk(GB // 2, 4 * P, 2 * CHUNK_LANES)],
        out_shape=[jax.ShapeDtypeStruct((G, CHUNK_LANES, CHUNK_LANES), BF16),
                   jax.ShapeDtypeStruct((G // 2, 4 * P, 2 * CHUNK_LANES), BF16),
                   jax.ShapeDtypeStruct((G // 2, 4 * P, 2 * CHUNK_LANES), BF16)],
        compiler_params=pltpu.CompilerParams(dimension_semantics=("parallel",)),
        name="ssm_prep",
    )(lambda_re, lambda_im, log_step.reshape(G, 1),
      b_re, b_im, jnp.swapaxes(b_re, 1, 2), jnp.swapaxes(b_im, 1, 2),
      jnp.swapaxes(c_re, 1, 2), jnp.swapaxes(c_im, 1, 2), d_skip.reshape(G, H, 1))


def _in_proj_kernel(x_ref, gain_ref, w_ref, u_ref, v_ref):
    h = _rms_norm(x_ref[...], gain_ref[...]).astype(BF16)
    p = jnp.dot(h, w_ref[...], preferred_element_type=F32)
    u_ref[...] = p[:, :SSM_WIDTH].astype(u_ref.dtype)
    v_ref[...] = p[:, SSM_WIDTH:].astype(v_ref.dtype)


def _in_proj(x, gain, w_in, tm):
    L = x.shape[0]
    return pl.pallas_call(
        _in_proj_kernel,
        grid=(L // tm,),
        in_specs=[pl.BlockSpec((tm, D_MODEL), lambda i: (i, 0)),
                  pl.BlockSpec((1, D_MODEL), lambda i: (0, 0)),
                  pl.BlockSpec((D_MODEL, D_MODEL), lambda i: (0, 0))],
        out_specs=[pl.BlockSpec((tm, SSM_WIDTH), lambda i: (i, 0)),
                   pl.BlockSpec((tm, POOL_WIDTH), lambda i: (i, 0))],
        out_shape=[jax.ShapeDtypeStruct((L, SSM_WIDTH), F32),
                   jax.ShapeDtypeStruct((L, POOL_WIDTH), BF16)],
        compiler_params=pltpu.CompilerParams(dimension_semantics=("parallel",),
                                             vmem_limit_bytes=VMEM_LIMIT_BYTES),
        name="in_proj",
    )(x, gain, w_in)


def _granule_transpose(a):
    lane = lax.broadcasted_iota(jnp.int32, a[0].shape, 1)
    a = list(a)
    for s in (4, 2, 1):
        upper = ((lane // SSM_GROUP) & s) != 0
        nxt = list(a)
        for i in range(GROUPS_PER_BLOCK):
            if i & s:
                continue
            lo, hi = a[i], a[i + s]
            nxt[i] = jnp.where(upper, pltpu.roll(hi, s * SSM_GROUP, axis=1), lo)
            nxt[i + s] = jnp.where(upper, hi, pltpu.roll(lo, LANES - s * SSM_GROUP, axis=1))
        a = nxt
    return a


def _ssm_kernel(u_ref, m_ref, bw_ref, cw_ref, lre_ref, lim_ref, ls_ref, y_ref, xg_ref, yg_ref):
    n_chunks = xg_ref.shape[1]
    half = CHUNK // 2
    rb = RELAYOUT_ROWS

    def relayout_in(cb, carry):
        r0 = pl.multiple_of(cb * rb, rb)
        for hf in range(2):
            a = [u_ref[pl.ds(r0 * CHUNK + hf * half + t, rb, stride=CHUNK), :].astype(BF16)
                 for t in range(half)]
            b = _granule_transpose(a)
            for g in range(GROUPS_PER_BLOCK):
                xg_ref[g, pl.ds(r0, rb), hf * LANES:(hf + 1) * LANES] = b[g]
        return carry

    lax.fori_loop(0, n_chunks // rb, relayout_in, 0, unroll=2)

    row = lax.broadcasted_iota(jnp.int32, (n_chunks, 2 * SSM_STATE), 0)
    for q in range(GROUPS_PER_BLOCK // 2):
        x0, x1 = xg_ref[2 * q], xg_ref[2 * q + 1]
        xp = jnp.concatenate([x0, x1], axis=1)
        e = lax.dot_general(xp, bw_ref[q], (((1,), (1,)), ((), ())), preferred_element_type=F32)
        sr = jnp.where(row >= 1, pltpu.roll(e[:, :2 * SSM_STATE], 1, axis=0), 0.0)
        si = jnp.where(row >= 1, pltpu.roll(e[:, 2 * SSM_STATE:], 1, axis=0), 0.0)
        step = jnp.exp(ls_ref[q])
        zr, zi = lre_ref[q] * step * CHUNK, lim_ref[q] * step * CHUNK
        mag = jnp.exp(zr)
        ar, ai = mag * jnp.cos(zi), mag * jnp.sin(zi)
        d = 1
        while d < n_chunks:
            tr = jnp.where(row >= d, pltpu.roll(sr, d, axis=0), 0.0)
            ti = jnp.where(row >= d, pltpu.roll(si, d, axis=0), 0.0)
            sr, si = sr + ar * tr - ai * ti, si + ar * ti + ai * tr
            ar, ai = ar * ar - ai * ai, 2.0 * ar * ai
            d *= 2
        s = jnp.concatenate([sr, si], axis=1).astype(BF16)
        yc = jnp.dot(s, cw_ref[q], preferred_element_type=F32)
        y0 = jnp.dot(x0, m_ref[2 * q], preferred_element_type=F32) + yc[:, :CHUNK_LANES]
        y1 = jnp.dot(x1, m_ref[2 * q + 1], preferred_element_type=F32) + yc[:, CHUNK_LANES:]
        yg_ref[2 * q] = y0.astype(yg_ref.dtype)
        yg_ref[2 * q + 1] = y1.astype(yg_ref.dtype)

    def relayout_out(cb, carry):
        r0 = pl.multiple_of(cb * rb, rb)
        for hf in range(2):
            b = [yg_ref[g, pl.ds(r0, rb), hf * LANES:(hf + 1) * LANES] for g in range(GROUPS_PER_BLOCK)]
            a = _granule_transpose(b)
            for t in range(half):
                y_ref[pl.ds(r0 * CHUNK + hf * half + t, rb, stride=CHUNK), :] = a[t].astype(F32)
        return carry

    lax.fori_loop(0, n_chunks // rb, relayout_out, 0, unroll=2)


def _ssm(u, m, bw_pair, cw_pair, lre_row, lim_row, ls_row):
    L = u.shape[0]
    n_chunks = L // CHUNK
    n_blocks = SSM_WIDTH // LANES
    ppb = GROUPS_PER_BLOCK // 2
    return pl.pallas_call(
        _ssm_kernel,
        grid=(n_blocks,),
        in_specs=[pl.BlockSpec((L, LANES), lambda b: (0, b)),
                  pl.BlockSpec((GROUPS_PER_BLOCK, CHUNK_LANES, CHUNK_LANES), lambda b: (b, 0, 0)),
                  pl.BlockSpec((ppb, 4 * SSM_STATE, 2 * CHUNK_LANES), lambda b: (b, 0, 0)),
                  pl.BlockSpec((ppb, 4 * SSM_STATE, 2 * CHUNK_LANES), lambda b: (b, 0, 0)),
                  pl.BlockSpec((ppb, 1, 2 * SSM_STATE), lambda b: (b, 0, 0)),
                  pl.BlockSpec((ppb, 1, 2 * SSM_STATE), lambda b: (b, 0, 0)),
                  pl.BlockSpec((ppb, 1, 2 * SSM_STATE), lambda b: (b, 0, 0))],
        out_specs=pl.BlockSpec((L, LANES), lambda b: (0, b)),
        out_shape=jax.ShapeDtypeStruct((L, SSM_WIDTH), F32),
        scratch_shapes=[pltpu.VMEM((GROUPS_PER_BLOCK, n_chunks, CHUNK_LANES), BF16),
                        pltpu.VMEM((GROUPS_PER_BLOCK, n_chunks, CHUNK_LANES), BF16)],
        compiler_params=pltpu.CompilerParams(dimension_semantics=("parallel",),
                                             vmem_limit_bytes=VMEM_LIMIT_BYTES),
        name="ssm",
    )(u, m, bw_pair, cw_pair, lre_row, lim_row, ls_row)


def _out_proj_kernel(y_ref, v_ref, vh_ref, x_ref, wglu_ref, bglu_ref, wpool_ref, bpool_ref,
                     pscale_ref, wtop_ref, wbot_ref, o_ref, vbuf_ref):
    i = pl.program_id(0)
    tm = y_ref.shape[0]
    y = y_ref[...].astype(F32)
    g = 0.5 * y * (1.0 + lax.erf(y * (1.0 / math.sqrt(2.0))))
    z = jnp.dot(g.astype(BF16), wglu_ref[...], preferred_element_type=F32) + bglu_ref[...]
    ys = g * (1.0 / (1.0 + jnp.exp(-z)))

    vbuf_ref[0:HALO, :] = jnp.where(i > 0, vh_ref[...].astype(F32), 0.0)
    vbuf_ref[HALO:, :] = v_ref[...].astype(F32)
    t1 = (lax.broadcasted_iota(jnp.int32, (tm, 1), 0) + (i * tm + 1)).astype(F32)
    pooled = []
    for k, w in enumerate(POOL_WINDOWS):
        cols = slice(k * POOL_GROUP, (k + 1) * POOL_GROUP)
        vk = vbuf_ref[:, cols]
        acc, span = vk, 1
        while span < w:
            acc = acc + pltpu.roll(acc, span, axis=0)
            span *= 2
        mean = acc[HALO:, :] / jnp.minimum(t1, float(w))
        pk = (mean - vk[HALO:, :]).astype(BF16)
        mk = jnp.dot(pk, wpool_ref[k], preferred_element_type=F32)
        pooled.append(mk)
    yp = (jnp.concatenate(pooled, axis=1) + bpool_ref[...]) * pscale_ref[...]

    o_ref[...] = (x_ref[...]
                  + jnp.dot(ys.astype(BF16), wtop_ref[...], preferred_element_type=F32)
                  + jnp.dot(yp.astype(BF16), wbot_ref[...], preferred_element_type=F32))


def _out_proj(y, v, x, w_glu, b_glu, w_pool, b_pool, pool_scale, w_out, tm):
    L = x.shape[0]
    halo_blocks = tm // HALO
    const = lambda *shape: pl.BlockSpec(shape, lambda i: (0,) * len(shape))
    return pl.pallas_call(
        _out_proj_kernel,
        grid=(L // tm,),
        in_specs=[pl.BlockSpec((tm, SSM_WIDTH), lambda i: (i, 0)),
                  pl.BlockSpec((tm, POOL_WIDTH), lambda i: (i, 0)),
                  pl.BlockSpec((HALO, POOL_WIDTH), lambda i: (jnp.maximum(i * halo_blocks - 1, 0), 0)),
                  pl.BlockSpec((tm, D_MODEL), lambda i: (i, 0)),
                  const(SSM_WIDTH, SSM_WIDTH), const(1, SSM_WIDTH),
                  const(len(POOL_WINDOWS), POOL_GROUP, POOL_GROUP), const(1, POOL_WIDTH),
                  const(1, POOL_WIDTH),
                  pl.BlockSpec((SSM_WIDTH, D_MODEL), lambda i: (0, 0)),
                  pl.BlockSpec((POOL_WIDTH, D_MODEL), lambda i: (1, 0))],
        out_specs=pl.BlockSpec((tm, D_MODEL), lambda i: (i, 0)),
        out_shape=jax.ShapeDtypeStruct((L, D_MODEL), F32),
        scratch_shapes=[pltpu.VMEM((HALO + tm, POOL_WIDTH), F32)],
        compiler_params=pltpu.CompilerParams(dimension_semantics=("parallel",),
                                             vmem_limit_bytes=VMEM_LIMIT_BYTES),
        name="out_proj",
    )(y, v, v, x, w_glu, b_glu, w_pool, b_pool, pool_scale, w_out, w_out)


def _ffn_kernel(x_ref, gain_ref, wg_ref, wu_ref, wd_ref, gfin_ref, o_ref, h_ref):
    j = pl.program_id(1)

    @pl.when(j == 0)
    def _():
        x = x_ref[...]
        h_ref[...] = _rms_norm(x, gain_ref[...]).astype(BF16)
        o_ref[...] = x

    h = h_ref[...]
    a = jnp.dot(h, wg_ref[...], preferred_element_type=F32)
    b = jnp.dot(h, wu_ref[...], preferred_element_type=F32)
    ff = (a * (1.0 / (1.0 + jnp.exp(-a))) * b).astype(BF16)
    o_ref[...] += jnp.dot(ff, wd_ref[...], preferred_element_type=F32)

    @pl.when(j == pl.num_programs(1) - 1)
    def _():
        o_ref[...] = _rms_norm(o_ref[...], gfin_ref[...])


def _ffn(x, gain, w_gate, w_up, w_down, gain_final, tm, tf):
    L = x.shape[0]
    d_ff = w_gate.shape[1]
    return pl.pallas_call(
        _ffn_kernel,
        grid=(L // tm, d_ff // tf),
        in_specs=[pl.BlockSpec((tm, D_MODEL), lambda i, j: (i, 0)),
                  pl.BlockSpec((1, D_MODEL), lambda i, j: (0, 0)),
                  pl.BlockSpec((D_MODEL, tf), lambda i, j: (0, j)),
                  pl.BlockSpec((D_MODEL, tf), lambda i, j: (0, j)),
                  pl.BlockSpec((tf, D_MODEL), lambda i, j: (j, 0)),
                  pl.BlockSpec((1, D_MODEL), lambda i, j: (0, 0))],
        out_specs=pl.BlockSpec((tm, D_MODEL), lambda i, j: (i, 0)),
        out_shape=jax.ShapeDtypeStruct((L, D_MODEL), F32),
        scratch_shapes=[pltpu.VMEM((tm, D_MODEL), BF16)],
        compiler_params=pltpu.CompilerParams(dimension_semantics=("parallel", "arbitrary"),
                                             vmem_limit_bytes=VMEM_LIMIT_BYTES),
        name="ffn",
    )(x, gain, w_gate, w_up, w_down, gain_final)


def kernel(x, norm_mix, w_in, lambda_re, lambda_im, log_step, b_re, b_im, c_re, c_im, d_skip,
           w_glu, b_glu, w_pool, b_pool, pool_scale, w_out, norm_ffn, w_gate, w_up, w_down,
           norm_final):
    bsz, L, D = x.shape
    assert bsz == 1 and D == D_MODEL and w_in.shape[0] == 1 and L % 1024 == 0
    G, P = SSM_GROUPS, SSM_STATE
    xs = x.reshape(L, D).astype(F32)

    m, bw_pair, cw_pair = _ssm_prep(
        lambda_re[0].astype(F32), lambda_im[0].astype(F32), log_step[0].astype(F32),
        b_re[0].astype(F32), b_im[0].astype(F32), c_re[0].astype(F32), c_im[0].astype(F32),
        d_skip[0].astype(F32))
    lre_row = lambda_re[0].astype(F32).reshape(G // 2, 1, 2 * P)
    lim_row = lambda_im[0].astype(F32).reshape(G // 2, 1, 2 * P)
    ls_row = jnp.repeat(log_step[0].astype(F32), P).reshape(G // 2, 1, 2 * P)

    u, v = _in_proj(xs, norm_mix[0].astype(F32).reshape(1, D), w_in[0].astype(BF16), tm=512)

    y = _ssm(u, m, bw_pair, cw_pair, lre_row, lim_row, ls_row)

    x1 = _out_proj(y, v, xs, w_glu[0].astype(BF16), b_glu[0].astype(F32).reshape(1, SSM_WIDTH),
                   w_pool[0].astype(BF16), b_pool[0].astype(F32).reshape(1, POOL_WIDTH),
                   pool_scale[0].astype(F32).reshape(1, POOL_WIDTH),
                   w_out[0].astype(BF16), tm=512)

    out = _ffn(x1, norm_ffn[0].astype(F32).reshape(1, D), w_gate[0].astype(BF16),
               w_up[0].astype(BF16), w_down[0].astype(BF16),
               norm_final.astype(F32).reshape(1, D), tm=1024, tf=512)
    return out.reshape(bsz, L, D).astype(x.dtype)
```

```python
import math

import jax
import jax.numpy as jnp
from jax import lax
from jax.experimental import pallas as pl
from jax.experimental.pallas import tpu as pltpu

F32 = jnp.float32
BF16 = jnp.bfloat16

D_MODEL = 2048
SSM_WIDTH = 1024
POOL_WIDTH = 1024
SSM_GROUP = 16
SSM_GROUPS = 64
SSM_STATE = 64
POOL_WINDOWS = (2, 4, 8, 16)
POOL_GROUP = 256
EPS = 1e-6

CHUNK = 16
CHUNK_LANES = CHUNK * SSM_GROUP
HALO = 16
LANES = 128
GROUPS_PER_BLOCK = LANES // SSM_GROUP
RELAYOUT_ROWS = 16
RELAYOUT_UNITS = 4
SCAN_BLOCK = 8

VMEM_LIMIT_BYTES = 60 * 1024 * 1024

HIGHEST = lax.Precision.HIGHEST


def _rms_norm(x, gain):
    ms = jnp.mean(x * x, axis=-1, keepdims=True)
    return x * lax.rsqrt(ms + EPS) * gain


def _ssm_prep_kernel(lre_ref, lim_ref, ls_ref, b_re_ref, b_im_ref, bt_re_ref, bt_im_ref,
                     ct_re_ref, ct_im_ref, d_ref, m_ref, bw_ref, cw_ref):
    P, H, T, GB = SSM_STATE, SSM_GROUP, CHUNK, GROUPS_PER_BLOCK
    lre, lim = lre_ref[...], lim_ref[...]
    step = jnp.exp(ls_ref[...])
    mag = jnp.exp(lre * step)
    lbr, lbi = mag * jnp.cos(lim * step), mag * jnp.sin(lim * step)
    den = lre * lre + lim * lim
    cfr = ((lbr - 1.0) * lre + lbi * lim) / den
    cfi = (lbi * lre - (lbr - 1.0) * lim) / den
    eye = (lax.broadcasted_iota(jnp.int32, (P, P), 0)
           == lax.broadcasted_iota(jnp.int32, (P, P), 1)).astype(F32)
    rows = jnp.concatenate([lbr, lbi, cfr, cfi], axis=0)
    cols = lax.dot_general(eye, rows, (((1,), (1,)), ((), ())), precision=HIGHEST,
                           preferred_element_type=F32)

    lane = lax.broadcasted_iota(jnp.int32, (1, CHUNK_LANES), 1)
    k_idx = lane // H
    sub = lax.broadcasted_iota(jnp.int32, (H, CHUNK_LANES), 0)
    lane_h = lax.broadcasted_iota(jnp.int32, (H, CHUNK_LANES), 1)
    rep = (lane_h % H == sub).astype(F32)

    def tile(a):
        return jnp.dot(a, rep, precision=HIGHEST, preferred_element_type=F32)

    def cmul(ar, ai, br, bi):
        return ar * br - ai * bi, ar * bi + ai * br

    bw_ref[...] = jnp.zeros_like(bw_ref)
    cw_ref[...] = jnp.zeros_like(cw_ref)
    for g in range(GB):
        col = lambda quantity: cols[:, quantity * GB + g:quantity * GB + g + 1]
        l1r, l1i, cr, ci = col(0), col(1), col(2), col(3)
        sq = [(l1r, l1i)]
        for _ in range(3):
            sq.append(cmul(*sq[-1], *sq[-1]))
        pkr = pki = prr = pri = None
        for bit, (fr, fi) in enumerate(sq):
            on = ((k_idx >> bit) & 1) == 1
            ur, ui = jnp.where(on, fr, 1.0), jnp.where(on, fi, 0.0)
            dr, di = jnp.where(on, 1.0, fr), jnp.where(on, 0.0, fi)
            if bit == 0:
                pkr, pki, prr, pri = ur, ui, dr, di
            else:
                pkr, pki = cmul(pkr, pki, ur, ui)
                prr, pri = cmul(prr, pri, dr, di)

        bb_re, bb_im = cmul(cr, ci, b_re_ref[g], b_im_ref[g])
        bbt_re, bbt_im = cmul(cfr[g:g + 1, :], cfi[g:g + 1, :], bt_re_ref[g], bt_im_ref[g])

        q, j = g // 2, g % 2
        r_re = slice(j * P, (j + 1) * P)
        r_im = slice((2 + j) * P, (3 + j) * P)
        c_g = slice(j * CHUNK_LANES, (j + 1) * CHUNK_LANES)
        w_re, w_im = cmul(prr, pri, tile(bb_re), tile(bb_im))
        bw_ref[q, r_re, c_g] = w_re.astype(bw_ref.dtype)
        bw_ref[q, r_im, c_g] = w_im.astype(bw_ref.dtype)
        g_re, g_im = cmul(pkr, pki, tile(ct_re_ref[g]), tile(ct_im_ref[g]))
        g1_re, g1_im = cmul(l1r, l1i, g_re, g_im)
        cw_ref[q, r_re, c_g] = g1_re.astype(cw_ref.dtype)
        cw_ref[q, r_im, c_g] = (-g1_im).astype(cw_ref.dtype)
        kt = (jnp.dot(bbt_re, g_re, precision=HIGHEST, preferred_element_type=F32)
              - jnp.dot(bbt_im, g_im, precision=HIGHEST, preferred_element_type=F32))
        kt = kt + jnp.where(lane_h == sub, d_ref[g], 0.0)
        for tau in range(T):
            blk = kt if tau == 0 else jnp.where(lane_h >= tau * H, pltpu.roll(kt, tau * H, axis=1), 0.0)
            m_ref[g, tau * H:(tau + 1) * H, :] = blk.astype(m_ref.dtype)


def _ssm_prep(lambda_re, lambda_im, log_step, b_re, b_im, c_re, c_im, d_skip):
    G, P, H, GB = SSM_GROUPS, SSM_STATE, SSM_GROUP, GROUPS_PER_BLOCK
    blk = lambda *shape: pl.BlockSpec(shape, lambda i: (i,) + (0,) * (len(shape) - 1))
    return pl.pallas_call(
        _ssm_prep_kernel,
        grid=(G // GB,),
        in_specs=[blk(GB, P), blk(GB, P), blk(GB, 1),
                  blk(GB, P, H), blk(GB, P, H), blk(GB, H, P), blk(GB, H, P),
                  blk(GB, P, H), blk(GB, P, H), blk(GB, H, 1)],
        out_specs=[blk(GB, CHUNK_LANES, CHUNK_LANES), blk(GB // 2, 4 * P, 2 * CHUNK_LANES),
                   blk(GB // 2, 4 * P, 2 * CHUNK_LANES)],
        out_shape=[jax.ShapeDtypeStruct((G, CHUNK_LANES, CHUNK_LANES), BF16),
                   jax.ShapeDtypeStruct((G // 2, 4 * P, 2 * CHUNK_LANES), BF16),
                   jax.ShapeDtypeStruct((G // 2, 4 * P, 2 * CHUNK_LANES), BF16)],
        compiler_params=pltpu.CompilerParams(dimension_semantics=("parallel",)),
        name="ssm_prep",
    )(lambda_re, lambda_im, log_step.reshape(G, 1),
      b_re, b_im, jnp.swapaxes(b_re, 1, 2), jnp.swapaxes(b_im, 1, 2),
      jnp.swapaxes(c_re, 1, 2), jnp.swapaxes(c_im, 1, 2), d_skip.reshape(G, H, 1))


def _in_proj_kernel(x_ref, gain_ref, w_ref, u_ref, v_ref):
    h = _rms_norm(x_ref[...], gain_ref[...]).astype(BF16)
    p = jnp.dot(h, w_ref[...], preferred_element_type=F32)
    u_ref[...] = p[:, :SSM_WIDTH].astype(u_ref.dtype)
    v_ref[...] = p[:, SSM_WIDTH:].astype(v_ref.dtype)


def _in_proj(x, gain, w_in, tm):
    L = x.shape[0]
    return pl.pallas_call(
        _in_proj_kernel,
        grid=(L // tm,),
        in_specs=[pl.BlockSpec((tm, D_MODEL), lambda i: (i, 0)),
                  pl.BlockSpec((1, D_MODEL), lambda i: (0, 0)),
                  pl.BlockSpec((D_MODEL, D_MODEL), lambda i: (0, 0))],
        out_specs=[pl.BlockSpec((tm, SSM_WIDTH), lambda i: (i, 0)),
                   pl.BlockSpec((tm, POOL_WIDTH), lambda i: (i, 0))],
        out_shape=[jax.ShapeDtypeStruct((L, SSM_WIDTH), F32),
                   jax.ShapeDtypeStruct((L, POOL_WIDTH), BF16)],
        compiler_params=pltpu.CompilerParams(dimension_semantics=("parallel",),
                                             vmem_limit_bytes=VMEM_LIMIT_BYTES),
        name="in_proj",
    )(x, gain, w_in)


def _granule_transpose(units):
    lane = lax.broadcasted_iota(jnp.int32, units[0][0].shape, 1)
    units = [list(a) for a in units]
    for s in (4, 2, 1):
        upper = ((lane // SSM_GROUP) & s) != 0
        for a in units:
            for i in range(GROUPS_PER_BLOCK):
                if i & s:
                    continue
                lo, hi = a[i], a[i + s]
                a[i] = jnp.where(upper, pltpu.roll(hi, s * SSM_GROUP, axis=1), lo)
                a[i + s] = jnp.where(upper, hi, pltpu.roll(lo, LANES - s * SSM_GROUP, axis=1))
    return units


def _cmul(ar, ai, br, bi):
    return ar * br - ai * bi, ar * bi + ai * br


def _ssm_kernel(u_ref, m_ref, bw_ref, cw_ref, lre_ref, lim_ref, ls_ref, y_ref,
                xg_ref, yg_ref, fr_ref, fi_ref, cr_ref, ci_ref):
    n_chunks = xg_ref.shape[1]
    half = CHUNK // 2
    rb = RELAYOUT_ROWS
    span = RELAYOUT_UNITS * rb
    sb = SCAN_BLOCK
    n_sb = n_chunks // sb

    def relayout_in(it, carry):
        base = pl.multiple_of(it * span, span)
        units = [[u_ref[pl.ds((base + k * rb) * CHUNK + hf * half + t, rb, stride=CHUNK), :].astype(BF16)
                  for t in range(half)]
                 for k in range(RELAYOUT_UNITS) for hf in range(2)]
        units = _granule_transpose(units)
        for k in range(RELAYOUT_UNITS):
            for hf in range(2):
                for g in range(GROUPS_PER_BLOCK):
                    xg_ref[g, pl.ds(base + k * rb, rb), hf * LANES:(hf + 1) * LANES] = units[2 * k + hf][g]
        return carry

    lax.fori_loop(0, n_chunks // span, relayout_in, 0, unroll=2)

    blk3 = (n_sb, sb, 2 * SSM_STATE)
    row_in = lax.broadcasted_iota(jnp.int32, blk3, 1)
    brow = lax.broadcasted_iota(jnp.int32, (n_sb, 2 * SSM_STATE), 0)
    for q in range(GROUPS_PER_BLOCK // 2):
        x0, x1 = xg_ref[2 * q], xg_ref[2 * q + 1]
        xp = jnp.concatenate([x0, x1], axis=1)
        e = lax.dot_general(xp, bw_ref[q], (((1,), (1,)), ((), ())), preferred_element_type=F32)
        step = jnp.exp(ls_ref[q])
        zr, zi = lre_ref[q] * step * CHUNK, lim_ref[q] * step * CHUNK
        mag = jnp.exp(zr)
        a1r, a1i = mag * jnp.cos(zi), mag * jnp.sin(zi)

        fr = e[:, :2 * SSM_STATE].reshape(blk3)
        fi = e[:, 2 * SSM_STATE:].reshape(blk3)
        ar, ai = a1r, a1i
        d = 1
        while d < sb:
            tr = jnp.where(row_in >= d, pltpu.roll(fr, d, axis=1), 0.0)
            ti = jnp.where(row_in >= d, pltpu.roll(fi, d, axis=1), 0.0)
            fr, fi = fr + ar * tr - ai * ti, fi + ar * ti + ai * tr
            ar, ai = _cmul(ar, ai, ar, ai)
            d *= 2
        fr_ref[...] = fr.reshape(n_chunks, 2 * SSM_STATE)
        fi_ref[...] = fi.reshape(n_chunks, 2 * SSM_STATE)
        gr = fr_ref[pl.ds(sb - 1, n_sb, stride=sb), :]
        gi = fi_ref[pl.ds(sb - 1, n_sb, stride=sb), :]
        gr = jnp.where(brow >= 1, pltpu.roll(gr, 1, axis=0), 0.0)
        gi = jnp.where(brow >= 1, pltpu.roll(gi, 1, axis=0), 0.0)
        d = 1
        while d < n_sb:
            tr = jnp.where(brow >= d, pltpu.roll(gr, d, axis=0), 0.0)
            ti = jnp.where(brow >= d, pltpu.roll(gi, d, axis=0), 0.0)
            gr, gi = gr + ar * tr - ai * ti, gi + ar * ti + ai * tr
            ar, ai = _cmul(ar, ai, ar, ai)
            d *= 2
        pr, pi = jnp.ones_like(a1r), jnp.zeros_like(a1i)
        for r in range(sb):
            cr_ref[pl.ds(r, n_sb, stride=sb), :] = pr * gr - pi * gi
            ci_ref[pl.ds(r, n_sb, stride=sb), :] = pr * gi + pi * gr
            pr, pi = _cmul(pr, pi, a1r, a1i)
        sr = jnp.where(row_in >= 1, pltpu.roll(fr, 1, axis=1), 0.0).reshape(n_chunks, 2 * SSM_STATE)
        si = jnp.where(row_in >= 1, pltpu.roll(fi, 1, axis=1), 0.0).reshape(n_chunks, 2 * SSM_STATE)
        sr = sr + cr_ref[...]
        si = si + ci_ref[...]
        s = jnp.concatenate([sr, si], axis=1).astype(BF16)
        yc = jnp.dot(s, cw_ref[q], preferred_element_type=F32)
        y0 = jnp.dot(x0, m_ref[2 * q], preferred_element_type=F32) + yc[:, :CHUNK_LANES]
        y1 = jnp.dot(x1, m_ref[2 * q + 1], preferred_element_type=F32) + yc[:, CHUNK_LANES:]
        yg_ref[2 * q] = y0.astype(yg_ref.dtype)
        yg_ref[2 * q + 1] = y1.astype(yg_ref.dtype)

    def relayout_out(it, carry):
        base = pl.multiple_of(it * span, span)
        units = [[yg_ref[g, pl.ds(base + k * rb, rb), hf * LANES:(hf + 1) * LANES]
                  for g in range(GROUPS_PER_BLOCK)]
                 for k in range(RELAYOUT_UNITS) for hf in range(2)]
        units = _granule_transpose(units)
        for k in range(RELAYOUT_UNITS):
            for hf in range(2):
                for t in range(half):
                    y_ref[pl.ds((base + k * rb) * CHUNK + hf * half + t, rb, stride=CHUNK), :] = (
                        units[2 * k + hf][t].astype(F32))
        return carry

    lax.fori_loop(0, n_chunks // span, relayout_out, 0, unroll=2)


def _ssm(u, m, bw_pair, cw_pair, lre_row, lim_row, ls_row):
    L = u.shape[0]
    n_chunks = L // CHUNK
    n_blocks = SSM_WIDTH // LANES
    ppb = GROUPS_PER_BLOCK // 2
    return pl.pallas_call(
        _ssm_kernel,
        grid=(n_blocks,),
        in_specs=[pl.BlockSpec((L, LANES), lambda b: (0, b)),
                  pl.BlockSpec((GROUPS_PER_BLOCK, CHUNK_LANES, CHUNK_LANES), lambda b: (b, 0, 0)),
                  pl.BlockSpec((ppb, 4 * SSM_STATE, 2 * CHUNK_LANES), lambda b: (b, 0, 0)),
                  pl.BlockSpec((ppb, 4 * SSM_STATE, 2 * CHUNK_LANES), lambda b: (b, 0, 0)),
                  pl.BlockSpec((ppb, 1, 2 * SSM_STATE), lambda b: (b, 0, 0)),
                  pl.BlockSpec((ppb, 1, 2 * SSM_STATE), lambda b: (b, 0, 0)),
                  pl.BlockSpec((ppb, 1, 2 * SSM_STATE), lambda b: (b, 0, 0))],
        out_specs=pl.BlockSpec((L, LANES), lambda b: (0, b)),
        out_shape=jax.ShapeDtypeStruct((L, SSM_WIDTH), F32),
        scratch_shapes=[pltpu.VMEM((GROUPS_PER_BLOCK, n_chunks, CHUNK_LANES), BF16),
                        pltpu.VMEM((GROUPS_PER_BLOCK, n_chunks, CHUNK_LANES), BF16)]
                       + [pltpu.VMEM((n_chunks, 2 * SSM_STATE), F32)] * 4,
        compiler_params=pltpu.CompilerParams(dimension_semantics=("parallel",),
                                             vmem_limit_bytes=VMEM_LIMIT_BYTES),
        name="ssm",
    )(u, m, bw_pair, cw_pair, lre_row, lim_row, ls_row)


def _out_proj_kernel(y_ref, v_ref, vh_ref, x_ref, wglu_ref, bglu_ref, wpool_ref, bpool_ref,
                     pscale_ref, wtop_ref, wbot_ref, o_ref, vbuf_ref):
    i = pl.program_id(0)
    tm = y_ref.shape[0]
    y = y_ref[...].astype(F32)
    g = 0.5 * y * (1.0 + lax.erf(y * (1.0 / math.sqrt(2.0))))
    z = jnp.dot(g.astype(BF16), wglu_ref[...], preferred_element_type=F32) + bglu_ref[...]
    ys = g * (1.0 / (1.0 + jnp.exp(-z)))

    vbuf_ref[0:HALO, :] = jnp.where(i > 0, vh_ref[...].astype(F32), 0.0)
    vbuf_ref[HALO:, :] = v_ref[...].astype(F32)
    t1 = (lax.broadcasted_iota(jnp.int32, (tm, 1), 0) + (i * tm + 1)).astype(F32)
    pooled = []
    for k, w in enumerate(POOL_WINDOWS):
        cols = slice(k * POOL_GROUP, (k + 1) * POOL_GROUP)
        vk = vbuf_ref[:, cols]
        acc, span = vk, 1
        while span < w:
            acc = acc + pltpu.roll(acc, span, axis=0)
            span *= 2
        mean = acc[HALO:, :] / jnp.minimum(t1, float(w))
        pk = (mean - vk[HALO:, :]).astype(BF16)
        mk = jnp.dot(pk, wpool_ref[k], preferred_element_type=F32)
        pooled.append(mk)
    yp = (jnp.concatenate(pooled, axis=1) + bpool_ref[...]) * pscale_ref[...]

    o_ref[...] = (x_ref[...]
                  + jnp.dot(ys.astype(BF16), wtop_ref[...], preferred_element_type=F32)
                  + jnp.dot(yp.astype(BF16), wbot_ref[...], preferred_element_type=F32))


def _out_proj(y, v, x, w_glu, b_glu, w_pool, b_pool, pool_scale, w_out, tm):
    L = x.shape[0]
    halo_blocks = tm // HALO
    const = lambda *shape: pl.BlockSpec(shape, lambda i: (0,) * len(shape))
    return pl.pallas_call(
        _out_proj_kernel,
        grid=(L // tm,),
        in_specs=[pl.BlockSpec((tm, SSM_WIDTH), lambda i: (i, 0)),
                  pl.BlockSpec((tm, POOL_WIDTH), lambda i: (i, 0)),
                  pl.BlockSpec((HALO, POOL_WIDTH), lambda i: (jnp.maximum(i * halo_blocks - 1, 0), 0)),
                  pl.BlockSpec((tm, D_MODEL), lambda i: (i, 0)),
                  const(SSM_WIDTH, SSM_WIDTH), const(1, SSM_WIDTH),
                  const(len(POOL_WINDOWS), POOL_GROUP, POOL_GROUP), const(1, POOL_WIDTH),
                  const(1, POOL_WIDTH),
                  pl.BlockSpec((SSM_WIDTH, D_MODEL), lambda i: (0, 0)),
                  pl.BlockSpec((POOL_WIDTH, D_MODEL), lambda i: (1, 0))],
        out_specs=pl.BlockSpec((tm, D_MODEL), lambda i: (i, 0)),
        out_shape=jax.ShapeDtypeStruct((L, D_MODEL), F32),
        scratch_shapes=[pltpu.VMEM((HALO + tm, POOL_WIDTH), F32)],
        compiler_params=pltpu.CompilerParams(dimension_semantics=("parallel",),
                                             vmem_limit_bytes=VMEM_LIMIT_BYTES),
        name="out_proj",
    )(y, v, v, x, w_glu, b_glu, w_pool, b_pool, pool_scale, w_out, w_out)


def _ffn_kernel(x_ref, gain_ref, wg_ref, wu_ref, wd_ref, gfin_ref, o_ref, h_ref):
    j = pl.program_id(1)

    @pl.when(j == 0)
    def _():
        x = x_ref[...]
        h_ref[...] = _rms_norm(x, gain_ref[...]).astype(BF16)
        o_ref[...] = x

    h = h_ref[...]
    a = jnp.dot(h, wg_ref[...], preferred_element_type=F32)
    b = jnp.dot(h, wu_ref[...], preferred_element_type=F32)
    ff = (a * (1.0 / (1.0 + jnp.exp(-a))) * b).astype(BF16)
    o_ref[...] += jnp.dot(ff, wd_ref[...], preferred_element_type=F32)

    @pl.when(j == pl.num_programs(1) - 1)
    def _():
        o_ref[...] = _rms_norm(o_ref[...], gfin_ref[...])


def _ffn(x, gain, w_gate, w_up, w_down, gain_final, tm, tf):
    L = x.shape[0]
    d_ff = w_gate.shape[1]
    return pl.pallas_call(
        _ffn_kernel,
        grid=(L // tm, d_ff // tf),
        in_specs=[pl.BlockSpec((tm, D_MODEL), lambda i, j: (i, 0)),
                  pl.BlockSpec((1, D_MODEL), lambda i, j: (0, 0)),
                  pl.BlockSpec((D_MODEL, tf), lambda i, j: (0, j)),
                  pl.BlockSpec((D_MODEL, tf), lambda i, j: (0, j)),
                  pl.BlockSpec((tf, D_MODEL), lambda i, j: (j, 0)),
                  pl.BlockSpec((1, D_MODEL), lambda i, j: (0, 0))],
        out_specs=pl.BlockSpec((tm, D_MODEL), lambda i, j: (i, 0)),
        out_shape=jax.ShapeDtypeStruct((L, D_MODEL), F32),
        scratch_shapes=[pltpu.VMEM((tm, D_MODEL), BF16)],
        compiler_params=pltpu.CompilerParams(dimension_semantics=("parallel", "arbitrary"),
                                             vmem_limit_bytes=VMEM_LIMIT_BYTES),
        name="ffn",
    )(x, gain, w_gate, w_up, w_down, gain_final)


def kernel(x, norm_mix, w_in, lambda_re, lambda_im, log_step, b_re, b_im, c_re, c_im, d_skip,
           w_glu, b_glu, w_pool, b_pool, pool_scale, w_out, norm_ffn, w_gate, w_up, w_down,
           norm_final):
    bsz, L, D = x.shape
    assert bsz == 1 and D == D_MODEL and w_in.shape[0] == 1 and L % 1024 == 0
    G, P = SSM_GROUPS, SSM_STATE
    xs = x.reshape(L, D).astype(F32)

    m, bw_pair, cw_pair = _ssm_prep(
        lambda_re[0].astype(F32), lambda_im[0].astype(F32), log_step[0].astype(F32),
        b_re[0].astype(F32), b_im[0].astype(F32), c_re[0].astype(F32), c_im[0].astype(F32),
        d_skip[0].astype(F32))
    lre_row = lambda_re[0].astype(F32).reshape(G // 2, 1, 2 * P)
    lim_row = lambda_im[0].astype(F32).reshape(G // 2, 1, 2 * P)
    ls_row = jnp.repeat(log_step[0].astype(F32), P).reshape(G // 2, 1, 2 * P)

    u, v = _in_proj(xs, norm_mix[0].astype(F32).reshape(1, D), w_in[0].astype(BF16), tm=512)

    y = _ssm(u, m, bw_pair, cw_pair, lre_row, lim_row, ls_row)

    x1 = _out_proj(y, v, xs, w_glu[0].astype(BF16), b_glu[0].astype(F32).reshape(1, SSM_WIDTH),
                   w_pool[0].astype(BF16), b_pool[0].astype(F32).reshape(1, POOL_WIDTH),
                   pool_scale[0].astype(F32).reshape(1, POOL_WIDTH),
                   w_out[0].astype(BF16), tm=512)

    out = _ffn(x1, norm_ffn[0].astype(F32).reshape(1, D), w_gate[0].astype(BF16),
               w_up[0].astype(BF16), w_down[0].astype(BF16),
               norm_final.astype(F32).reshape(1, D), tm=1024, tf=512)
    return out.reshape(bsz, L, D).astype(x.dtype)
```

```python
import functools
import math

import jax
import jax.numpy as jnp
from jax import lax
from jax.experimental import pallas as pl
from jax.experimental.pallas import tpu as pltpu

F32 = jnp.float32
BF16 = jnp.bfloat16

D_MODEL = 2048
SSM_WIDTH = 1024
POOL_WIDTH = 1024
SSM_GROUP = 16
SSM_GROUPS = 64
SSM_STATE = 64
POOL_WINDOWS = (2, 4, 8, 16)
POOL_GROUP = 256
EPS = 1e-6

CHUNK = 16
CHUNK_LANES = CHUNK * SSM_GROUP
HALO = 16
LANES = 128
GROUPS_PER_BLOCK = LANES // SSM_GROUP
RELAYOUT_ROWS = 16
RELAYOUT_UNITS = 4
SCAN_BLOCK = 8

VMEM_LIMIT_BYTES = 60 * 1024 * 1024

HIGHEST = lax.Precision.HIGHEST


def _rms_norm(x, gain):
    ms = jnp.mean(x * x, axis=-1, keepdims=True)
    return x * lax.rsqrt(ms + EPS) * gain


def _cast_specs(weights, n_steps):
    in_specs, out_specs, out_shapes = [], [], []
    for w in weights:
        rows, cols = w.shape
        slab = pl.BlockSpec((rows // n_steps, cols), lambda i: (i, 0))
        in_specs.append(slab)
        out_specs.append(slab)
        out_shapes.append(jax.ShapeDtypeStruct((rows, cols), BF16))
    return in_specs, out_specs, out_shapes


def _cast_slabs(src_refs, dst_refs):
    for src, dst in zip(src_refs, dst_refs):
        dst[...] = src[...].astype(dst.dtype)


def _ssm_prep_kernel(n_cast, lre_ref, lim_ref, ls_ref, b_re_ref, b_im_ref, bt_re_ref, bt_im_ref,
                     ct_re_ref, ct_im_ref, d_ref, *refs):
    cast_src, (m_ref, bw_ref, cw_ref), cast_dst = refs[:n_cast], refs[n_cast:n_cast + 3], refs[n_cast + 3:]
    _cast_slabs(cast_src, cast_dst)
    P, H, T, GB = SSM_STATE, SSM_GROUP, CHUNK, GROUPS_PER_BLOCK
    lre, lim = lre_ref[...], lim_ref[...]
    step = jnp.exp(ls_ref[...])
    mag = jnp.exp(lre * step)
    lbr, lbi = mag * jnp.cos(lim * step), mag * jnp.sin(lim * step)
    den = lre * lre + lim * lim
    cfr = ((lbr - 1.0) * lre + lbi * lim) / den
    cfi = (lbi * lre - (lbr - 1.0) * lim) / den
    eye = (lax.broadcasted_iota(jnp.int32, (P, P), 0)
           == lax.broadcasted_iota(jnp.int32, (P, P), 1)).astype(F32)
    rows = jnp.concatenate([lbr, lbi, cfr, cfi], axis=0)
    cols = lax.dot_general(eye, rows, (((1,), (1,)), ((), ())), precision=HIGHEST,
                           preferred_element_type=F32)

    lane = lax.broadcasted_iota(jnp.int32, (1, CHUNK_LANES), 1)
    k_idx = lane // H
    sub = lax.broadcasted_iota(jnp.int32, (H, CHUNK_LANES), 0)
    lane_h = lax.broadcasted_iota(jnp.int32, (H, CHUNK_LANES), 1)
    rep = (lane_h % H == sub).astype(F32)

    def tile(a):
        return jnp.dot(a, rep, precision=HIGHEST, preferred_element_type=F32)

    def cmul(ar, ai, br, bi):
        return ar * br - ai * bi, ar * bi + ai * br

    bw_ref[...] = jnp.zeros_like(bw_ref)
    cw_ref[...] = jnp.zeros_like(cw_ref)
    for g in range(GB):
        col = lambda quantity: cols[:, quantity * GB + g:quantity * GB + g + 1]
        l1r, l1i, cr, ci = col(0), col(1), col(2), col(3)
        sq = [(l1r, l1i)]
        for _ in range(3):
            sq.append(cmul(*sq[-1], *sq[-1]))
        pkr = pki = prr = pri = None
        for bit, (fr, fi) in enumerate(sq):
            on = ((k_idx >> bit) & 1) == 1
            ur, ui = jnp.where(on, fr, 1.0), jnp.where(on, fi, 0.0)
            dr, di = jnp.where(on, 1.0, fr), jnp.where(on, 0.0, fi)
            if bit == 0:
                pkr, pki, prr, pri = ur, ui, dr, di
            else:
                pkr, pki = cmul(pkr, pki, ur, ui)
                prr, pri = cmul(prr, pri, dr, di)

        bb_re, bb_im = cmul(cr, ci, b_re_ref[g], b_im_ref[g])
        bbt_re, bbt_im = cmul(cfr[g:g + 1, :], cfi[g:g + 1, :], bt_re_ref[g], bt_im_ref[g])

        q, j = g // 2, g % 2
        r_re = slice(j * P, (j + 1) * P)
        r_im = slice((2 + j) * P, (3 + j) * P)
        c_g = slice(j * CHUNK_LANES, (j + 1) * CHUNK_LANES)
        w_re, w_im = cmul(prr, pri, tile(bb_re), tile(bb_im))
        bw_ref[q, r_re, c_g] = w_re.astype(bw_ref.dtype)
        bw_ref[q, r_im, c_g] = w_im.astype(bw_ref.dtype)
        g_re, g_im = cmul(pkr, pki, tile(ct_re_ref[g]), tile(ct_im_ref[g]))
        g1_re, g1_im = cmul(l1r, l1i, g_re, g_im)
        cw_ref[q, r_re, c_g] = g1_re.astype(cw_ref.dtype)
        cw_ref[q, r_im, c_g] = (-g1_im).astype(cw_ref.dtype)
        kt = (jnp.dot(bbt_re, g_re, precision=HIGHEST, preferred_element_type=F32)
              - jnp.dot(bbt_im, g_im, precision=HIGHEST, preferred_element_type=F32))
        kt = kt + jnp.where(lane_h == sub, d_ref[g], 0.0)
        for tau in range(T):
            blk = kt if tau == 0 else jnp.where(lane_h >= tau * H, pltpu.roll(kt, tau * H, axis=1), 0.0)
            m_ref[g, tau * H:(tau + 1) * H, :] = blk.astype(m_ref.dtype)


def _ssm_prep(lambda_re, lambda_im, log_step, b_re, b_im, c_re, c_im, d_skip, cast):
    G, P, H, GB = SSM_GROUPS, SSM_STATE, SSM_GROUP, GROUPS_PER_BLOCK
    n_steps = G // GB
    blk = lambda *shape: pl.BlockSpec(shape, lambda i: (i,) + (0,) * (len(shape) - 1))
    c_in, c_out, c_shapes = _cast_specs(cast, n_steps)
    return pl.pallas_call(
        functools.partial(_ssm_prep_kernel, len(cast)),
        grid=(n_steps,),
        in_specs=[blk(GB, P), blk(GB, P), blk(GB, 1),
                  blk(GB, P, H), blk(GB, P, H), blk(GB, H, P), blk(GB, H, P),
                  blk(GB, P, H), blk(GB, P, H), blk(GB, H, 1)] + c_in,
        out_specs=[blk(GB, CHUNK_LANES, CHUNK_LANES), blk(GB // 2, 4 * P, 2 * CHUNK_LANES),
                   blk(GB // 2, 4 * P, 2 * CHUNK_LANES)] + c_out,
        out_shape=[jax.ShapeDtypeStruct((G, CHUNK_LANES, CHUNK_LANES), BF16),
                   jax.ShapeDtypeStruct((G // 2, 4 * P, 2 * CHUNK_LANES), BF16),
                   jax.ShapeDtypeStruct((G // 2, 4 * P, 2 * CHUNK_LANES), BF16)] + c_shapes,
        compiler_params=pltpu.CompilerParams(dimension_semantics=("parallel",),
                                             vmem_limit_bytes=VMEM_LIMIT_BYTES),
        name="ssm_prep",
    )(lambda_re, lambda_im, log_step.reshape(G, 1),
      b_re, b_im, jnp.swapaxes(b_re, 1, 2), jnp.swapaxes(b_im, 1, 2),
      jnp.swapaxes(c_re, 1, 2), jnp.swapaxes(c_im, 1, 2), d_skip.reshape(G, H, 1), *cast)


def _in_proj_kernel(n_cast, x_ref, gain_ref, w_ref, *refs):
    cast_src, (u_ref, v_ref), cast_dst = refs[:n_cast], refs[n_cast:n_cast + 2], refs[n_cast + 2:]
    _cast_slabs(cast_src, cast_dst)
    h = _rms_norm(x_ref[...], gain_ref[...]).astype(BF16)
    p = jnp.dot(h, w_ref[...], preferred_element_type=F32)
    u_ref[...] = p[:, :SSM_WIDTH].astype(u_ref.dtype)
    v_ref[...] = p[:, SSM_WIDTH:].astype(v_ref.dtype)


def _in_proj(x, gain, w_in, cast, tm):
    L = x.shape[0]
    c_in, c_out, c_shapes = _cast_specs(cast, L // tm)
    return pl.pallas_call(
        functools.partial(_in_proj_kernel, len(cast)),
        grid=(L // tm,),
        in_specs=[pl.BlockSpec((tm, D_MODEL), lambda i: (i, 0)),
                  pl.BlockSpec((1, D_MODEL), lambda i: (0, 0)),
                  pl.BlockSpec((D_MODEL, D_MODEL), lambda i: (0, 0))] + c_in,
        out_specs=[pl.BlockSpec((tm, SSM_WIDTH), lambda i: (i, 0)),
                   pl.BlockSpec((tm, POOL_WIDTH), lambda i: (i, 0))] + c_out,
        out_shape=[jax.ShapeDtypeStruct((L, SSM_WIDTH), F32),
                   jax.ShapeDtypeStruct((L, POOL_WIDTH), BF16)] + c_shapes,
        compiler_params=pltpu.CompilerParams(dimension_semantics=("parallel",),
                                             vmem_limit_bytes=VMEM_LIMIT_BYTES),
        name="in_proj",
    )(x, gain, w_in, *cast)


def _granule_transpose(units):
    lane = lax.broadcasted_iota(jnp.int32, units[0][0].shape, 1)
    units = [list(a) for a in units]
    for s in (4, 2, 1):
        upper = ((lane // SSM_GROUP) & s) != 0
        for a in units:
            for i in range(GROUPS_PER_BLOCK):
                if i & s:
                    continue
                lo, hi = a[i], a[i + s]
                a[i] = jnp.where(upper, pltpu.roll(hi, s * SSM_GROUP, axis=1), lo)
                a[i + s] = jnp.where(upper, hi, pltpu.roll(lo, LANES - s * SSM_GROUP, axis=1))
    return units


def _cmul(ar, ai, br, bi):
    return ar * br - ai * bi, ar * bi + ai * br


def _ssm_kernel(u_ref, m_ref, bw_ref, cw_ref, lre_ref, lim_ref, ls_ref, y_ref,
                xg_ref, yg_ref, fr_ref, fi_ref, cr_ref, ci_ref):
    n_chunks = xg_ref.shape[1]
    half = CHUNK // 2
    rb = RELAYOUT_ROWS
    span = RELAYOUT_UNITS * rb
    sb = SCAN_BLOCK
    n_sb = n_chunks // sb

    def relayout_in(it, carry):
        base = pl.multiple_of(it * span, span)
        units = [[u_ref[pl.ds((base + k * rb) * CHUNK + hf * half + t, rb, stride=CHUNK), :].astype(BF16)
                  for t in range(half)]
                 for k in range(RELAYOUT_UNITS) for hf in range(2)]
        units = _granule_transpose(units)
        for k in range(RELAYOUT_UNITS):
            for hf in range(2):
                for g in range(GROUPS_PER_BLOCK):
                    xg_ref[g, pl.ds(base + k * rb, rb), hf * LANES:(hf + 1) * LANES] = units[2 * k + hf][g]
        return carry

    lax.fori_loop(0, n_chunks // span, relayout_in, 0, unroll=2)

    blk3 = (n_sb, sb, 2 * SSM_STATE)
    row_in = lax.broadcasted_iota(jnp.int32, blk3, 1)
    brow = lax.broadcasted_iota(jnp.int32, (n_sb, 2 * SSM_STATE), 0)
    for q in range(GROUPS_PER_BLOCK // 2):
        x0, x1 = xg_ref[2 * q], xg_ref[2 * q + 1]
        xp = jnp.concatenate([x0, x1], axis=1)
        e = lax.dot_general(xp, bw_ref[q], (((1,), (1,)), ((), ())), preferred_element_type=F32)
        step = jnp.exp(ls_ref[q])
        zr, zi = lre_ref[q] * step * CHUNK, lim_ref[q] * step * CHUNK
        mag = jnp.exp(zr)
        a1r, a1i = mag * jnp.cos(zi), mag * jnp.sin(zi)

        fr = e[:, :2 * SSM_STATE].reshape(blk3)
        fi = e[:, 2 * SSM_STATE:].reshape(blk3)
        ar, ai = a1r, a1i
        d = 1
        while d < sb:
            tr = jnp.where(row_in >= d, pltpu.roll(fr, d, axis=1), 0.0)
            ti = jnp.where(row_in >= d, pltpu.roll(fi, d, axis=1), 0.0)
            fr, fi = fr + ar * tr - ai * ti, fi + ar * ti + ai * tr
            ar, ai = _cmul(ar, ai, ar, ai)
            d *= 2
        fr_ref[...] = fr.reshape(n_chunks, 2 * SSM_STATE)
        fi_ref[...] = fi.reshape(n_chunks, 2 * SSM_STATE)
        gr = fr_ref[pl.ds(sb - 1, n_sb, stride=sb), :]
        gi = fi_ref[pl.ds(sb - 1, n_sb, stride=sb), :]
        gr = jnp.where(brow >= 1, pltpu.roll(gr, 1, axis=0), 0.0)
        gi = jnp.where(brow >= 1, pltpu.roll(gi, 1, axis=0), 0.0)
        d = 1
        while d < n_sb:
            tr = jnp.where(brow >= d, pltpu.roll(gr, d, axis=0), 0.0)
            ti = jnp.where(brow >= d, pltpu.roll(gi, d, axis=0), 0.0)
            gr, gi = gr + ar * tr - ai * ti, gi + ar * ti + ai * tr
            ar, ai = _cmul(ar, ai, ar, ai)
            d *= 2
        pr, pi = jnp.ones_like(a1r), jnp.zeros_like(a1i)
        for r in range(sb):
            cr_ref[pl.ds(r, n_sb, stride=sb), :] = pr * gr - pi * gi
            ci_ref[pl.ds(r, n_sb, stride=sb), :] = pr * gi + pi * gr
            pr, pi = _cmul(pr, pi, a1r, a1i)
        sr = jnp.where(row_in >= 1, pltpu.roll(fr, 1, axis=1), 0.0).reshape(n_chunks, 2 * SSM_STATE)
        si = jnp.where(row_in >= 1, pltpu.roll(fi, 1, axis=1), 0.0).reshape(n_chunks, 2 * SSM_STATE)
        sr = sr + cr_ref[...]
        si = si + ci_ref[...]
        s = jnp.concatenate([sr, si], axis=1).astype(BF16)
        yc = jnp.dot(s, cw_ref[q], preferred_element_type=F32)
        y0 = jnp.dot(x0, m_ref[2 * q], preferred_element_type=F32) + yc[:, :CHUNK_LANES]
        y1 = jnp.dot(x1, m_ref[2 * q + 1], preferred_element_type=F32) + yc[:, CHUNK_LANES:]
        yg_ref[2 * q] = y0.astype(yg_ref.dtype)
        yg_ref[2 * q + 1] = y1.astype(yg_ref.dtype)

    def relayout_out(it, carry):
        base = pl.multiple_of(it * span, span)
        units = [[yg_ref[g, pl.ds(base + k * rb, rb), hf * LANES:(hf + 1) * LANES]
                  for g in range(GROUPS_PER_BLOCK)]
                 for k in range(RELAYOUT_UNITS) for hf in range(2)]
        units = _granule_transpose(units)
        for k in range(RELAYOUT_UNITS):
            for hf in range(2):
                for t in range(half):
                    y_ref[pl.ds((base + k * rb) * CHUNK + hf * half + t, rb, stride=CHUNK), :] = (
                        units[2 * k + hf][t].astype(F32))
        return carry

    lax.fori_loop(0, n_chunks // span, relayout_out, 0, unroll=2)


def _ssm(u, m, bw_pair, cw_pair, lre_row, lim_row, ls_row):
    L = u.shape[0]
    n_chunks = L // CHUNK
    n_blocks = SSM_WIDTH // LANES
    ppb = GROUPS_PER_BLOCK // 2
    return pl.pallas_call(
        _ssm_kernel,
        grid=(n_blocks,),
        in_specs=[pl.BlockSpec((L, LANES), lambda b: (0, b)),
                  pl.BlockSpec((GROUPS_PER_BLOCK, CHUNK_LANES, CHUNK_LANES), lambda b: (b, 0, 0)),
                  pl.BlockSpec((ppb, 4 * SSM_STATE, 2 * CHUNK_LANES), lambda b: (b, 0, 0)),
                  pl.BlockSpec((ppb, 4 * SSM_STATE, 2 * CHUNK_LANES), lambda b: (b, 0, 0)),
                  pl.BlockSpec((ppb, 1, 2 * SSM_STATE), lambda b: (b, 0, 0)),
                  pl.BlockSpec((ppb, 1, 2 * SSM_STATE), lambda b: (b, 0, 0)),
                  pl.BlockSpec((ppb, 1, 2 * SSM_STATE), lambda b: (b, 0, 0))],
        out_specs=pl.BlockSpec((L, LANES), lambda b: (0, b)),
        out_shape=jax.ShapeDtypeStruct((L, SSM_WIDTH), F32),
        scratch_shapes=[pltpu.VMEM((GROUPS_PER_BLOCK, n_chunks, CHUNK_LANES), BF16),
                        pltpu.VMEM((GROUPS_PER_BLOCK, n_chunks, CHUNK_LANES), BF16)]
                       + [pltpu.VMEM((n_chunks, 2 * SSM_STATE), F32)] * 4,
        compiler_params=pltpu.CompilerParams(dimension_semantics=("parallel",),
                                             vmem_limit_bytes=VMEM_LIMIT_BYTES),
        name="ssm",
    )(u, m, bw_pair, cw_pair, lre_row, lim_row, ls_row)


def _out_proj_kernel(n_cast, y_ref, v_ref, vh_ref, x_ref, wglu_ref, bglu_ref, wpool_ref, bpool_ref,
                     pscale_ref, wtop_ref, wbot_ref, *refs):
    cast_src, o_ref, cast_dst, vbuf_ref = refs[:n_cast], refs[n_cast], refs[n_cast + 1:-1], refs[-1]
    _cast_slabs(cast_src, cast_dst)
    i = pl.program_id(0)
    tm = y_ref.shape[0]
    y = y_ref[...].astype(F32)
    g = 0.5 * y * (1.0 + lax.erf(y * (1.0 / math.sqrt(2.0))))
    z = jnp.dot(g.astype(BF16), wglu_ref[...], preferred_element_type=F32) + bglu_ref[...]
    ys = g * (1.0 / (1.0 + jnp.exp(-z)))

    vbuf_ref[0:HALO, :] = jnp.where(i > 0, vh_ref[...].astype(F32), 0.0)
    vbuf_ref[HALO:, :] = v_ref[...].astype(F32)
    t1 = (lax.broadcasted_iota(jnp.int32, (tm, 1), 0) + (i * tm + 1)).astype(F32)
    pooled = []
    for k, w in enumerate(POOL_WINDOWS):
        cols = slice(k * POOL_GROUP, (k + 1) * POOL_GROUP)
        vk = vbuf_ref[:, cols]
        acc, span = vk, 1
        while span < w:
            acc = acc + pltpu.roll(acc, span, axis=0)
            span *= 2
        mean = acc[HALO:, :] / jnp.minimum(t1, float(w))
        pk = (mean - vk[HALO:, :]).astype(BF16)
        mk = jnp.dot(pk, wpool_ref[k], preferred_element_type=F32)
        pooled.append(mk)
    yp = (jnp.concatenate(pooled, axis=1) + bpool_ref[...]) * pscale_ref[...]

    o_ref[...] = (x_ref[...]
                  + jnp.dot(ys.astype(BF16), wtop_ref[...], preferred_element_type=F32)
                  + jnp.dot(yp.astype(BF16), wbot_ref[...], preferred_element_type=F32))


def _out_proj(y, v, x, w_glu, b_glu, w_pool, b_pool, pool_scale, w_out, cast, tm):
    L = x.shape[0]
    halo_blocks = tm // HALO
    const = lambda *shape: pl.BlockSpec(shape, lambda i: (0,) * len(shape))
    c_in, c_out, c_shapes = _cast_specs(cast, L // tm)
    return pl.pallas_call(
        functools.partial(_out_proj_kernel, len(cast)),
        grid=(L // tm,),
        in_specs=[pl.BlockSpec((tm, SSM_WIDTH), lambda i: (i, 0)),
                  pl.BlockSpec((tm, POOL_WIDTH), lambda i: (i, 0)),
                  pl.BlockSpec((HALO, POOL_WIDTH), lambda i: (jnp.maximum(i * halo_blocks - 1, 0), 0)),
                  pl.BlockSpec((tm, D_MODEL), lambda i: (i, 0)),
                  const(SSM_WIDTH, SSM_WIDTH), const(1, SSM_WIDTH),
                  const(len(POOL_WINDOWS), POOL_GROUP, POOL_GROUP), const(1, POOL_WIDTH),
                  const(1, POOL_WIDTH),
                  pl.BlockSpec((SSM_WIDTH, D_MODEL), lambda i: (0, 0)),
                  pl.BlockSpec((POOL_WIDTH, D_MODEL), lambda i: (1, 0))] + c_in,
        out_specs=[pl.BlockSpec((tm, D_MODEL), lambda i: (i, 0))] + c_out,
        out_shape=[jax.ShapeDtypeStruct((L, D_MODEL), F32)] + c_shapes,
        scratch_shapes=[pltpu.VMEM((HALO + tm, POOL_WIDTH), F32)],
        compiler_params=pltpu.CompilerParams(dimension_semantics=("parallel",),
                                             vmem_limit_bytes=VMEM_LIMIT_BYTES),
        name="out_proj",
    )(y, v, v, x, w_glu, b_glu, w_pool, b_pool, pool_scale, w_out, w_out, *cast)


def _ffn_kernel(x_ref, gain_ref, wg_ref, wu_ref, wd_ref, gfin_ref, o_ref, h_ref):
    j = pl.program_id(1)

    @pl.when(j == 0)
    def _():
        x = x_ref[...]
        h_ref[...] = _rms_norm(x, gain_ref[...]).astype(BF16)
        o_ref[...] = x

    h = h_ref[...]
    a = jnp.dot(h, wg_ref[...], preferred_element_type=F32)
    b = jnp.dot(h, wu_ref[...], preferred_element_type=F32)
    ff = (a * (1.0 / (1.0 + jnp.exp(-a))) * b).astype(BF16)
    o_ref[...] += jnp.dot(ff, wd_ref[...], preferred_element_type=F32)

    @pl.when(j == pl.num_programs(1) - 1)
    def _():
        o_ref[...] = _rms_norm(o_ref[...], gfin_ref[...])


def _ffn(x, gain, w_gate, w_up, w_down, gain_final, tm, tf):
    L = x.shape[0]
    d_ff = w_gate.shape[1]
    return pl.pallas_call(
        _ffn_kernel,
        grid=(L // tm, d_ff // tf),
        in_specs=[pl.BlockSpec((tm, D_MODEL), lambda i, j: (i, 0)),
                  pl.BlockSpec((1, D_MODEL), lambda i, j: (0, 0)),
                  pl.BlockSpec((D_MODEL, tf), lambda i, j: (0, j)),
                  pl.BlockSpec((D_MODEL, tf), lambda i, j: (0, j)),
                  pl.BlockSpec((tf, D_MODEL), lambda i, j: (j, 0)),
                  pl.BlockSpec((1, D_MODEL), lambda i, j: (0, 0))],
        out_specs=pl.BlockSpec((tm, D_MODEL), lambda i, j: (i, 0)),
        out_shape=jax.ShapeDtypeStruct((L, D_MODEL), F32),
        scratch_shapes=[pltpu.VMEM((tm, D_MODEL), BF16)],
        compiler_params=pltpu.CompilerParams(dimension_semantics=("parallel", "arbitrary"),
                                             vmem_limit_bytes=VMEM_LIMIT_BYTES),
        name="ffn",
    )(x, gain, w_gate, w_up, w_down, gain_final)


def kernel(x, norm_mix, w_in, lambda_re, lambda_im, log_step, b_re, b_im, c_re, c_im, d_skip,
           w_glu, b_glu, w_pool, b_pool, pool_scale, w_out, norm_ffn, w_gate, w_up, w_down,
           norm_final):
    bsz, L, D = x.shape
    assert bsz == 1 and D == D_MODEL and w_in.shape[0] == 1 and L % 1024 == 0
    G, P = SSM_GROUPS, SSM_STATE
    xs = x.reshape(L, D).astype(F32)

    n_pool = len(POOL_WINDOWS)
    m, bw_pair, cw_pair, w_in_b, w_out_b, w_glu_b, w_pool_b = _ssm_prep(
        lambda_re[0].astype(F32), lambda_im[0].astype(F32), log_step[0].astype(F32),
        b_re[0].astype(F32), b_im[0].astype(F32), c_re[0].astype(F32), c_im[0].astype(F32),
        d_skip[0].astype(F32),
        cast=[w_in[0], w_out[0], w_glu[0], w_pool[0].reshape(n_pool * POOL_GROUP, POOL_GROUP)])
    lre_row = lambda_re[0].astype(F32).reshape(G // 2, 1, 2 * P)
    lim_row = lambda_im[0].astype(F32).reshape(G // 2, 1, 2 * P)
    ls_row = jnp.repeat(log_step[0].astype(F32), P).reshape(G // 2, 1, 2 * P)

    u, v, w_gate_b = _in_proj(xs, norm_mix[0].astype(F32).reshape(1, D), w_in_b, cast=[w_gate[0]], tm=512)

    y = _ssm(u, m, bw_pair, cw_pair, lre_row, lim_row, ls_row)

    x1, w_up_b, w_down_b = _out_proj(
        y, v, xs, w_glu_b, b_glu[0].astype(F32).reshape(1, SSM_WIDTH),
        w_pool_b.reshape(n_pool, POOL_GROUP, POOL_GROUP), b_pool[0].astype(F32).reshape(1, POOL_WIDTH),
        pool_scale[0].astype(F32).reshape(1, POOL_WIDTH), w_out_b, cast=[w_up[0], w_down[0]], tm=512)

    out = _ffn(x1, norm_ffn[0].astype(F32).reshape(1, D), w_gate_b, w_up_b, w_down_b,
               norm_final.astype(F32).reshape(1, D), tm=1024, tf=512)
    return out.reshape(bsz, L, D).astype(x.dtype)
```

```python
import functools
import math

import jax
import jax.numpy as jnp
from jax import lax
from jax.experimental import pallas as pl
from jax.experimental.pallas import tpu as pltpu

F32 = jnp.float32
BF16 = jnp.bfloat16

D_MODEL = 2048
SSM_WIDTH = 1024
POOL_WIDTH = 1024
SSM_GROUP = 16
SSM_GROUPS = 64
SSM_STATE = 64
POOL_WINDOWS = (2, 4, 8, 16)
POOL_GROUP = 256
EPS = 1e-6

CHUNK = 16
CHUNK_LANES = CHUNK * SSM_GROUP
HALO = 16
LANES = 128
GROUPS_PER_BLOCK = LANES // SSM_GROUP
RELAYOUT_ROWS = 16
RELAYOUT_UNITS = 4
SCAN_BLOCK = 8

VMEM_LIMIT_BYTES = 60 * 1024 * 1024

HIGHEST = lax.Precision.HIGHEST


def _rms_norm(x, gain):
    ms = jnp.mean(x * x, axis=-1, keepdims=True)
    return x * lax.rsqrt(ms + EPS) * gain


def _cast_specs(weights, n_steps):
    in_specs, out_specs, out_shapes = [], [], []
    for w in weights:
        rows, cols = w.shape
        slab = pl.BlockSpec((rows // n_steps, cols), lambda i: (i, 0))
        in_specs.append(slab)
        out_specs.append(slab)
        out_shapes.append(jax.ShapeDtypeStruct((rows, cols), BF16))
    return in_specs, out_specs, out_shapes


def _cast_slabs(src_refs, dst_refs):
    for src, dst in zip(src_refs, dst_refs):
        dst[...] = src[...].astype(dst.dtype)


def _ssm_prep_kernel(n_cast, lre_ref, lim_ref, ls_ref, b_re_ref, b_im_ref, bt_re_ref, bt_im_ref,
                     ct_re_ref, ct_im_ref, d_ref, *refs):
    cast_src, (m_ref, bw_ref, cw_ref), cast_dst = refs[:n_cast], refs[n_cast:n_cast + 3], refs[n_cast + 3:]
    _cast_slabs(cast_src, cast_dst)
    P, H, T, GB = SSM_STATE, SSM_GROUP, CHUNK, GROUPS_PER_BLOCK
    lre, lim = lre_ref[...], lim_ref[...]
    step = jnp.exp(ls_ref[...])
    mag = jnp.exp(lre * step)
    lbr, lbi = mag * jnp.cos(lim * step), mag * jnp.sin(lim * step)
    den = lre * lre + lim * lim
    cfr = ((lbr - 1.0) * lre + lbi * lim) / den
    cfi = (lbi * lre - (lbr - 1.0) * lim) / den
    eye = (lax.broadcasted_iota(jnp.int32, (P, P), 0)
           == lax.broadcasted_iota(jnp.int32, (P, P), 1)).astype(F32)
    rows = jnp.concatenate([lbr, lbi, cfr, cfi], axis=0)
    cols = lax.dot_general(eye, rows, (((1,), (1,)), ((), ())), precision=HIGHEST,
                           preferred_element_type=F32)

    lane = lax.broadcasted_iota(jnp.int32, (1, CHUNK_LANES), 1)
    k_idx = lane // H
    sub = lax.broadcasted_iota(jnp.int32, (H, CHUNK_LANES), 0)
    lane_h = lax.broadcasted_iota(jnp.int32, (H, CHUNK_LANES), 1)
    rep = (lane_h % H == sub).astype(F32)

    def tile(a):
        return jnp.dot(a, rep, precision=HIGHEST, preferred_element_type=F32)

    def cmul(ar, ai, br, bi):
        return ar * br - ai * bi, ar * bi + ai * br

    bw_ref[...] = jnp.zeros_like(bw_ref)
    cw_ref[...] = jnp.zeros_like(cw_ref)
    for g in range(GB):
        col = lambda quantity: cols[:, quantity * GB + g:quantity * GB + g + 1]
        l1r, l1i, cr, ci = col(0), col(1), col(2), col(3)
        sq = [(l1r, l1i)]
        for _ in range(3):
            sq.append(cmul(*sq[-1], *sq[-1]))
        pkr = pki = prr = pri = None
        for bit, (fr, fi) in enumerate(sq):
            on = ((k_idx >> bit) & 1) == 1
            ur, ui = jnp.where(on, fr, 1.0), jnp.where(on, fi, 0.0)
            dr, di = jnp.where(on, 1.0, fr), jnp.where(on, 0.0, fi)
            if bit == 0:
                pkr, pki, prr, pri = ur, ui, dr, di
            else:
                pkr, pki = cmul(pkr, pki, ur, ui)
                prr, pri = cmul(prr, pri, dr, di)

        bb_re, bb_im = cmul(cr, ci, b_re_ref[g], b_im_ref[g])
        bbt_re, bbt_im = cmul(cfr[g:g + 1, :], cfi[g:g + 1, :], bt_re_ref[g], bt_im_ref[g])

        q, j = g // 2, g % 2
        r_re = slice(j * P, (j + 1) * P)
        r_im = slice((2 + j) * P, (3 + j) * P)
        c_g = slice(j * CHUNK_LANES, (j + 1) * CHUNK_LANES)
        w_re, w_im = cmul(prr, pri, tile(bb_re), tile(bb_im))
        bw_ref[q, r_re, c_g] = w_re.astype(bw_ref.dtype)
        bw_ref[q, r_im, c_g] = w_im.astype(bw_ref.dtype)
        g_re, g_im = cmul(pkr, pki, tile(ct_re_ref[g]), tile(ct_im_ref[g]))
        g1_re, g1_im = cmul(l1r, l1i, g_re, g_im)
        cw_ref[q, r_re, c_g] = g1_re.astype(cw_ref.dtype)
        cw_ref[q, r_im, c_g] = (-g1_im).astype(cw_ref.dtype)
        kt = (jnp.dot(bbt_re, g_re, precision=HIGHEST, preferred_element_type=F32)
              - jnp.dot(bbt_im, g_im, precision=HIGHEST, preferred_element_type=F32))
        kt = kt + jnp.where(lane_h == sub, d_ref[g], 0.0)
        for tau in range(T):
            blk = kt if tau == 0 else jnp.where(lane_h >= tau * H, pltpu.roll(kt, tau * H, axis=1), 0.0)
            m_ref[g, tau * H:(tau + 1) * H, :] = blk.astype(m_ref.dtype)


def _ssm_prep(lambda_re, lambda_im, log_step, b_re, b_im, c_re, c_im, d_skip, cast):
    G, P, H, GB = SSM_GROUPS, SSM_STATE, SSM_GROUP, GROUPS_PER_BLOCK
    n_steps = G // GB
    blk = lambda *shape: pl.BlockSpec(shape, lambda i: (i,) + (0,) * (len(shape) - 1))
    c_in, c_out, c_shapes = _cast_specs(cast, n_steps)
    return pl.pallas_call(
        functools.partial(_ssm_prep_kernel, len(cast)),
        grid=(n_steps,),
        in_specs=[blk(GB, P), blk(GB, P), blk(GB, 1),
                  blk(GB, P, H), blk(GB, P, H), blk(GB, H, P), blk(GB, H, P),
                  blk(GB, P, H), blk(GB, P, H), blk(GB, H, 1)] + c_in,
        out_specs=[blk(GB, CHUNK_LANES, CHUNK_LANES), blk(GB // 2, 4 * P, 2 * CHUNK_LANES),
                   blk(GB // 2, 4 * P, 2 * CHUNK_LANES)] + c_out,
        out_shape=[jax.ShapeDtypeStruct((G, CHUNK_LANES, CHUNK_LANES), BF16),
                   jax.ShapeDtypeStruct((G // 2, 4 * P, 2 * CHUNK_LANES), BF16),
                   jax.ShapeDtypeStruct((G // 2, 4 * P, 2 * CHUNK_LANES), BF16)] + c_shapes,
        compiler_params=pltpu.CompilerParams(dimension_semantics=("parallel",),
                                             vmem_limit_bytes=VMEM_LIMIT_BYTES),
        name="ssm_prep",
    )(lambda_re, lambda_im, log_step.reshape(G, 1),
      b_re, b_im, jnp.swapaxes(b_re, 1, 2), jnp.swapaxes(b_im, 1, 2),
      jnp.swapaxes(c_re, 1, 2), jnp.swapaxes(c_im, 1, 2), d_skip.reshape(G, H, 1), *cast)


def _in_proj_kernel(n_cast, x_ref, gain_ref, w_ref, *refs):
    cast_src, (u_ref, v_ref), cast_dst = refs[:n_cast], refs[n_cast:n_cast + 2], refs[n_cast + 2:]
    _cast_slabs(cast_src, cast_dst)
    h = _rms_norm(x_ref[...], gain_ref[...]).astype(BF16)
    p = jnp.dot(h, w_ref[...], preferred_element_type=F32)
    u_ref[...] = p[:, :SSM_WIDTH].astype(u_ref.dtype)
    v_ref[...] = p[:, SSM_WIDTH:].astype(v_ref.dtype)


def _in_proj(x, gain, w_in, cast, tm):
    L = x.shape[0]
    c_in, c_out, c_shapes = _cast_specs(cast, L // tm)
    return pl.pallas_call(
        functools.partial(_in_proj_kernel, len(cast)),
        grid=(L // tm,),
        in_specs=[pl.BlockSpec((tm, D_MODEL), lambda i: (i, 0)),
                  pl.BlockSpec((1, D_MODEL), lambda i: (0, 0)),
                  pl.BlockSpec((D_MODEL, D_MODEL), lambda i: (0, 0))] + c_in,
        out_specs=[pl.BlockSpec((tm, SSM_WIDTH), lambda i: (i, 0)),
                   pl.BlockSpec((tm, POOL_WIDTH), lambda i: (i, 0))] + c_out,
        out_shape=[jax.ShapeDtypeStruct((L, SSM_WIDTH), F32),
                   jax.ShapeDtypeStruct((L, POOL_WIDTH), BF16)] + c_shapes,
        compiler_params=pltpu.CompilerParams(dimension_semantics=("parallel",),
                                             vmem_limit_bytes=VMEM_LIMIT_BYTES),
        name="in_proj",
    )(x, gain, w_in, *cast)


def _granule_transpose(units):
    lane = lax.broadcasted_iota(jnp.int32, units[0][0].shape, 1)
    units = [list(a) for a in units]
    for s in (4, 2, 1):
        upper = ((lane // SSM_GROUP) & s) != 0
        for a in units:
            for i in range(GROUPS_PER_BLOCK):
                if i & s:
                    continue
                lo, hi = a[i], a[i + s]
                a[i] = jnp.where(upper, pltpu.roll(hi, s * SSM_GROUP, axis=1), lo)
                a[i + s] = jnp.where(upper, hi, pltpu.roll(lo, LANES - s * SSM_GROUP, axis=1))
    return units


def _cmul(ar, ai, br, bi):
    return ar * br - ai * bi, ar * bi + ai * br


def _ssm_kernel(u_ref, m_ref, bw_ref, cw_ref, lre_ref, lim_ref, ls_ref, y_ref,
                xg_ref, yg_ref, fr_ref, fi_ref, cr_ref, ci_ref):
    n_chunks = xg_ref.shape[1]
    half = CHUNK // 2
    rb = RELAYOUT_ROWS
    span = RELAYOUT_UNITS * rb
    sb = SCAN_BLOCK
    n_sb = n_chunks // sb

    def relayout_in(it, carry):
        base = pl.multiple_of(it * span, span)
        units = [[u_ref[pl.ds((base + k * rb) * CHUNK + hf * half + t, rb, stride=CHUNK), :].astype(BF16)
                  for t in range(half)]
                 for k in range(RELAYOUT_UNITS) for hf in range(2)]
        units = _granule_transpose(units)
        for k in range(RELAYOUT_UNITS):
            for hf in range(2):
                for g in range(GROUPS_PER_BLOCK):
                    xg_ref[g, pl.ds(base + k * rb, rb), hf * LANES:(hf + 1) * LANES] = units[2 * k + hf][g]
        return carry

    lax.fori_loop(0, n_chunks // span, relayout_in, 0, unroll=2)

    blk3 = (n_sb, sb, 2 * SSM_STATE)
    row_in = lax.broadcasted_iota(jnp.int32, blk3, 1)
    brow = lax.broadcasted_iota(jnp.int32, (n_sb, 2 * SSM_STATE), 0)
    for q in range(GROUPS_PER_BLOCK // 2):
        x0, x1 = xg_ref[2 * q], xg_ref[2 * q + 1]
        xp = jnp.concatenate([x0, x1], axis=1)
        e = lax.dot_general(xp, bw_ref[q], (((1,), (1,)), ((), ())), preferred_element_type=F32)
        step = jnp.exp(ls_ref[q])
        zr, zi = lre_ref[q] * step * CHUNK, lim_ref[q] * step * CHUNK
        mag = jnp.exp(zr)
        a1r, a1i = mag * jnp.cos(zi), mag * jnp.sin(zi)

        fr = e[:, :2 * SSM_STATE].reshape(blk3)
        fi = e[:, 2 * SSM_STATE:].reshape(blk3)
        ar, ai = a1r, a1i
        d = 1
        while d < sb:
            tr = jnp.where(row_in >= d, pltpu.roll(fr, d, axis=1), 0.0)
            ti = jnp.where(row_in >= d, pltpu.roll(fi, d, axis=1), 0.0)
            fr, fi = fr + ar * tr - ai * ti, fi + ar * ti + ai * tr
            ar, ai = _cmul(ar, ai, ar, ai)
            d *= 2
        fr_ref[...] = fr.reshape(n_chunks, 2 * SSM_STATE)
        fi_ref[...] = fi.reshape(n_chunks, 2 * SSM_STATE)
        gr = fr_ref[pl.ds(sb - 1, n_sb, stride=sb), :]
        gi = fi_ref[pl.ds(sb - 1, n_sb, stride=sb), :]
        gr = jnp.where(brow >= 1, pltpu.roll(gr, 1, axis=0), 0.0)
        gi = jnp.where(brow >= 1, pltpu.roll(gi, 1, axis=0), 0.0)
        d = 1
        while d < n_sb:
            tr = jnp.where(brow >= d, pltpu.roll(gr, d, axis=0), 0.0)
            ti = jnp.where(brow >= d, pltpu.roll(gi, d, axis=0), 0.0)
            gr, gi = gr + ar * tr - ai * ti, gi + ar * ti + ai * tr
            ar, ai = _cmul(ar, ai, ar, ai)
            d *= 2
        pr, pi = jnp.ones_like(a1r), jnp.zeros_like(a1i)
        for r in range(sb):
            cr_ref[pl.ds(r, n_sb, stride=sb), :] = pr * gr - pi * gi
            ci_ref[pl.ds(r, n_sb, stride=sb), :] = pr * gi + pi * gr
            pr, pi = _cmul(pr, pi, a1r, a1i)
        sr = jnp.where(row_in >= 1, pltpu.roll(fr, 1, axis=1), 0.0).reshape(n_chunks, 2 * SSM_STATE)
        si = jnp.where(row_in >= 1, pltpu.roll(fi, 1, axis=1), 0.0).reshape(n_chunks, 2 * SSM_STATE)
        sr = sr + cr_ref[...]
        si = si + ci_ref[...]
        s = jnp.concatenate([sr, si], axis=1).astype(BF16)
        yc = jnp.dot(s, cw_ref[q], preferred_element_type=F32)
        y0 = jnp.dot(x0, m_ref[2 * q], preferred_element_type=F32) + yc[:, :CHUNK_LANES]
        y1 = jnp.dot(x1, m_ref[2 * q + 1], preferred_element_type=F32) + yc[:, CHUNK_LANES:]
        yg_ref[2 * q] = y0.astype(yg_ref.dtype)
        yg_ref[2 * q + 1] = y1.astype(yg_ref.dtype)

    def relayout_out(it, carry):
        base = pl.multiple_of(it * span, span)
        units = [[yg_ref[g, pl.ds(base + k * rb, rb), hf * LANES:(hf + 1) * LANES]
                  for g in range(GROUPS_PER_BLOCK)]
                 for k in range(RELAYOUT_UNITS) for hf in range(2)]
        units = _granule_transpose(units)
        for k in range(RELAYOUT_UNITS):
            for hf in range(2):
                for t in range(half):
                    y_ref[pl.ds((base + k * rb) * CHUNK + hf * half + t, rb, stride=CHUNK), :] = (
                        units[2 * k + hf][t].astype(F32))
        return carry

    lax.fori_loop(0, n_chunks // span, relayout_out, 0, unroll=2)


def _ssm(u, m, bw_pair, cw_pair, lre_row, lim_row, ls_row):
    L = u.shape[0]
    n_chunks = L // CHUNK
    n_blocks = SSM_WIDTH // LANES
    ppb = GROUPS_PER_BLOCK // 2
    return pl.pallas_call(
        _ssm_kernel,
        grid=(n_blocks,),
        in_specs=[pl.BlockSpec((L, LANES), lambda b: (0, b)),
                  pl.BlockSpec((GROUPS_PER_BLOCK, CHUNK_LANES, CHUNK_LANES), lambda b: (b, 0, 0)),
                  pl.BlockSpec((ppb, 4 * SSM_STATE, 2 * CHUNK_LANES), lambda b: (b, 0, 0)),
                  pl.BlockSpec((ppb, 4 * SSM_STATE, 2 * CHUNK_LANES), lambda b: (b, 0, 0)),
                  pl.BlockSpec((ppb, 1, 2 * SSM_STATE), lambda b: (b, 0, 0)),
                  pl.BlockSpec((ppb, 1, 2 * SSM_STATE), lambda b: (b, 0, 0)),
                  pl.BlockSpec((ppb, 1, 2 * SSM_STATE), lambda b: (b, 0, 0))],
        out_specs=pl.BlockSpec((L, LANES), lambda b: (0, b)),
        out_shape=jax.ShapeDtypeStruct((L, SSM_WIDTH), F32),
        scratch_shapes=[pltpu.VMEM((GROUPS_PER_BLOCK, n_chunks, CHUNK_LANES), BF16),
                        pltpu.VMEM((GROUPS_PER_BLOCK, n_chunks, CHUNK_LANES), BF16)]
                       + [pltpu.VMEM((n_chunks, 2 * SSM_STATE), F32)] * 4,
        compiler_params=pltpu.CompilerParams(dimension_semantics=("parallel",),
                                             vmem_limit_bytes=VMEM_LIMIT_BYTES),
        name="ssm",
    )(u, m, bw_pair, cw_pair, lre_row, lim_row, ls_row)


def _out_proj_kernel(n_cast, y_ref, v_ref, vh_ref, x_ref, wglu_ref, bglu_ref, wpool_ref, bpool_ref,
                     pscale_ref, wtop_ref, wbot_ref, *refs):
    cast_src, o_ref, cast_dst, vbuf_ref = refs[:n_cast], refs[n_cast], refs[n_cast + 1:-1], refs[-1]
    _cast_slabs(cast_src, cast_dst)
    i = pl.program_id(0)
    tm = y_ref.shape[0]
    y = y_ref[...].astype(F32)
    g = 0.5 * y * (1.0 + lax.erf(y * (1.0 / math.sqrt(2.0))))
    z = jnp.dot(g.astype(BF16), wglu_ref[...], preferred_element_type=F32) + bglu_ref[...]
    ys = g * (1.0 / (1.0 + jnp.exp(-z)))

    vbuf_ref[0:HALO, :] = jnp.where(i > 0, vh_ref[...].astype(F32), 0.0)
    vbuf_ref[HALO:, :] = v_ref[...].astype(F32)
    t1 = (lax.broadcasted_iota(jnp.int32, (tm, 1), 0) + (i * tm + 1)).astype(F32)
    pooled = []
    for k, w in enumerate(POOL_WINDOWS):
        cols = slice(k * POOL_GROUP, (k + 1) * POOL_GROUP)
        vk = vbuf_ref[:, cols]
        acc, span = vk, 1
        while span < w:
            acc = acc + pltpu.roll(acc, span, axis=0)
            span *= 2
        mean = acc[HALO:, :] / jnp.minimum(t1, float(w))
        pk = (mean - vk[HALO:, :]).astype(BF16)
        mk = jnp.dot(pk, wpool_ref[k], preferred_element_type=F32)
        pooled.append(mk)
    yp = (jnp.concatenate(pooled, axis=1) + bpool_ref[...]) * pscale_ref[...]

    o_ref[...] = (x_ref[...]
                  + jnp.dot(ys.astype(BF16), wtop_ref[...], preferred_element_type=F32)
                  + jnp.dot(yp.astype(BF16), wbot_ref[...], preferred_element_type=F32))


def _out_proj(y, v, x, w_glu, b_glu, w_pool, b_pool, pool_scale, w_out, cast, tm):
    L = x.shape[0]
    halo_blocks = tm // HALO
    const = lambda *shape: pl.BlockSpec(shape, lambda i: (0,) * len(shape))
    c_in, c_out, c_shapes = _cast_specs(cast, L // tm)
    return pl.pallas_call(
        functools.partial(_out_proj_kernel, len(cast)),
        grid=(L // tm,),
        in_specs=[pl.BlockSpec((tm, SSM_WIDTH), lambda i: (i, 0)),
                  pl.BlockSpec((tm, POOL_WIDTH), lambda i: (i, 0)),
                  pl.BlockSpec((HALO, POOL_WIDTH), lambda i: (jnp.maximum(i * halo_blocks - 1, 0), 0)),
                  pl.BlockSpec((tm, D_MODEL), lambda i: (i, 0)),
                  const(SSM_WIDTH, SSM_WIDTH), const(1, SSM_WIDTH),
                  const(len(POOL_WINDOWS), POOL_GROUP, POOL_GROUP), const(1, POOL_WIDTH),
                  const(1, POOL_WIDTH),
                  pl.BlockSpec((SSM_WIDTH, D_MODEL), lambda i: (0, 0)),
                  pl.BlockSpec((POOL_WIDTH, D_MODEL), lambda i: (1, 0))] + c_in,
        out_specs=[pl.BlockSpec((tm, D_MODEL), lambda i: (i, 0))] + c_out,
        out_shape=[jax.ShapeDtypeStruct((L, D_MODEL), F32)] + c_shapes,
        scratch_shapes=[pltpu.VMEM((HALO + tm, POOL_WIDTH), F32)],
        compiler_params=pltpu.CompilerParams(dimension_semantics=("parallel",),
                                             vmem_limit_bytes=VMEM_LIMIT_BYTES),
        name="out_proj",
    )(y, v, v, x, w_glu, b_glu, w_pool, b_pool, pool_scale, w_out, w_out, *cast)


def _ffn_kernel(x_ref, gain_ref, wg_ref, wu_ref, wd_ref, gfin_ref, o_ref, h_ref, r_ref):
    j = pl.program_id(1)
    last = pl.num_programs(1) - 1
    lane_tiles = wg_ref.shape[1] // LANES

    def slab(first, final):
        if first:
            x = x_ref[...]
            h = (x * gain_ref[...]).astype(BF16)
            h_ref[...] = h
            r = lax.rsqrt(jnp.mean(x * x, axis=-1, keepdims=True) + EPS)
            r = jnp.broadcast_to(r, r_ref.shape)
            r_ref[...] = r
        else:
            h, r = h_ref[...], r_ref[...]
        rt = jnp.concatenate([r] * lane_tiles, axis=1)
        a = jnp.dot(h, wg_ref[...], preferred_element_type=F32) * rt
        b = jnp.dot(h, wu_ref[...], preferred_element_type=F32) * rt
        ff = (a * (1.0 / (1.0 + jnp.exp(-a))) * b).astype(BF16)
        acc = (x if first else o_ref[...]) + jnp.dot(ff, wd_ref[...], preferred_element_type=F32)
        o_ref[...] = _rms_norm(acc, gfin_ref[...]) if final else acc

    pl.when(j == 0)(lambda: slab(True, False))
    pl.when(jnp.logical_and(j > 0, j < last))(lambda: slab(False, False))
    pl.when(j == last)(lambda: slab(False, True))


def _ffn(x, gain, w_gate, w_up, w_down, gain_final, tm, tf):
    L = x.shape[0]
    d_ff = w_gate.shape[1]
    assert d_ff // tf >= 2
    return pl.pallas_call(
        _ffn_kernel,
        grid=(L // tm, d_ff // tf),
        in_specs=[pl.BlockSpec((tm, D_MODEL), lambda i, j: (i, 0)),
                  pl.BlockSpec((1, D_MODEL), lambda i, j: (0, 0)),
                  pl.BlockSpec((D_MODEL, tf), lambda i, j: (0, j)),
                  pl.BlockSpec((D_MODEL, tf), lambda i, j: (0, j)),
                  pl.BlockSpec((tf, D_MODEL), lambda i, j: (j, 0)),
                  pl.BlockSpec((1, D_MODEL), lambda i, j: (0, 0))],
        out_specs=pl.BlockSpec((tm, D_MODEL), lambda i, j: (i, 0)),
        out_shape=jax.ShapeDtypeStruct((L, D_MODEL), F32),
        scratch_shapes=[pltpu.VMEM((tm, D_MODEL), BF16), pltpu.VMEM((tm, LANES), F32)],
        compiler_params=pltpu.CompilerParams(dimension_semantics=("parallel", "arbitrary"),
                                             vmem_limit_bytes=VMEM_LIMIT_BYTES),
        name="ffn",
    )(x, gain, w_gate, w_up, w_down, gain_final)


def kernel(x, norm_mix, w_in, lambda_re, lambda_im, log_step, b_re, b_im, c_re, c_im, d_skip,
           w_glu, b_glu, w_pool, b_pool, pool_scale, w_out, norm_ffn, w_gate, w_up, w_down,
           norm_final):
    bsz, L, D = x.shape
    assert bsz == 1 and D == D_MODEL and w_in.shape[0] == 1 and L % 1024 == 0
    G, P = SSM_GROUPS, SSM_STATE
    xs = x.reshape(L, D).astype(F32)

    n_pool = len(POOL_WINDOWS)
    m, bw_pair, cw_pair, w_in_b, w_out_b, w_glu_b, w_pool_b = _ssm_prep(
        lambda_re[0].astype(F32), lambda_im[0].astype(F32), log_step[0].astype(F32),
        b_re[0].astype(F32), b_im[0].astype(F32), c_re[0].astype(F32), c_im[0].astype(F32),
        d_skip[0].astype(F32),
        cast=[w_in[0], w_out[0], w_glu[0], w_pool[0].reshape(n_pool * POOL_GROUP, POOL_GROUP)])
    lre_row = lambda_re[0].astype(F32).reshape(G // 2, 1, 2 * P)
    lim_row = lambda_im[0].astype(F32).reshape(G // 2, 1, 2 * P)
    ls_row = jnp.repeat(log_step[0].astype(F32), P).reshape(G // 2, 1, 2 * P)

    u, v, w_gate_b = _in_proj(xs, norm_mix[0].astype(F32).reshape(1, D), w_in_b, cast=[w_gate[0]], tm=512)

    y = _ssm(u, m, bw_pair, cw_pair, lre_row, lim_row, ls_row)

    x1, w_up_b, w_down_b = _out_proj(
        y, v, xs, w_glu_b, b_glu[0].astype(F32).reshape(1, SSM_WIDTH),
        w_pool_b.reshape(n_pool, POOL_GROUP, POOL_GROUP), b_pool[0].astype(F32).reshape(1, POOL_WIDTH),
        pool_scale[0].astype(F32).reshape(1, POOL_WIDTH), w_out_b, cast=[w_up[0], w_down[0]], tm=512)

    out = _ffn(x1, norm_ffn[0].astype(F32).reshape(1, D), w_gate_b, w_up_b, w_down_b,
               norm_final.astype(F32).reshape(1, D), tm=1024, tf=512)
    return out.reshape(bsz, L, D).astype(x.dtype)
```

```python
import functools
import math

import jax
import jax.numpy as jnp
from jax import lax
from jax.experimental import pallas as pl
from jax.experimental.pallas import tpu as pltpu

F32 = jnp.float32
BF16 = jnp.bfloat16

D_MODEL = 2048
SSM_WIDTH = 1024
POOL_WIDTH = 1024
SSM_GROUP = 16
SSM_GROUPS = 64
SSM_STATE = 64
POOL_WINDOWS = (2, 4, 8, 16)
POOL_GROUP = 256
EPS = 1e-6

CHUNK = 16
CHUNK_LANES = CHUNK * SSM_GROUP
HALO = 16
LANES = 128
GROUPS_PER_BLOCK = LANES // SSM_GROUP
RELAYOUT_ROWS = 16
RELAYOUT_UNITS = 4
SCAN_BLOCK = 8

VMEM_LIMIT_BYTES = 60 * 1024 * 1024

HIGHEST = lax.Precision.HIGHEST


def _rms_norm(x, gain):
    ms = jnp.mean(x * x, axis=-1, keepdims=True)
    return x * lax.rsqrt(ms + EPS) * gain


def _cast_specs(weights, n_steps):
    in_specs, out_specs, out_shapes = [], [], []
    for w in weights:
        rows, cols = w.shape
        slab = pl.BlockSpec((rows // n_steps, cols), lambda i: (jnp.minimum(i, n_steps - 1), 0))
        in_specs.append(slab)
        out_specs.append(slab)
        out_shapes.append(jax.ShapeDtypeStruct((rows, cols), BF16))
    return in_specs, out_specs, out_shapes


def _cast_slabs(src_refs, dst_refs):
    for src, dst in zip(src_refs, dst_refs):
        dst[...] = src[...].astype(dst.dtype)


def _ssm_prep_kernel(n_cast, lre_ref, lim_ref, ls_ref, b_re_ref, b_im_ref, bt_re_ref, bt_im_ref,
                     ct_re_ref, ct_im_ref, d_ref, *refs):
    cast_src, (m_ref, bw_ref, cw_ref), cast_dst = refs[:n_cast], refs[n_cast:n_cast + 3], refs[n_cast + 3:]
    _cast_slabs(cast_src, cast_dst)
    P, H, T, GB = SSM_STATE, SSM_GROUP, CHUNK, GROUPS_PER_BLOCK
    lre, lim = lre_ref[...], lim_ref[...]
    step = jnp.exp(ls_ref[...])
    mag = jnp.exp(lre * step)
    lbr, lbi = mag * jnp.cos(lim * step), mag * jnp.sin(lim * step)
    den = lre * lre + lim * lim
    cfr = ((lbr - 1.0) * lre + lbi * lim) / den
    cfi = (lbi * lre - (lbr - 1.0) * lim) / den
    eye = (lax.broadcasted_iota(jnp.int32, (P, P), 0)
           == lax.broadcasted_iota(jnp.int32, (P, P), 1)).astype(F32)
    rows = jnp.concatenate([lbr, lbi, cfr, cfi], axis=0)
    cols = lax.dot_general(eye, rows, (((1,), (1,)), ((), ())), precision=HIGHEST,
                           preferred_element_type=F32)

    lane = lax.broadcasted_iota(jnp.int32, (1, CHUNK_LANES), 1)
    k_idx = lane // H
    sub = lax.broadcasted_iota(jnp.int32, (H, CHUNK_LANES), 0)
    lane_h = lax.broadcasted_iota(jnp.int32, (H, CHUNK_LANES), 1)
    rep = (lane_h % H == sub).astype(F32)

    def tile(a):
        return jnp.dot(a, rep, precision=HIGHEST, preferred_element_type=F32)

    def cmul(ar, ai, br, bi):
        return ar * br - ai * bi, ar * bi + ai * br

    bw_ref[...] = jnp.zeros_like(bw_ref)
    cw_ref[...] = jnp.zeros_like(cw_ref)
    for g in range(GB):
        col = lambda quantity: cols[:, quantity * GB + g:quantity * GB + g + 1]
        l1r, l1i, cr, ci = col(0), col(1), col(2), col(3)
        sq = [(l1r, l1i)]
        for _ in range(3):
            sq.append(cmul(*sq[-1], *sq[-1]))
        pkr = pki = prr = pri = None
        for bit, (fr, fi) in enumerate(sq):
            on = ((k_idx >> bit) & 1) == 1
            ur, ui = jnp.where(on, fr, 1.0), jnp.where(on, fi, 0.0)
            dr, di = jnp.where(on, 1.0, fr), jnp.where(on, 0.0, fi)
            if bit == 0:
                pkr, pki, prr, pri = ur, ui, dr, di
            else:
                pkr, pki = cmul(pkr, pki, ur, ui)
                prr, pri = cmul(prr, pri, dr, di)

        bb_re, bb_im = cmul(cr, ci, b_re_ref[g], b_im_ref[g])
        bbt_re, bbt_im = cmul(cfr[g:g + 1, :], cfi[g:g + 1, :], bt_re_ref[g], bt_im_ref[g])

        q, j = g // 2, g % 2
        r_re = slice(j * P, (j + 1) * P)
        r_im = slice((2 + j) * P, (3 + j) * P)
        c_g = slice(j * CHUNK_LANES, (j + 1) * CHUNK_LANES)
        w_re, w_im = cmul(prr, pri, tile(bb_re), tile(bb_im))
        bw_ref[q, r_re, c_g] = w_re.astype(bw_ref.dtype)
        bw_ref[q, r_im, c_g] = w_im.astype(bw_ref.dtype)
        g_re, g_im = cmul(pkr, pki, tile(ct_re_ref[g]), tile(ct_im_ref[g]))
        g1_re, g1_im = cmul(l1r, l1i, g_re, g_im)
        cw_ref[q, r_re, c_g] = g1_re.astype(cw_ref.dtype)
        cw_ref[q, r_im, c_g] = (-g1_im).astype(cw_ref.dtype)
        kt = (jnp.dot(bbt_re, g_re, precision=HIGHEST, preferred_element_type=F32)
              - jnp.dot(bbt_im, g_im, precision=HIGHEST, preferred_element_type=F32))
        kt = kt + jnp.where(lane_h == sub, d_ref[g], 0.0)
        for tau in range(T):
            blk = kt if tau == 0 else jnp.where(lane_h >= tau * H, pltpu.roll(kt, tau * H, axis=1), 0.0)
            m_ref[g, tau * H:(tau + 1) * H, :] = blk.astype(m_ref.dtype)


def _ssm_prep(lambda_re, lambda_im, log_step, b_re, b_im, c_re, c_im, d_skip, cast):
    G, P, H, GB = SSM_GROUPS, SSM_STATE, SSM_GROUP, GROUPS_PER_BLOCK
    n_steps = G // GB
    blk = lambda *shape: pl.BlockSpec(shape, lambda i: (i,) + (0,) * (len(shape) - 1))
    c_in, c_out, c_shapes = _cast_specs(cast, n_steps)
    return pl.pallas_call(
        functools.partial(_ssm_prep_kernel, len(cast)),
        grid=(n_steps,),
        in_specs=[blk(GB, P), blk(GB, P), blk(GB, 1),
                  blk(GB, P, H), blk(GB, P, H), blk(GB, H, P), blk(GB, H, P),
                  blk(GB, P, H), blk(GB, P, H), blk(GB, H, 1)] + c_in,
        out_specs=[blk(GB, CHUNK_LANES, CHUNK_LANES), blk(GB // 2, 4 * P, 2 * CHUNK_LANES),
                   blk(GB // 2, 4 * P, 2 * CHUNK_LANES)] + c_out,
        out_shape=[jax.ShapeDtypeStruct((G, CHUNK_LANES, CHUNK_LANES), BF16),
                   jax.ShapeDtypeStruct((G // 2, 4 * P, 2 * CHUNK_LANES), BF16),
                   jax.ShapeDtypeStruct((G // 2, 4 * P, 2 * CHUNK_LANES), BF16)] + c_shapes,
        compiler_params=pltpu.CompilerParams(dimension_semantics=("parallel",),
                                             vmem_limit_bytes=VMEM_LIMIT_BYTES),
        name="ssm_prep",
    )(lambda_re, lambda_im, log_step.reshape(G, 1),
      b_re, b_im, jnp.swapaxes(b_re, 1, 2), jnp.swapaxes(b_im, 1, 2),
      jnp.swapaxes(c_re, 1, 2), jnp.swapaxes(c_im, 1, 2), d_skip.reshape(G, H, 1), *cast)


def _granule_transpose(units):
    lane = lax.broadcasted_iota(jnp.int32, units[0][0].shape, 1)
    units = [list(a) for a in units]
    for s in (4, 2, 1):
        upper = ((lane // SSM_GROUP) & s) != 0
        for a in units:
            for i in range(GROUPS_PER_BLOCK):
                if i & s:
                    continue
                lo, hi = a[i], a[i + s]
                a[i] = jnp.where(upper, pltpu.roll(hi, s * SSM_GROUP, axis=1), lo)
                a[i + s] = jnp.where(upper, hi, pltpu.roll(lo, LANES - s * SSM_GROUP, axis=1))
    return units


def _relayout_units(n_chunks):
    return [(b, cb, hf) for b in range(SSM_WIDTH // LANES)
            for cb in range(n_chunks // RELAYOUT_ROWS) for hf in range(2)]


def _relayout_to_groups(src_ref, dst_ref, n_chunks):
    half = CHUNK // 2
    rb = RELAYOUT_ROWS
    where = _relayout_units(n_chunks)
    batch = 2 * RELAYOUT_UNITS
    for i in range(0, len(where), batch):
        units = [[src_ref[b, pl.ds(cb * rb * CHUNK + hf * half + t, rb, stride=CHUNK), :].astype(BF16)
                  for t in range(half)] for (b, cb, hf) in where[i:i + batch]]
        units = _granule_transpose(units)
        for (b, cb, hf), unit in zip(where[i:i + batch], units):
            for g in range(GROUPS_PER_BLOCK):
                dst_ref[b * GROUPS_PER_BLOCK + g, cb * rb:(cb + 1) * rb, hf * LANES:(hf + 1) * LANES] = unit[g]


def _relayout_from_groups(src_ref, dst_ref, n_chunks):
    half = CHUNK // 2
    rb = RELAYOUT_ROWS
    where = _relayout_units(n_chunks)
    batch = 2 * RELAYOUT_UNITS
    for i in range(0, len(where), batch):
        units = [[src_ref[b * GROUPS_PER_BLOCK + g, cb * rb:(cb + 1) * rb, hf * LANES:(hf + 1) * LANES]
                  for g in range(GROUPS_PER_BLOCK)] for (b, cb, hf) in where[i:i + batch]]
        units = _granule_transpose(units)
        for (b, cb, hf), unit in zip(where[i:i + batch], units):
            for t in range(half):
                dst_ref[b, pl.ds(cb * rb * CHUNK + hf * half + t, rb, stride=CHUNK), :] = unit[t].astype(F32)


def _in_proj_kernel(n_cast, x_ref, gain_ref, w_ref, *refs):
    cast_src, (xg_ref, v_ref), cast_dst = refs[:n_cast], refs[n_cast:n_cast + 2], refs[n_cast + 2:-2]
    park = refs[-2:]
    _cast_slabs(cast_src, cast_dst)
    s = pl.program_id(0)

    @pl.when(s == 0)
    def _():
        park[1][...] = jnp.zeros_like(park[1])

    def step(p_prev, p_next):
        _relayout_to_groups(p_prev, xg_ref, x_ref.shape[0] // CHUNK)
        h = _rms_norm(x_ref[...], gain_ref[...]).astype(BF16)
        p = jnp.dot(h, w_ref[...], preferred_element_type=F32)
        v_ref[...] = p[:, SSM_WIDTH:].astype(v_ref.dtype)
        for b in range(SSM_WIDTH // LANES):
            p_next[b] = p[:, b * LANES:(b + 1) * LANES]

    pl.when(s % 2 == 0)(lambda: step(park[1], park[0]))
    pl.when(s % 2 == 1)(lambda: step(park[0], park[1]))


def _in_proj(x, gain, w_in, cast, tm):
    L = x.shape[0]
    n = L // tm
    c_in, c_out, c_shapes = _cast_specs(cast, n)
    cur = lambda s: (jnp.minimum(s, n - 1), 0)
    return pl.pallas_call(
        functools.partial(_in_proj_kernel, len(cast)),
        grid=(n + 1,),
        in_specs=[pl.BlockSpec((tm, D_MODEL), cur),
                  pl.BlockSpec((1, D_MODEL), lambda s: (0, 0)),
                  pl.BlockSpec((D_MODEL, D_MODEL), lambda s: (0, 0))] + c_in,
        out_specs=[pl.BlockSpec((SSM_GROUPS, tm // CHUNK, CHUNK_LANES), lambda s: (0, jnp.maximum(s - 1, 0), 0)),
                   pl.BlockSpec((tm, POOL_WIDTH), cur)] + c_out,
        out_shape=[jax.ShapeDtypeStruct((SSM_GROUPS, L // CHUNK, CHUNK_LANES), BF16),
                   jax.ShapeDtypeStruct((L, POOL_WIDTH), BF16)] + c_shapes,
        scratch_shapes=[pltpu.VMEM((SSM_WIDTH // LANES, tm, LANES), F32)] * 2,
        compiler_params=pltpu.CompilerParams(dimension_semantics=("arbitrary",),
                                             vmem_limit_bytes=VMEM_LIMIT_BYTES),
        name="in_proj",
    )(x, gain, w_in, *cast)


def _cmul(ar, ai, br, bi):
    return ar * br - ai * bi, ar * bi + ai * br


def _ssm_kernel(xg_ref, m_ref, bw_ref, cw_ref, lre_ref, lim_ref, ls_ref, yg_ref,
                fr_ref, fi_ref, cr_ref, ci_ref):
    n_chunks = xg_ref.shape[1]
    sb = SCAN_BLOCK
    n_sb = n_chunks // sb

    blk3 = (n_sb, sb, 2 * SSM_STATE)
    row_in = lax.broadcasted_iota(jnp.int32, blk3, 1)
    brow = lax.broadcasted_iota(jnp.int32, (n_sb, 2 * SSM_STATE), 0)
    for q in range(GROUPS_PER_BLOCK // 2):
        x0, x1 = xg_ref[2 * q], xg_ref[2 * q + 1]
        xp = jnp.concatenate([x0, x1], axis=1)
        e = lax.dot_general(xp, bw_ref[q], (((1,), (1,)), ((), ())), preferred_element_type=F32)
        step = jnp.exp(ls_ref[q])
        zr, zi = lre_ref[q] * step * CHUNK, lim_ref[q] * step * CHUNK
        mag = jnp.exp(zr)
        a1r, a1i = mag * jnp.cos(zi), mag * jnp.sin(zi)

        fr = e[:, :2 * SSM_STATE].reshape(blk3)
        fi = e[:, 2 * SSM_STATE:].reshape(blk3)
        ar, ai = a1r, a1i
        d = 1
        while d < sb:
            tr = jnp.where(row_in >= d, pltpu.roll(fr, d, axis=1), 0.0)
            ti = jnp.where(row_in >= d, pltpu.roll(fi, d, axis=1), 0.0)
            fr, fi = fr + ar * tr - ai * ti, fi + ar * ti + ai * tr
            ar, ai = _cmul(ar, ai, ar, ai)
            d *= 2
        fr_ref[...] = fr.reshape(n_chunks, 2 * SSM_STATE)
        fi_ref[...] = fi.reshape(n_chunks, 2 * SSM_STATE)
        gr = fr_ref[pl.ds(sb - 1, n_sb, stride=sb), :]
        gi = fi_ref[pl.ds(sb - 1, n_sb, stride=sb), :]
        gr = jnp.where(brow >= 1, pltpu.roll(gr, 1, axis=0), 0.0)
        gi = jnp.where(brow >= 1, pltpu.roll(gi, 1, axis=0), 0.0)
        d = 1
        while d < n_sb:
            tr = jnp.where(brow >= d, pltpu.roll(gr, d, axis=0), 0.0)
            ti = jnp.where(brow >= d, pltpu.roll(gi, d, axis=0), 0.0)
            gr, gi = gr + ar * tr - ai * ti, gi + ar * ti + ai * tr
            ar, ai = _cmul(ar, ai, ar, ai)
            d *= 2
        pr, pi = jnp.ones_like(a1r), jnp.zeros_like(a1i)
        for r in range(sb):
            cr_ref[pl.ds(r, n_sb, stride=sb), :] = pr * gr - pi * gi
            ci_ref[pl.ds(r, n_sb, stride=sb), :] = pr * gi + pi * gr
            pr, pi = _cmul(pr, pi, a1r, a1i)
        sr = jnp.where(row_in >= 1, pltpu.roll(fr, 1, axis=1), 0.0).reshape(n_chunks, 2 * SSM_STATE)
        si = jnp.where(row_in >= 1, pltpu.roll(fi, 1, axis=1), 0.0).reshape(n_chunks, 2 * SSM_STATE)
        sr = sr + cr_ref[...]
        si = si + ci_ref[...]
        s = jnp.concatenate([sr, si], axis=1).astype(BF16)
        yc = jnp.dot(s, cw_ref[q], preferred_element_type=F32)
        y0 = jnp.dot(x0, m_ref[2 * q], preferred_element_type=F32) + yc[:, :CHUNK_LANES]
        y1 = jnp.dot(x1, m_ref[2 * q + 1], preferred_element_type=F32) + yc[:, CHUNK_LANES:]
        yg_ref[2 * q] = y0.astype(yg_ref.dtype)
        yg_ref[2 * q + 1] = y1.astype(yg_ref.dtype)


def _ssm(xg, m, bw_pair, cw_pair, lre_row, lim_row, ls_row):
    n_chunks = xg.shape[1]
    n_blocks = SSM_WIDTH // LANES
    ppb = GROUPS_PER_BLOCK // 2
    return pl.pallas_call(
        _ssm_kernel,
        grid=(n_blocks,),
        in_specs=[pl.BlockSpec((GROUPS_PER_BLOCK, n_chunks, CHUNK_LANES), lambda b: (b, 0, 0)),
                  pl.BlockSpec((GROUPS_PER_BLOCK, CHUNK_LANES, CHUNK_LANES), lambda b: (b, 0, 0)),
                  pl.BlockSpec((ppb, 4 * SSM_STATE, 2 * CHUNK_LANES), lambda b: (b, 0, 0)),
                  pl.BlockSpec((ppb, 4 * SSM_STATE, 2 * CHUNK_LANES), lambda b: (b, 0, 0)),
                  pl.BlockSpec((ppb, 1, 2 * SSM_STATE), lambda b: (b, 0, 0)),
                  pl.BlockSpec((ppb, 1, 2 * SSM_STATE), lambda b: (b, 0, 0)),
                  pl.BlockSpec((ppb, 1, 2 * SSM_STATE), lambda b: (b, 0, 0))],
        out_specs=pl.BlockSpec((GROUPS_PER_BLOCK, n_chunks, CHUNK_LANES), lambda b: (b, 0, 0)),
        out_shape=jax.ShapeDtypeStruct(xg.shape, BF16),
        scratch_shapes=[pltpu.VMEM((n_chunks, 2 * SSM_STATE), F32)] * 4,
        compiler_params=pltpu.CompilerParams(dimension_semantics=("parallel",),
                                             vmem_limit_bytes=VMEM_LIMIT_BYTES),
        name="ssm",
    )(xg, m, bw_pair, cw_pair, lre_row, lim_row, ls_row)


def _out_proj_kernel(n_cast, yg_ref, v_ref, vh_ref, x_ref, wglu_ref, bglu_ref, wpool_ref, bpool_ref,
                     pscale_ref, wtop_ref, wbot_ref, *refs):
    cast_src, o_ref, cast_dst = refs[:n_cast], refs[n_cast], refs[n_cast + 1:-3]
    park, vbuf_ref = refs[-3:-1], refs[-1]
    _cast_slabs(cast_src, cast_dst)
    s = pl.program_id(0)
    i = s - 1
    tm = x_ref.shape[0]

    @pl.when(s == 0)
    def _():
        park[1][...] = jnp.zeros_like(park[1])

    def step(y_prev, y_next):
        _relayout_from_groups(yg_ref, y_next, tm // CHUNK)
        y = jnp.concatenate([y_prev[b] for b in range(SSM_WIDTH // LANES)], axis=1)
        g = 0.5 * y * (1.0 + lax.erf(y * (1.0 / math.sqrt(2.0))))
        z = jnp.dot(g.astype(BF16), wglu_ref[...], preferred_element_type=F32) + bglu_ref[...]
        ys = g * (1.0 / (1.0 + jnp.exp(-z)))

        vbuf_ref[0:HALO, :] = jnp.where(i > 0, vh_ref[...].astype(F32), 0.0)
        vbuf_ref[HALO:, :] = v_ref[...].astype(F32)
        t1 = (lax.broadcasted_iota(jnp.int32, (tm, 1), 0) + (i * tm + 1)).astype(F32)
        pooled = []
        for k, w in enumerate(POOL_WINDOWS):
            cols = slice(k * POOL_GROUP, (k + 1) * POOL_GROUP)
            vk = vbuf_ref[:, cols]
            acc, span = vk, 1
            while span < w:
                acc = acc + pltpu.roll(acc, span, axis=0)
                span *= 2
            mean = acc[HALO:, :] / jnp.minimum(t1, float(w))
            pk = (mean - vk[HALO:, :]).astype(BF16)
            pooled.append(jnp.dot(pk, wpool_ref[k], preferred_element_type=F32))
        yp = (jnp.concatenate(pooled, axis=1) + bpool_ref[...]) * pscale_ref[...]

        o_ref[...] = (x_ref[...]
                      + jnp.dot(ys.astype(BF16), wtop_ref[...], preferred_element_type=F32)
                      + jnp.dot(yp.astype(BF16), wbot_ref[...], preferred_element_type=F32))

    pl.when(s % 2 == 0)(lambda: step(park[1], park[0]))
    pl.when(s % 2 == 1)(lambda: step(park[0], park[1]))


def _out_proj(yg, v, x, w_glu, b_glu, w_pool, b_pool, pool_scale, w_out, cast, tm):
    L = x.shape[0]
    n = L // tm
    halo_blocks = tm // HALO
    const = lambda *shape: pl.BlockSpec(shape, lambda s: (0,) * len(shape))
    c_in, c_out, c_shapes = _cast_specs(cast, n)
    prev = lambda s: (jnp.maximum(s - 1, 0), 0)
    return pl.pallas_call(
        functools.partial(_out_proj_kernel, len(cast)),
        grid=(n + 1,),
        in_specs=[pl.BlockSpec((SSM_GROUPS, tm // CHUNK, CHUNK_LANES), lambda s: (0, jnp.minimum(s, n - 1), 0)),
                  pl.BlockSpec((tm, POOL_WIDTH), prev),
                  pl.BlockSpec((HALO, POOL_WIDTH), lambda s: (jnp.maximum((s - 1) * halo_blocks - 1, 0), 0)),
                  pl.BlockSpec((tm, D_MODEL), prev),
                  const(SSM_WIDTH, SSM_WIDTH), const(1, SSM_WIDTH),
                  const(len(POOL_WINDOWS), POOL_GROUP, POOL_GROUP), const(1, POOL_WIDTH),
                  const(1, POOL_WIDTH),
                  pl.BlockSpec((SSM_WIDTH, D_MODEL), lambda s: (0, 0)),
                  pl.BlockSpec((POOL_WIDTH, D_MODEL), lambda s: (1, 0))] + c_in,
        out_specs=[pl.BlockSpec((tm, D_MODEL), prev)] + c_out,
        out_shape=[jax.ShapeDtypeStruct((L, D_MODEL), F32)] + c_shapes,
        scratch_shapes=[pltpu.VMEM((SSM_WIDTH // LANES, tm, LANES), F32)] * 2
                       + [pltpu.VMEM((HALO + tm, POOL_WIDTH), F32)],
        compiler_params=pltpu.CompilerParams(dimension_semantics=("arbitrary",),
                                             vmem_limit_bytes=VMEM_LIMIT_BYTES),
        name="out_proj",
    )(yg, v, v, x, w_glu, b_glu, w_pool, b_pool, pool_scale, w_out, w_out, *cast)


def _ffn_kernel(x_ref, gain_ref, wg_ref, wu_ref, wd_ref, gfin_ref, o_ref, h_ref, r_ref):
    j = pl.program_id(1)
    last = pl.num_programs(1) - 1
    lane_tiles = wg_ref.shape[1] // LANES

    def slab(first, final):
        if first:
            x = x_ref[...]
            h = (x * gain_ref[...]).astype(BF16)
            h_ref[...] = h
            r = lax.rsqrt(jnp.mean(x * x, axis=-1, keepdims=True) + EPS)
            r = jnp.broadcast_to(r, r_ref.shape)
            r_ref[...] = r
        else:
            h, r = h_ref[...], r_ref[...]
        rt = jnp.concatenate([r] * lane_tiles, axis=1)
        a = jnp.dot(h, wg_ref[...], preferred_element_type=F32) * rt
        b = jnp.dot(h, wu_ref[...], preferred_element_type=F32) * rt
        ff = (a * (1.0 / (1.0 + jnp.exp(-a))) * b).astype(BF16)
        acc = (x if first else o_ref[...]) + jnp.dot(ff, wd_ref[...], preferred_element_type=F32)
        o_ref[...] = _rms_norm(acc, gfin_ref[...]) if final else acc

    pl.when(j == 0)(lambda: slab(True, False))
    pl.when(jnp.logical_and(j > 0, j < last))(lambda: slab(False, False))
    pl.when(j == last)(lambda: slab(False, True))


def _ffn(x, gain, w_gate, w_up, w_down, gain_final, tm, tf):
    L = x.shape[0]
    d_ff = w_gate.shape[1]
    assert d_ff // tf >= 2
    return pl.pallas_call(
        _ffn_kernel,
        grid=(L // tm, d_ff // tf),
        in_specs=[pl.BlockSpec((tm, D_MODEL), lambda i, j: (i, 0)),
                  pl.BlockSpec((1, D_MODEL), lambda i, j: (0, 0)),
                  pl.BlockSpec((D_MODEL, tf), lambda i, j: (0, j)),
                  pl.BlockSpec((D_MODEL, tf), lambda i, j: (0, j)),
                  pl.BlockSpec((tf, D_MODEL), lambda i, j: (j, 0)),
                  pl.BlockSpec((1, D_MODEL), lambda i, j: (0, 0))],
        out_specs=pl.BlockSpec((tm, D_MODEL), lambda i, j: (i, 0)),
        out_shape=jax.ShapeDtypeStruct((L, D_MODEL), F32),
        scratch_shapes=[pltpu.VMEM((tm, D_MODEL), BF16), pltpu.VMEM((tm, LANES), F32)],
        compiler_params=pltpu.CompilerParams(dimension_semantics=("parallel", "arbitrary"),
                                             vmem_limit_bytes=VMEM_LIMIT_BYTES),
        name="ffn",
    )(x, gain, w_gate, w_up, w_down, gain_final)


def kernel(x, norm_mix, w_in, lambda_re, lambda_im, log_step, b_re, b_im, c_re, c_im, d_skip,
           w_glu, b_glu, w_pool, b_pool, pool_scale, w_out, norm_ffn, w_gate, w_up, w_down,
           norm_final):
    bsz, L, D = x.shape
    assert bsz == 1 and D == D_MODEL and w_in.shape[0] == 1 and L % 1024 == 0
    G, P = SSM_GROUPS, SSM_STATE
    xs = x.reshape(L, D).astype(F32)

    n_pool = len(POOL_WINDOWS)
    m, bw_pair, cw_pair, w_in_b, w_out_b, w_glu_b, w_pool_b = _ssm_prep(
        lambda_re[0].astype(F32), lambda_im[0].astype(F32), log_step[0].astype(F32),
        b_re[0].astype(F32), b_im[0].astype(F32), c_re[0].astype(F32), c_im[0].astype(F32),
        d_skip[0].astype(F32),
        cast=[w_in[0], w_out[0], w_glu[0], w_pool[0].reshape(n_pool * POOL_GROUP, POOL_GROUP)])
    lre_row = lambda_re[0].astype(F32).reshape(G // 2, 1, 2 * P)
    lim_row = lambda_im[0].astype(F32).reshape(G // 2, 1, 2 * P)
    ls_row = jnp.repeat(log_step[0].astype(F32), P).reshape(G // 2, 1, 2 * P)

    xg, v, w_gate_b = _in_proj(xs, norm_mix[0].astype(F32).reshape(1, D), w_in_b, cast=[w_gate[0]], tm=512)

    yg = _ssm(xg, m, bw_pair, cw_pair, lre_row, lim_row, ls_row)

    x1, w_up_b, w_down_b = _out_proj(
        yg, v, xs, w_glu_b, b_glu[0].astype(F32).reshape(1, SSM_WIDTH),
        w_pool_b.reshape(n_pool, POOL_GROUP, POOL_GROUP), b_pool[0].astype(F32).reshape(1, POOL_WIDTH),
        pool_scale[0].astype(F32).reshape(1, POOL_WIDTH), w_out_b, cast=[w_up[0], w_down[0]], tm=512)

    out = _ffn(x1, norm_ffn[0].astype(F32).reshape(1, D), w_gate_b, w_up_b, w_down_b,
               norm_final.astype(F32).reshape(1, D), tm=1024, tf=512)
    return out.reshape(bsz, L, D).astype(x.dtype)
```

```python
import functools
import math

import jax
import jax.numpy as jnp
from jax import lax
from jax.experimental import pallas as pl
from jax.experimental.pallas import tpu as pltpu

F32 = jnp.float32
BF16 = jnp.bfloat16

D_MODEL = 2048
SSM_WIDTH = 1024
POOL_WIDTH = 1024
SSM_GROUP = 16
SSM_GROUPS = 64
SSM_STATE = 64
POOL_WINDOWS = (2, 4, 8, 16)
POOL_GROUP = 256
EPS = 1e-6

CHUNK = 16
CHUNK_LANES = CHUNK * SSM_GROUP
HALO = 16
LANES = 128
GROUPS_PER_BLOCK = LANES // SSM_GROUP
RELAYOUT_ROWS = 16
RELAYOUT_UNITS = 4
SCAN_BLOCK = 8

VMEM_LIMIT_BYTES = 60 * 1024 * 1024

HIGHEST = lax.Precision.HIGHEST


def _rms_norm(x, gain):
    ms = jnp.mean(x * x, axis=-1, keepdims=True)
    return x * lax.rsqrt(ms + EPS) * gain


def _cast_specs(weights, n_steps):
    in_specs, out_specs, out_shapes = [], [], []
    for w in weights:
        rows, cols = w.shape
        slab = pl.BlockSpec((rows // n_steps, cols), lambda i: (jnp.minimum(i, n_steps - 1), 0))
        in_specs.append(slab)
        out_specs.append(slab)
        out_shapes.append(jax.ShapeDtypeStruct((rows, cols), BF16))
    return in_specs, out_specs, out_shapes


def _cast_slabs(src_refs, dst_refs):
    for src, dst in zip(src_refs, dst_refs):
        dst[...] = src[...].astype(dst.dtype)


def _ssm_prep_kernel(n_cast, lre_ref, lim_ref, ls_ref, b_re_ref, b_im_ref, bt_re_ref, bt_im_ref,
                     ct_re_ref, ct_im_ref, d_ref, *refs):
    cast_src, (m_ref, bw_ref, cw_ref), cast_dst = refs[:n_cast], refs[n_cast:n_cast + 3], refs[n_cast + 3:]
    _cast_slabs(cast_src, cast_dst)
    P, H, T, GB = SSM_STATE, SSM_GROUP, CHUNK, GROUPS_PER_BLOCK
    lre, lim = lre_ref[...], lim_ref[...]
    step = jnp.exp(ls_ref[...])
    mag = jnp.exp(lre * step)
    lbr, lbi = mag * jnp.cos(lim * step), mag * jnp.sin(lim * step)
    den = lre * lre + lim * lim
    cfr = ((lbr - 1.0) * lre + lbi * lim) / den
    cfi = (lbi * lre - (lbr - 1.0) * lim) / den
    eye = (lax.broadcasted_iota(jnp.int32, (P, P), 0)
           == lax.broadcasted_iota(jnp.int32, (P, P), 1)).astype(F32)
    rows = jnp.concatenate([lbr, lbi, cfr, cfi], axis=0)
    cols = lax.dot_general(eye, rows, (((1,), (1,)), ((), ())), precision=HIGHEST,
                           preferred_element_type=F32)

    lane = lax.broadcasted_iota(jnp.int32, (1, CHUNK_LANES), 1)
    k_idx = lane // H
    sub = lax.broadcasted_iota(jnp.int32, (H, CHUNK_LANES), 0)
    lane_h = lax.broadcasted_iota(jnp.int32, (H, CHUNK_LANES), 1)
    rep = (lane_h % H == sub).astype(F32)

    def tile(a):
        return jnp.dot(a, rep, precision=HIGHEST, preferred_element_type=F32)

    def cmul(ar, ai, br, bi):
        return ar * br - ai * bi, ar * bi + ai * br

    bw_ref[...] = jnp.zeros_like(bw_ref)
    cw_ref[...] = jnp.zeros_like(cw_ref)
    for g in range(GB):
        col = lambda quantity: cols[:, quantity * GB + g:quantity * GB + g + 1]
        l1r, l1i, cr, ci = col(0), col(1), col(2), col(3)
        sq = [(l1r, l1i)]
        for _ in range(3):
            sq.append(cmul(*sq[-1], *sq[-1]))
        pkr = pki = prr = pri = None
        for bit, (fr, fi) in enumerate(sq):
            on = ((k_idx >> bit) & 1) == 1
            ur, ui = jnp.where(on, fr, 1.0), jnp.where(on, fi, 0.0)
            dr, di = jnp.where(on, 1.0, fr), jnp.where(on, 0.0, fi)
            if bit == 0:
                pkr, pki, prr, pri = ur, ui, dr, di
            else:
                pkr, pki = cmul(pkr, pki, ur, ui)
                prr, pri = cmul(prr, pri, dr, di)

        bb_re, bb_im = cmul(cr, ci, b_re_ref[g], b_im_ref[g])
        bbt_re, bbt_im = cmul(cfr[g:g + 1, :], cfi[g:g + 1, :], bt_re_ref[g], bt_im_ref[g])

        q, j = g // 2, g % 2
        r_re = slice(j * P, (j + 1) * P)
        r_im = slice((2 + j) * P, (3 + j) * P)
        c_g = slice(j * CHUNK_LANES, (j + 1) * CHUNK_LANES)
        w_re, w_im = cmul(prr, pri, tile(bb_re), tile(bb_im))
        bw_ref[q, r_re, c_g] = w_re.astype(bw_ref.dtype)
        bw_ref[q, r_im, c_g] = w_im.astype(bw_ref.dtype)
        g_re, g_im = cmul(pkr, pki, tile(ct_re_ref[g]), tile(ct_im_ref[g]))
        g1_re, g1_im = cmul(l1r, l1i, g_re, g_im)
        cw_ref[q, r_re, c_g] = g1_re.astype(cw_ref.dtype)
        cw_ref[q, r_im, c_g] = (-g1_im).astype(cw_ref.dtype)
        kt = (jnp.dot(bbt_re, g_re, precision=HIGHEST, preferred_element_type=F32)
              - jnp.dot(bbt_im, g_im, precision=HIGHEST, preferred_element_type=F32))
        kt = kt + jnp.where(lane_h == sub, d_ref[g], 0.0)
        for tau in range(T):
            blk = kt if tau == 0 else jnp.where(lane_h >= tau * H, pltpu.roll(kt, tau * H, axis=1), 0.0)
            m_ref[g, tau * H:(tau + 1) * H, :] = blk.astype(m_ref.dtype)


def _ssm_prep(lambda_re, lambda_im, log_step, b_re, b_im, c_re, c_im, d_skip, cast):
    G, P, H, GB = SSM_GROUPS, SSM_STATE, SSM_GROUP, GROUPS_PER_BLOCK
    n_steps = G // GB
    blk = lambda *shape: pl.BlockSpec(shape, lambda i: (i,) + (0,) * (len(shape) - 1))
    c_in, c_out, c_shapes = _cast_specs(cast, n_steps)
    return pl.pallas_call(
        functools.partial(_ssm_prep_kernel, len(cast)),
        grid=(n_steps,),
        in_specs=[blk(GB, P), blk(GB, P), blk(GB, 1),
                  blk(GB, P, H), blk(GB, P, H), blk(GB, H, P), blk(GB, H, P),
                  blk(GB, P, H), blk(GB, P, H), blk(GB, H, 1)] + c_in,
        out_specs=[blk(GB, CHUNK_LANES, CHUNK_LANES), blk(GB // 2, 4 * P, 2 * CHUNK_LANES),
                   blk(GB // 2, 4 * P, 2 * CHUNK_LANES)] + c_out,
        out_shape=[jax.ShapeDtypeStruct((G, CHUNK_LANES, CHUNK_LANES), BF16),
                   jax.ShapeDtypeStruct((G // 2, 4 * P, 2 * CHUNK_LANES), BF16),
                   jax.ShapeDtypeStruct((G // 2, 4 * P, 2 * CHUNK_LANES), BF16)] + c_shapes,
        compiler_params=pltpu.CompilerParams(dimension_semantics=("parallel",),
                                             vmem_limit_bytes=VMEM_LIMIT_BYTES),
        name="ssm_prep",
    )(lambda_re, lambda_im, log_step.reshape(G, 1),
      b_re, b_im, jnp.swapaxes(b_re, 1, 2), jnp.swapaxes(b_im, 1, 2),
      jnp.swapaxes(c_re, 1, 2), jnp.swapaxes(c_im, 1, 2), d_skip.reshape(G, H, 1), *cast)


def _granule_transpose(units):
    lane = lax.broadcasted_iota(jnp.int32, units[0][0].shape, 1)
    units = [list(a) for a in units]
    for s in (4, 2, 1):
        upper = ((lane // SSM_GROUP) & s) != 0
        for a in units:
            for i in range(GROUPS_PER_BLOCK):
                if i & s:
                    continue
                lo, hi = a[i], a[i + s]
                a[i] = jnp.where(upper, pltpu.roll(hi, s * SSM_GROUP, axis=1), lo)
                a[i + s] = jnp.where(upper, hi, pltpu.roll(lo, LANES - s * SSM_GROUP, axis=1))
    return units


def _relayout_units(n_chunks):
    return [(b, cb, hf) for b in range(SSM_WIDTH // LANES)
            for cb in range(n_chunks // RELAYOUT_ROWS) for hf in range(2)]


def _relayout_to_groups(src_ref, dst_ref, n_chunks):
    half = CHUNK // 2
    rb = RELAYOUT_ROWS
    where = _relayout_units(n_chunks)
    batch = 2 * RELAYOUT_UNITS
    for i in range(0, len(where), batch):
        units = [[src_ref[b, pl.ds(cb * rb * CHUNK + hf * half + t, rb, stride=CHUNK), :].astype(BF16)
                  for t in range(half)] for (b, cb, hf) in where[i:i + batch]]
        units = _granule_transpose(units)
        for (b, cb, hf), unit in zip(where[i:i + batch], units):
            for g in range(GROUPS_PER_BLOCK):
                dst_ref[b * GROUPS_PER_BLOCK + g, cb * rb:(cb + 1) * rb, hf * LANES:(hf + 1) * LANES] = unit[g]


def _relayout_from_groups(src_ref, dst_ref, n_chunks, blocks=range(SSM_WIDTH // LANES)):
    half = CHUNK // 2
    rb = RELAYOUT_ROWS
    where = [u for u in _relayout_units(n_chunks) if u[0] in blocks]
    batch = 2 * RELAYOUT_UNITS
    for i in range(0, len(where), batch):
        units = [[src_ref[b * GROUPS_PER_BLOCK + g, cb * rb:(cb + 1) * rb, hf * LANES:(hf + 1) * LANES]
                  for g in range(GROUPS_PER_BLOCK)] for (b, cb, hf) in where[i:i + batch]]
        units = _granule_transpose(units)
        for (b, cb, hf), unit in zip(where[i:i + batch], units):
            for t in range(half):
                dst_ref[b, pl.ds(cb * rb * CHUNK + hf * half + t, rb, stride=CHUNK), :] = unit[t].astype(F32)


def _in_proj_kernel(n_cast, x_ref, gain_ref, w_ref, *refs):
    cast_src, (xg_ref, v_ref), cast_dst = refs[:n_cast], refs[n_cast:n_cast + 2], refs[n_cast + 2:-2]
    park = refs[-2:]
    _cast_slabs(cast_src, cast_dst)
    s = pl.program_id(0)

    @pl.when(s == 0)
    def _():
        park[1][...] = jnp.zeros_like(park[1])

    def step(p_prev, p_next):
        _relayout_to_groups(p_prev, xg_ref, x_ref.shape[0] // CHUNK)
        x = x_ref[...]
        r = lax.rsqrt(jnp.mean(x * x, axis=-1, keepdims=True) + EPS)
        p = jnp.dot((x * gain_ref[...]).astype(BF16), w_ref[...], preferred_element_type=F32) * r
        v_ref[...] = p[:, SSM_WIDTH:].astype(v_ref.dtype)
        for b in range(SSM_WIDTH // LANES):
            p_next[b] = p[:, b * LANES:(b + 1) * LANES]

    pl.when(s % 2 == 0)(lambda: step(park[1], park[0]))
    pl.when(s % 2 == 1)(lambda: step(park[0], park[1]))


def _in_proj(x, gain, w_in, cast, tm):
    L = x.shape[0]
    n = L // tm
    c_in, c_out, c_shapes = _cast_specs(cast, n)
    cur = lambda s: (jnp.minimum(s, n - 1), 0)
    return pl.pallas_call(
        functools.partial(_in_proj_kernel, len(cast)),
        grid=(n + 1,),
        in_specs=[pl.BlockSpec((tm, D_MODEL), cur),
                  pl.BlockSpec((1, D_MODEL), lambda s: (0, 0)),
                  pl.BlockSpec((D_MODEL, D_MODEL), lambda s: (0, 0))] + c_in,
        out_specs=[pl.BlockSpec((SSM_GROUPS, tm // CHUNK, CHUNK_LANES), lambda s: (0, jnp.maximum(s - 1, 0), 0)),
                   pl.BlockSpec((tm, POOL_WIDTH), cur)] + c_out,
        out_shape=[jax.ShapeDtypeStruct((SSM_GROUPS, L // CHUNK, CHUNK_LANES), BF16),
                   jax.ShapeDtypeStruct((L, POOL_WIDTH), BF16)] + c_shapes,
        scratch_shapes=[pltpu.VMEM((SSM_WIDTH // LANES, tm, LANES), F32)] * 2,
        compiler_params=pltpu.CompilerParams(dimension_semantics=("arbitrary",),
                                             vmem_limit_bytes=VMEM_LIMIT_BYTES),
        name="in_proj",
    )(x, gain, w_in, *cast)


def _cmul(ar, ai, br, bi):
    return ar * br - ai * bi, ar * bi + ai * br


def _ssm_kernel(xg_ref, m_ref, bw_ref, cw_ref, lre_ref, lim_ref, ls_ref, yg_ref,
                fr_ref, fi_ref, cr_ref, ci_ref):
    n_chunks = xg_ref.shape[1]
    sb = SCAN_BLOCK
    n_sb = n_chunks // sb

    blk3 = (n_sb, sb, 2 * SSM_STATE)
    row_in = lax.broadcasted_iota(jnp.int32, blk3, 1)
    brow = lax.broadcasted_iota(jnp.int32, (n_sb, 2 * SSM_STATE), 0)
    for q in range(GROUPS_PER_BLOCK // 2):
        x0, x1 = xg_ref[2 * q], xg_ref[2 * q + 1]
        xp = jnp.concatenate([x0, x1], axis=1)
        e = lax.dot_general(xp, bw_ref[q], (((1,), (1,)), ((), ())), preferred_element_type=F32)
        step = jnp.exp(ls_ref[q])
        zr, zi = lre_ref[q] * step * CHUNK, lim_ref[q] * step * CHUNK
        mag = jnp.exp(zr)
        a1r, a1i = mag * jnp.cos(zi), mag * jnp.sin(zi)

        fr = e[:, :2 * SSM_STATE].reshape(blk3)
        fi = e[:, 2 * SSM_STATE:].reshape(blk3)
        ar, ai = a1r, a1i
        d = 1
        while d < sb:
            tr = jnp.where(row_in >= d, pltpu.roll(fr, d, axis=1), 0.0)
            ti = jnp.where(row_in >= d, pltpu.roll(fi, d, axis=1), 0.0)
            fr, fi = fr + ar * tr - ai * ti, fi + ar * ti + ai * tr
            ar, ai = _cmul(ar, ai, ar, ai)
            d *= 2
        fr_ref[...] = fr.reshape(n_chunks, 2 * SSM_STATE)
        fi_ref[...] = fi.reshape(n_chunks, 2 * SSM_STATE)
        gr = fr_ref[pl.ds(sb - 1, n_sb, stride=sb), :]
        gi = fi_ref[pl.ds(sb - 1, n_sb, stride=sb), :]
        gr = jnp.where(brow >= 1, pltpu.roll(gr, 1, axis=0), 0.0)
        gi = jnp.where(brow >= 1, pltpu.roll(gi, 1, axis=0), 0.0)
        d = 1
        while d < n_sb:
            tr = jnp.where(brow >= d, pltpu.roll(gr, d, axis=0), 0.0)
            ti = jnp.where(brow >= d, pltpu.roll(gi, d, axis=0), 0.0)
            gr, gi = gr + ar * tr - ai * ti, gi + ar * ti + ai * tr
            ar, ai = _cmul(ar, ai, ar, ai)
            d *= 2
        pr, pi = jnp.ones_like(a1r), jnp.zeros_like(a1i)
        for r in range(sb):
            cr_ref[pl.ds(r, n_sb, stride=sb), :] = pr * gr - pi * gi
            ci_ref[pl.ds(r, n_sb, stride=sb), :] = pr * gi + pi * gr
            pr, pi = _cmul(pr, pi, a1r, a1i)
        sr = jnp.where(row_in >= 1, pltpu.roll(fr, 1, axis=1), 0.0).reshape(n_chunks, 2 * SSM_STATE)
        si = jnp.where(row_in >= 1, pltpu.roll(fi, 1, axis=1), 0.0).reshape(n_chunks, 2 * SSM_STATE)
        sr = sr + cr_ref[...]
        si = si + ci_ref[...]
        s = jnp.concatenate([sr, si], axis=1).astype(BF16)
        yc = jnp.dot(s, cw_ref[q], preferred_element_type=F32)
        y0 = jnp.dot(x0, m_ref[2 * q], preferred_element_type=F32) + yc[:, :CHUNK_LANES]
        y1 = jnp.dot(x1, m_ref[2 * q + 1], preferred_element_type=F32) + yc[:, CHUNK_LANES:]
        yg_ref[2 * q] = y0.astype(yg_ref.dtype)
        yg_ref[2 * q + 1] = y1.astype(yg_ref.dtype)


def _ssm(xg, m, bw_pair, cw_pair, lre_row, lim_row, ls_row):
    n_chunks = xg.shape[1]
    n_blocks = SSM_WIDTH // LANES
    ppb = GROUPS_PER_BLOCK // 2
    return pl.pallas_call(
        _ssm_kernel,
        grid=(n_blocks,),
        in_specs=[pl.BlockSpec((GROUPS_PER_BLOCK, n_chunks, CHUNK_LANES), lambda b: (b, 0, 0)),
                  pl.BlockSpec((GROUPS_PER_BLOCK, CHUNK_LANES, CHUNK_LANES), lambda b: (b, 0, 0)),
                  pl.BlockSpec((ppb, 4 * SSM_STATE, 2 * CHUNK_LANES), lambda b: (b, 0, 0)),
                  pl.BlockSpec((ppb, 4 * SSM_STATE, 2 * CHUNK_LANES), lambda b: (b, 0, 0)),
                  pl.BlockSpec((ppb, 1, 2 * SSM_STATE), lambda b: (b, 0, 0)),
                  pl.BlockSpec((ppb, 1, 2 * SSM_STATE), lambda b: (b, 0, 0)),
                  pl.BlockSpec((ppb, 1, 2 * SSM_STATE), lambda b: (b, 0, 0))],
        out_specs=pl.BlockSpec((GROUPS_PER_BLOCK, n_chunks, CHUNK_LANES), lambda b: (b, 0, 0)),
        out_shape=jax.ShapeDtypeStruct(xg.shape, BF16),
        scratch_shapes=[pltpu.VMEM((n_chunks, 2 * SSM_STATE), F32)] * 4,
        compiler_params=pltpu.CompilerParams(dimension_semantics=("parallel",),
                                             vmem_limit_bytes=VMEM_LIMIT_BYTES),
        name="ssm",
    )(xg, m, bw_pair, cw_pair, lre_row, lim_row, ls_row)


def _out_proj_kernel(n_cast, yg_ref, v_ref, vh_ref, x_ref, wglu_ref, bglu_ref, wpool_ref, bpool_ref,
                     pscale_ref, wtop_ref, wbot_ref, *refs):
    cast_src, o_ref, cast_dst = refs[:n_cast], refs[n_cast], refs[n_cast + 1:-4]
    mix_park, y_ref, vbuf_ref = refs[-4:-2], refs[-2], refs[-1]
    _cast_slabs(cast_src, cast_dst)
    s = pl.program_id(0)
    tm = x_ref.shape[0]

    @pl.when(s == 0)
    def _():
        mix_park[1][...] = jnp.zeros_like(mix_park[1])

    def step(mix_prev, mix_next):
        n_tile = 2 * LANES

        def project(n):
            c = slice(n * n_tile, (n + 1) * n_tile)
            o_ref[:, c] = (x_ref[:, c]
                           + jnp.dot(mix_prev[:, :SSM_WIDTH], wtop_ref[:, c], preferred_element_type=F32)
                           + jnp.dot(mix_prev[:, SSM_WIDTH:], wbot_ref[:, c], preferred_element_type=F32))

        vbuf_ref[0:HALO, :] = jnp.where(s > 0, vh_ref[...].astype(F32), 0.0)
        vbuf_ref[HALO:, :] = v_ref[...].astype(F32)
        t1 = (lax.broadcasted_iota(jnp.int32, (tm, 1), 0) + (s * tm + 1)).astype(F32)

        def pool_window(k):
            w = POOL_WINDOWS[k]
            vk = vbuf_ref[:, k * POOL_GROUP:(k + 1) * POOL_GROUP]
            acc, span = vk, 1
            while span < w:
                acc = acc + pltpu.roll(acc, span, axis=0)
                span *= 2
            mean = acc[HALO:, :] / jnp.minimum(t1, float(w))
            return (mean - vk[HALO:, :]).astype(BF16)

        def gelu(blocks):
            y = jnp.concatenate([y_ref[b] for b in blocks], axis=1)
            return 0.5 * y * (1.0 + lax.erf(y * (1.0 / math.sqrt(2.0))))

        half_blocks = SSM_WIDTH // LANES // 2
        pk = [pool_window(0), pool_window(1)]
        project(0)
        pk.append(pool_window(2))
        project(1)
        pk.append(pool_window(3))
        project(2)
        pooled = [jnp.dot(pk[k], wpool_ref[k], preferred_element_type=F32) for k in range(len(POOL_WINDOWS))]
        yp = (jnp.concatenate(pooled, axis=1) + bpool_ref[...]) * pscale_ref[...]
        mix_next[:, SSM_WIDTH:] = yp.astype(BF16)
        _relayout_from_groups(yg_ref, y_ref, tm // CHUNK, blocks=range(0, half_blocks))
        project(3)
        _relayout_from_groups(yg_ref, y_ref, tm // CHUNK, blocks=range(half_blocks, 2 * half_blocks))
        project(4)
        g_a = gelu(range(0, half_blocks))
        project(5)
        g_b = gelu(range(half_blocks, 2 * half_blocks))
        project(6)
        g = jnp.concatenate([g_a, g_b], axis=1)
        z = jnp.dot(g.astype(BF16), wglu_ref[...], preferred_element_type=F32) + bglu_ref[...]
        mix_next[:, :SSM_WIDTH] = (g * (1.0 / (1.0 + jnp.exp(-z)))).astype(BF16)
        project(7)

    pl.when(s % 2 == 0)(lambda: step(mix_park[1], mix_park[0]))
    pl.when(s % 2 == 1)(lambda: step(mix_park[0], mix_park[1]))


def _out_proj(yg, v, x, w_glu, b_glu, w_pool, b_pool, pool_scale, w_out, cast, tm):
    L = x.shape[0]
    n = L // tm
    halo_blocks = tm // HALO
    const = lambda *shape: pl.BlockSpec(shape, lambda s: (0,) * len(shape))
    c_in, c_out, c_shapes = _cast_specs(cast, n)
    cur = lambda s: jnp.minimum(s, n - 1)
    prev = lambda s: (jnp.maximum(s - 1, 0), 0)
    return pl.pallas_call(
        functools.partial(_out_proj_kernel, len(cast)),
        grid=(n + 1,),
        in_specs=[pl.BlockSpec((SSM_GROUPS, tm // CHUNK, CHUNK_LANES), lambda s: (0, cur(s), 0)),
                  pl.BlockSpec((tm, POOL_WIDTH), lambda s: (cur(s), 0)),
                  pl.BlockSpec((HALO, POOL_WIDTH), lambda s: (jnp.maximum(cur(s) * halo_blocks - 1, 0), 0)),
                  pl.BlockSpec((tm, D_MODEL), prev),
                  const(SSM_WIDTH, SSM_WIDTH), const(1, SSM_WIDTH),
                  const(len(POOL_WINDOWS), POOL_GROUP, POOL_GROUP), const(1, POOL_WIDTH),
                  const(1, POOL_WIDTH),
                  pl.BlockSpec((SSM_WIDTH, D_MODEL), lambda s: (0, 0)),
                  pl.BlockSpec((POOL_WIDTH, D_MODEL), lambda s: (1, 0))] + c_in,
        out_specs=[pl.BlockSpec((tm, D_MODEL), prev)] + c_out,
        out_shape=[jax.ShapeDtypeStruct((L, D_MODEL), F32)] + c_shapes,
        scratch_shapes=[pltpu.VMEM((tm, SSM_WIDTH + POOL_WIDTH), BF16)] * 2
                       + [pltpu.VMEM((SSM_WIDTH // LANES, tm, LANES), F32),
                          pltpu.VMEM((HALO + tm, POOL_WIDTH), F32)],
        compiler_params=pltpu.CompilerParams(dimension_semantics=("arbitrary",),
                                             vmem_limit_bytes=VMEM_LIMIT_BYTES),
        name="out_proj",
    )(yg, v, v, x, w_glu, b_glu, w_pool, b_pool, pool_scale, w_out, w_out, *cast)


def _ffn_kernel(x_ref, gain_ref, wg_ref, wu_ref, wd_ref, gfin_ref, o_ref, h_ref, r_ref):
    j = pl.program_id(1)
    last = pl.num_programs(1) - 1
    lane_tiles = wg_ref.shape[1] // LANES

    def slab(first, final):
        if first:
            x = x_ref[...]
            h = (x * gain_ref[...]).astype(BF16)
            h_ref[...] = h
            r = lax.rsqrt(jnp.mean(x * x, axis=-1, keepdims=True) + EPS)
            r = jnp.broadcast_to(r, r_ref.shape)
            r_ref[...] = r
        else:
            h, r = h_ref[...], r_ref[...]
        rt = jnp.concatenate([r] * lane_tiles, axis=1)
        a = jnp.dot(h, wg_ref[...], preferred_element_type=F32) * rt
        b = jnp.dot(h, wu_ref[...], preferred_element_type=F32) * rt
        ff = (a * (1.0 / (1.0 + jnp.exp(-a))) * b).astype(BF16)
        acc = (x if first else o_ref[...]) + jnp.dot(ff, wd_ref[...], preferred_element_type=F32)
        o_ref[...] = _rms_norm(acc, gfin_ref[...]) if final else acc

    pl.when(j == 0)(lambda: slab(True, False))
    pl.when(jnp.logical_and(j > 0, j < last))(lambda: slab(False, False))
    pl.when(j == last)(lambda: slab(False, True))


def _ffn(x, gain, w_gate, w_up, w_down, gain_final, tm, tf):
    L = x.shape[0]
    d_ff = w_gate.shape[1]
    assert d_ff // tf >= 2
    return pl.pallas_call(
        _ffn_kernel,
        grid=(L // tm, d_ff // tf),
        in_specs=[pl.BlockSpec((tm, D_MODEL), lambda i, j: (i, 0)),
                  pl.BlockSpec((1, D_MODEL), lambda i, j: (0, 0)),
                  pl.BlockSpec((D_MODEL, tf), lambda i, j: (0, j)),
                  pl.BlockSpec((D_MODEL, tf), lambda i, j: (0, j)),
                  pl.BlockSpec((tf, D_MODEL), lambda i, j: (j, 0)),
                  pl.BlockSpec((1, D_MODEL), lambda i, j: (0, 0))],
        out_specs=pl.BlockSpec((tm, D_MODEL), lambda i, j: (i, 0)),
        out_shape=jax.ShapeDtypeStruct((L, D_MODEL), F32),
        scratch_shapes=[pltpu.VMEM((tm, D_MODEL), BF16), pltpu.VMEM((tm, LANES), F32)],
        compiler_params=pltpu.CompilerParams(dimension_semantics=("parallel", "arbitrary"),
                                             vmem_limit_bytes=VMEM_LIMIT_BYTES),
        name="ffn",
    )(x, gain, w_gate, w_up, w_down, gain_final)


def kernel(x, norm_mix, w_in, lambda_re, lambda_im, log_step, b_re, b_im, c_re, c_im, d_skip,
           w_glu, b_glu, w_pool, b_pool, pool_scale, w_out, norm_ffn, w_gate, w_up, w_down,
           norm_final):
    bsz, L, D = x.shape
    assert bsz == 1 and D == D_MODEL and w_in.shape[0] == 1 and L % 1024 == 0
    G, P = SSM_GROUPS, SSM_STATE
    xs = x.reshape(L, D).astype(F32)

    n_pool = len(POOL_WINDOWS)
    m, bw_pair, cw_pair, w_in_b, w_out_b, w_glu_b, w_pool_b = _ssm_prep(
        lambda_re[0].astype(F32), lambda_im[0].astype(F32), log_step[0].astype(F32),
        b_re[0].astype(F32), b_im[0].astype(F32), c_re[0].astype(F32), c_im[0].astype(F32),
        d_skip[0].astype(F32),
        cast=[w_in[0], w_out[0], w_glu[0], w_pool[0].reshape(n_pool * POOL_GROUP, POOL_GROUP)])
    lre_row = lambda_re[0].astype(F32).reshape(G // 2, 1, 2 * P)
    lim_row = lambda_im[0].astype(F32).reshape(G // 2, 1, 2 * P)
    ls_row = jnp.repeat(log_step[0].astype(F32), P).reshape(G // 2, 1, 2 * P)

    xg, v, w_gate_b = _in_proj(xs, norm_mix[0].astype(F32).reshape(1, D), w_in_b, cast=[w_gate[0]], tm=512)

    yg = _ssm(xg, m, bw_pair, cw_pair, lre_row, lim_row, ls_row)

    x1, w_up_b, w_down_b = _out_proj(
        yg, v, xs, w_glu_b, b_glu[0].astype(F32).reshape(1, SSM_WIDTH),
        w_pool_b.reshape(n_pool, POOL_GROUP, POOL_GROUP), b_pool[0].astype(F32).reshape(1, POOL_WIDTH),
        pool_scale[0].astype(F32).reshape(1, POOL_WIDTH), w_out_b, cast=[w_up[0], w_down[0]], tm=512)

    out = _ffn(x1, norm_ffn[0].astype(F32).reshape(1, D), w_gate_b, w_up_b, w_down_b,
               norm_final.astype(F32).reshape(1, D), tm=1024, tf=512)
    return out.reshape(bsz, L, D).astype(x.dtype)
```

```python
import functools
import math

import jax
import jax.numpy as jnp
from jax import lax
from jax.experimental import pallas as pl
from jax.experimental.pallas import tpu as pltpu

F32 = jnp.float32
BF16 = jnp.bfloat16

D_MODEL = 2048
SSM_WIDTH = 1024
POOL_WIDTH = 1024
SSM_GROUP = 16
SSM_GROUPS = 64
SSM_STATE = 64
POOL_WINDOWS = (2, 4, 8, 16)
POOL_GROUP = 256
EPS = 1e-6

CHUNK = 16
CHUNK_LANES = CHUNK * SSM_GROUP
HALO = 16
LANES = 128
GROUPS_PER_BLOCK = LANES // SSM_GROUP
RELAYOUT_ROWS = 16
RELAYOUT_UNITS = 4
SCAN_BLOCK = 8
PARK_PITCH = 20

VMEM_LIMIT_BYTES = 60 * 1024 * 1024

HIGHEST = lax.Precision.HIGHEST


def _rms_norm(x, gain):
    ms = jnp.mean(x * x, axis=-1, keepdims=True)
    return x * lax.rsqrt(ms + EPS) * gain


def _cast_specs(weights, n_steps):
    in_specs, out_specs, out_shapes = [], [], []
    for w in weights:
        rows, cols = w.shape
        slab = pl.BlockSpec((rows // n_steps, cols), lambda i: (jnp.minimum(i, n_steps - 1), 0))
        in_specs.append(slab)
        out_specs.append(slab)
        out_shapes.append(jax.ShapeDtypeStruct((rows, cols), BF16))
    return in_specs, out_specs, out_shapes


def _cast_slabs(src_refs, dst_refs):
    for src, dst in zip(src_refs, dst_refs):
        dst[...] = src[...].astype(dst.dtype)


def _ssm_prep_kernel(n_cast, lre_ref, lim_ref, ls_ref, b_re_ref, b_im_ref, bt_re_ref, bt_im_ref,
                     ct_re_ref, ct_im_ref, d_ref, *refs):
    cast_src, (m_ref, bw_ref, cw_ref), cast_dst = refs[:n_cast], refs[n_cast:n_cast + 3], refs[n_cast + 3:]
    _cast_slabs(cast_src, cast_dst)
    P, H, T, GB = SSM_STATE, SSM_GROUP, CHUNK, GROUPS_PER_BLOCK
    lre, lim = lre_ref[...], lim_ref[...]
    step = jnp.exp(ls_ref[...])
    mag = jnp.exp(lre * step)
    lbr, lbi = mag * jnp.cos(lim * step), mag * jnp.sin(lim * step)
    den = lre * lre + lim * lim
    cfr = ((lbr - 1.0) * lre + lbi * lim) / den
    cfi = (lbi * lre - (lbr - 1.0) * lim) / den
    eye = (lax.broadcasted_iota(jnp.int32, (P, P), 0)
           == lax.broadcasted_iota(jnp.int32, (P, P), 1)).astype(F32)
    rows = jnp.concatenate([lbr, lbi, cfr, cfi], axis=0)
    cols = lax.dot_general(eye, rows, (((1,), (1,)), ((), ())), precision=HIGHEST,
                           preferred_element_type=F32)

    lane = lax.broadcasted_iota(jnp.int32, (1, CHUNK_LANES), 1)
    k_idx = lane // H
    sub = lax.broadcasted_iota(jnp.int32, (H, CHUNK_LANES), 0)
    lane_h = lax.broadcasted_iota(jnp.int32, (H, CHUNK_LANES), 1)
    rep = (lane_h % H == sub).astype(F32)

    def tile(a):
        return jnp.dot(a, rep, precision=HIGHEST, preferred_element_type=F32)

    def cmul(ar, ai, br, bi):
        return ar * br - ai * bi, ar * bi + ai * br

    bw_ref[...] = jnp.zeros_like(bw_ref)
    cw_ref[...] = jnp.zeros_like(cw_ref)
    for g in range(GB):
        col = lambda quantity: cols[:, quantity * GB + g:quantity * GB + g + 1]
        l1r, l1i, cr, ci = col(0), col(1), col(2), col(3)
        sq = [(l1r, l1i)]
        for _ in range(3):
            sq.append(cmul(*sq[-1], *sq[-1]))
        pkr = pki = prr = pri = None
        for bit, (fr, fi) in enumerate(sq):
            on = ((k_idx >> bit) & 1) == 1
            ur, ui = jnp.where(on, fr, 1.0), jnp.where(on, fi, 0.0)
            dr, di = jnp.where(on, 1.0, fr), jnp.where(on, 0.0, fi)
            if bit == 0:
                pkr, pki, prr, pri = ur, ui, dr, di
            else:
                pkr, pki = cmul(pkr, pki, ur, ui)
                prr, pri = cmul(prr, pri, dr, di)

        bb_re, bb_im = cmul(cr, ci, b_re_ref[g], b_im_ref[g])
        bbt_re, bbt_im = cmul(cfr[g:g + 1, :], cfi[g:g + 1, :], bt_re_ref[g], bt_im_ref[g])

        q, j = g // 2, g % 2
        r_re = slice(j * P, (j + 1) * P)
        r_im = slice((2 + j) * P, (3 + j) * P)
        c_g = slice(j * CHUNK_LANES, (j + 1) * CHUNK_LANES)
        w_re, w_im = cmul(prr, pri, tile(bb_re), tile(bb_im))
        bw_ref[q, r_re, c_g] = w_re.astype(bw_ref.dtype)
        bw_ref[q, r_im, c_g] = w_im.astype(bw_ref.dtype)
        g_re, g_im = cmul(pkr, pki, tile(ct_re_ref[g]), tile(ct_im_ref[g]))
        g1_re, g1_im = cmul(l1r, l1i, g_re, g_im)
        cw_ref[q, r_re, c_g] = g1_re.astype(cw_ref.dtype)
        cw_ref[q, r_im, c_g] = (-g1_im).astype(cw_ref.dtype)
        kt = (jnp.dot(bbt_re, g_re, precision=HIGHEST, preferred_element_type=F32)
              - jnp.dot(bbt_im, g_im, precision=HIGHEST, preferred_element_type=F32))
        kt = kt + jnp.where(lane_h == sub, d_ref[g], 0.0)
        for tau in range(T):
            blk = kt if tau == 0 else jnp.where(lane_h >= tau * H, pltpu.roll(kt, tau * H, axis=1), 0.0)
            m_ref[g, tau * H:(tau + 1) * H, :] = blk.astype(m_ref.dtype)


def _ssm_prep(lambda_re, lambda_im, log_step, b_re, b_im, c_re, c_im, d_skip, cast):
    G, P, H, GB = SSM_GROUPS, SSM_STATE, SSM_GROUP, GROUPS_PER_BLOCK
    n_steps = G // GB
    blk = lambda *shape: pl.BlockSpec(shape, lambda i: (i,) + (0,) * (len(shape) - 1))
    c_in, c_out, c_shapes = _cast_specs(cast, n_steps)
    return pl.pallas_call(
        functools.partial(_ssm_prep_kernel, len(cast)),
        grid=(n_steps,),
        in_specs=[blk(GB, P), blk(GB, P), blk(GB, 1),
                  blk(GB, P, H), blk(GB, P, H), blk(GB, H, P), blk(GB, H, P),
                  blk(GB, P, H), blk(GB, P, H), blk(GB, H, 1)] + c_in,
        out_specs=[blk(GB, CHUNK_LANES, CHUNK_LANES), blk(GB // 2, 4 * P, 2 * CHUNK_LANES),
                   blk(GB // 2, 4 * P, 2 * CHUNK_LANES)] + c_out,
        out_shape=[jax.ShapeDtypeStruct((G, CHUNK_LANES, CHUNK_LANES), BF16),
                   jax.ShapeDtypeStruct((G // 2, 4 * P, 2 * CHUNK_LANES), BF16),
                   jax.ShapeDtypeStruct((G // 2, 4 * P, 2 * CHUNK_LANES), BF16)] + c_shapes,
        compiler_params=pltpu.CompilerParams(dimension_semantics=("parallel",),
                                             vmem_limit_bytes=VMEM_LIMIT_BYTES),
        name="ssm_prep",
    )(lambda_re, lambda_im, log_step.reshape(G, 1),
      b_re, b_im, jnp.swapaxes(b_re, 1, 2), jnp.swapaxes(b_im, 1, 2),
      jnp.swapaxes(c_re, 1, 2), jnp.swapaxes(c_im, 1, 2), d_skip.reshape(G, H, 1), *cast)


def _granule_transpose(units):
    lane = lax.broadcasted_iota(jnp.int32, units[0][0].shape, 1)
    units = [list(a) for a in units]
    for s in (4, 2, 1):
        upper = ((lane // SSM_GROUP) & s) != 0
        for a in units:
            for i in range(GROUPS_PER_BLOCK):
                if i & s:
                    continue
                lo, hi = a[i], a[i + s]
                a[i] = jnp.where(upper, pltpu.roll(hi, s * SSM_GROUP, axis=1), lo)
                a[i + s] = jnp.where(upper, hi, pltpu.roll(lo, LANES - s * SSM_GROUP, axis=1))
    return units


def _relayout_units(n_chunks):
    return [(b, cb, hf) for b in range(SSM_WIDTH // LANES)
            for cb in range(n_chunks // RELAYOUT_ROWS) for hf in range(2)]


def _relayout_to_groups(src_ref, dst_ref, n_chunks):
    half = CHUNK // 2
    rb = RELAYOUT_ROWS
    where = _relayout_units(n_chunks)
    batch = 2 * RELAYOUT_UNITS
    for i in range(0, len(where), batch):
        units = [[src_ref[b, pl.ds(cb * rb * PARK_PITCH + hf * half + t, rb, stride=PARK_PITCH), :].astype(BF16)
                  for t in range(half)] for (b, cb, hf) in where[i:i + batch]]
        units = _granule_transpose(units)
        for (b, cb, hf), unit in zip(where[i:i + batch], units):
            for g in range(GROUPS_PER_BLOCK):
                dst_ref[b * GROUPS_PER_BLOCK + g, cb * rb:(cb + 1) * rb, hf * LANES:(hf + 1) * LANES] = unit[g]


def _relayout_from_groups(src_ref, dst_ref, n_chunks, blocks=range(SSM_WIDTH // LANES)):
    half = CHUNK // 2
    rb = RELAYOUT_ROWS
    where = [u for u in _relayout_units(n_chunks) if u[0] in blocks]
    batch = 2 * RELAYOUT_UNITS
    for i in range(0, len(where), batch):
        units = [[src_ref[b * GROUPS_PER_BLOCK + g, cb * rb:(cb + 1) * rb, hf * LANES:(hf + 1) * LANES]
                  for g in range(GROUPS_PER_BLOCK)] for (b, cb, hf) in where[i:i + batch]]
        units = _granule_transpose(units)
        for (b, cb, hf), unit in zip(where[i:i + batch], units):
            for t in range(half):
                dst_ref[b, pl.ds(cb * rb * PARK_PITCH + hf * half + t, rb, stride=PARK_PITCH), :] = unit[t].astype(F32)


def _in_proj_kernel(n_cast, x_ref, gain_ref, w_ref, *refs):
    cast_src, (xg_ref, v_ref), cast_dst = refs[:n_cast], refs[n_cast:n_cast + 2], refs[n_cast + 2:-2]
    park = refs[-2:]
    _cast_slabs(cast_src, cast_dst)
    s = pl.program_id(0)

    @pl.when(s == 0)
    def _():
        park[1][...] = jnp.zeros_like(park[1])

    def step(p_prev, p_next):
        _relayout_to_groups(p_prev, xg_ref, x_ref.shape[0] // CHUNK)
        h = _rms_norm(x_ref[...], gain_ref[...]).astype(BF16)
        p = jnp.dot(h, w_ref[...], preferred_element_type=F32)
        v_ref[...] = p[:, SSM_WIDTH:].astype(v_ref.dtype)
        for b in range(SSM_WIDTH // LANES):
            for c in range(x_ref.shape[0] // CHUNK):
                p_next[b, c * PARK_PITCH:c * PARK_PITCH + CHUNK, :] = p[c * CHUNK:(c + 1) * CHUNK, b * LANES:(b + 1) * LANES]

    pl.when(s % 2 == 0)(lambda: step(park[1], park[0]))
    pl.when(s % 2 == 1)(lambda: step(park[0], park[1]))


def _in_proj(x, gain, w_in, cast, tm):
    L = x.shape[0]
    n = L // tm
    c_in, c_out, c_shapes = _cast_specs(cast, n)
    cur = lambda s: (jnp.minimum(s, n - 1), 0)
    return pl.pallas_call(
        functools.partial(_in_proj_kernel, len(cast)),
        grid=(n + 1,),
        in_specs=[pl.BlockSpec((tm, D_MODEL), cur),
                  pl.BlockSpec((1, D_MODEL), lambda s: (0, 0)),
                  pl.BlockSpec((D_MODEL, D_MODEL), lambda s: (0, 0))] + c_in,
        out_specs=[pl.BlockSpec((SSM_GROUPS, tm // CHUNK, CHUNK_LANES), lambda s: (0, jnp.maximum(s - 1, 0), 0)),
                   pl.BlockSpec((tm, POOL_WIDTH), cur)] + c_out,
        out_shape=[jax.ShapeDtypeStruct((SSM_GROUPS, L // CHUNK, CHUNK_LANES), BF16),
                   jax.ShapeDtypeStruct((L, POOL_WIDTH), BF16)] + c_shapes,
        scratch_shapes=[pltpu.VMEM((SSM_WIDTH // LANES, tm // CHUNK * PARK_PITCH, LANES), F32)] * 2,
        compiler_params=pltpu.CompilerParams(dimension_semantics=("arbitrary",),
                                             vmem_limit_bytes=VMEM_LIMIT_BYTES),
        name="in_proj",
    )(x, gain, w_in, *cast)


def _cmul(ar, ai, br, bi):
    return ar * br - ai * bi, ar * bi + ai * br


def _ssm_kernel(xg_ref, m_ref, bw_ref, cw_ref, lre_ref, lim_ref, ls_ref, yg_ref,
                fr_ref, fi_ref, cr_ref, ci_ref):
    n_chunks = xg_ref.shape[1]
    sb = SCAN_BLOCK
    n_sb = n_chunks // sb

    blk3 = (n_sb, sb, 2 * SSM_STATE)
    row_in = lax.broadcasted_iota(jnp.int32, blk3, 1)
    brow = lax.broadcasted_iota(jnp.int32, (n_sb, 2 * SSM_STATE), 0)
    for q in range(GROUPS_PER_BLOCK // 2):
        x0, x1 = xg_ref[2 * q], xg_ref[2 * q + 1]
        xp = jnp.concatenate([x0, x1], axis=1)
        e = lax.dot_general(xp, bw_ref[q], (((1,), (1,)), ((), ())), preferred_element_type=F32)
        step = jnp.exp(ls_ref[q])
        zr, zi = lre_ref[q] * step * CHUNK, lim_ref[q] * step * CHUNK
        mag = jnp.exp(zr)
        a1r, a1i = mag * jnp.cos(zi), mag * jnp.sin(zi)

        fr = e[:, :2 * SSM_STATE].reshape(blk3)
        fi = e[:, 2 * SSM_STATE:].reshape(blk3)
        ar, ai = a1r, a1i
        d = 1
        while d < sb:
            tr = jnp.where(row_in >= d, pltpu.roll(fr, d, axis=1), 0.0)
            ti = jnp.where(row_in >= d, pltpu.roll(fi, d, axis=1), 0.0)
            fr, fi = fr + ar * tr - ai * ti, fi + ar * ti + ai * tr
            ar, ai = _cmul(ar, ai, ar, ai)
            d *= 2
        fr_ref[...] = fr.reshape(n_chunks, 2 * SSM_STATE)
        fi_ref[...] = fi.reshape(n_chunks, 2 * SSM_STATE)
        gr = fr_ref[pl.ds(sb - 1, n_sb, stride=sb), :]
        gi = fi_ref[pl.ds(sb - 1, n_sb, stride=sb), :]
        gr = jnp.where(brow >= 1, pltpu.roll(gr, 1, axis=0), 0.0)
        gi = jnp.where(brow >= 1, pltpu.roll(gi, 1, axis=0), 0.0)
        d = 1
        while d < n_sb:
            tr = jnp.where(brow >= d, pltpu.roll(gr, d, axis=0), 0.0)
            ti = jnp.where(brow >= d, pltpu.roll(gi, d, axis=0), 0.0)
            gr, gi = gr + ar * tr - ai * ti, gi + ar * ti + ai * tr
            ar, ai = _cmul(ar, ai, ar, ai)
            d *= 2
        pr, pi = jnp.ones_like(a1r), jnp.zeros_like(a1i)
        for r in range(sb):
            cr_ref[pl.ds(r, n_sb, stride=sb), :] = pr * gr - pi * gi
            ci_ref[pl.ds(r, n_sb, stride=sb), :] = pr * gi + pi * gr
            pr, pi = _cmul(pr, pi, a1r, a1i)
        sr = jnp.where(row_in >= 1, pltpu.roll(fr, 1, axis=1), 0.0).reshape(n_chunks, 2 * SSM_STATE)
        si = jnp.where(row_in >= 1, pltpu.roll(fi, 1, axis=1), 0.0).reshape(n_chunks, 2 * SSM_STATE)
        sr = sr + cr_ref[...]
        si = si + ci_ref[...]
        s = jnp.concatenate([sr, si], axis=1).astype(BF16)
        yc = jnp.dot(s, cw_ref[q], preferred_element_type=F32)
        y0 = jnp.dot(x0, m_ref[2 * q], preferred_element_type=F32) + yc[:, :CHUNK_LANES]
        y1 = jnp.dot(x1, m_ref[2 * q + 1], preferred_element_type=F32) + yc[:, CHUNK_LANES:]
        yg_ref[2 * q] = y0.astype(yg_ref.dtype)
        yg_ref[2 * q + 1] = y1.astype(yg_ref.dtype)


def _ssm(xg, m, bw_pair, cw_pair, lre_row, lim_row, ls_row):
    n_chunks = xg.shape[1]
    n_blocks = SSM_WIDTH // LANES
    ppb = GROUPS_PER_BLOCK // 2
    return pl.pallas_call(
        _ssm_kernel,
        grid=(n_blocks,),
        in_specs=[pl.BlockSpec((GROUPS_PER_BLOCK, n_chunks, CHUNK_LANES), lambda b: (b, 0, 0)),
                  pl.BlockSpec((GROUPS_PER_BLOCK, CHUNK_LANES, CHUNK_LANES), lambda b: (b, 0, 0)),
                  pl.BlockSpec((ppb, 4 * SSM_STATE, 2 * CHUNK_LANES), lambda b: (b, 0, 0)),
                  pl.BlockSpec((ppb, 4 * SSM_STATE, 2 * CHUNK_LANES), lambda b: (b, 0, 0)),
                  pl.BlockSpec((ppb, 1, 2 * SSM_STATE), lambda b: (b, 0, 0)),
                  pl.BlockSpec((ppb, 1, 2 * SSM_STATE), lambda b: (b, 0, 0)),
                  pl.BlockSpec((ppb, 1, 2 * SSM_STATE), lambda b: (b, 0, 0))],
        out_specs=pl.BlockSpec((GROUPS_PER_BLOCK, n_chunks, CHUNK_LANES), lambda b: (b, 0, 0)),
        out_shape=jax.ShapeDtypeStruct(xg.shape, BF16),
        scratch_shapes=[pltpu.VMEM((n_chunks, 2 * SSM_STATE), F32)] * 4,
        compiler_params=pltpu.CompilerParams(dimension_semantics=("parallel",),
                                             vmem_limit_bytes=VMEM_LIMIT_BYTES),
        name="ssm",
    )(xg, m, bw_pair, cw_pair, lre_row, lim_row, ls_row)


def _out_proj_kernel(n_cast, yg_ref, v_ref, vh_ref, x_ref, wglu_ref, bglu_ref, wpool_ref, bpool_ref,
                     pscale_ref, wtop_ref, wbot_ref, *refs):
    cast_src, o_ref, cast_dst = refs[:n_cast], refs[n_cast], refs[n_cast + 1:-4]
    mix_park, y_ref, vbuf_ref = refs[-4:-2], refs[-2], refs[-1]
    _cast_slabs(cast_src, cast_dst)
    s = pl.program_id(0)
    tm = x_ref.shape[0]

    @pl.when(s == 0)
    def _():
        mix_park[1][...] = jnp.zeros_like(mix_park[1])

    def step(mix_prev, mix_next):
        n_tile = 2 * LANES

        def project(n):
            c = slice(n * n_tile, (n + 1) * n_tile)
            o_ref[:, c] = (x_ref[:, c]
                           + jnp.dot(mix_prev[:, :SSM_WIDTH], wtop_ref[:, c], preferred_element_type=F32)
                           + jnp.dot(mix_prev[:, SSM_WIDTH:], wbot_ref[:, c], preferred_element_type=F32))

        vbuf_ref[0:HALO, :] = jnp.where(s > 0, vh_ref[...].astype(F32), 0.0)
        vbuf_ref[HALO:, :] = v_ref[...].astype(F32)
        t1 = (lax.broadcasted_iota(jnp.int32, (tm, 1), 0) + (s * tm + 1)).astype(F32)

        def pool_window(k):
            w = POOL_WINDOWS[k]
            vk = vbuf_ref[:, k * POOL_GROUP:(k + 1) * POOL_GROUP]
            acc, span = vk, 1
            while span < w:
                acc = acc + pltpu.roll(acc, span, axis=0)
                span *= 2
            mean = acc[HALO:, :] / jnp.minimum(t1, float(w))
            return (mean - vk[HALO:, :]).astype(BF16)

        def gelu(blocks):
            y = jnp.concatenate(
                [jnp.concatenate([y_ref[b, c * PARK_PITCH:c * PARK_PITCH + CHUNK, :] for c in range(tm // CHUNK)], axis=0)
                 for b in blocks], axis=1)
            return 0.5 * y * (1.0 + lax.erf(y * (1.0 / math.sqrt(2.0))))

        half_blocks = SSM_WIDTH // LANES // 2
        pk = [pool_window(0), pool_window(1)]
        project(0)
        pk.append(pool_window(2))
        project(1)
        pk.append(pool_window(3))
        project(2)
        pooled = [jnp.dot(pk[k], wpool_ref[k], preferred_element_type=F32) for k in range(len(POOL_WINDOWS))]
        yp = (jnp.concatenate(pooled, axis=1) + bpool_ref[...]) * pscale_ref[...]
        mix_next[:, SSM_WIDTH:] = yp.astype(BF16)
        _relayout_from_groups(yg_ref, y_ref, tm // CHUNK, blocks=range(0, half_blocks))
        project(3)
        _relayout_from_groups(yg_ref, y_ref, tm // CHUNK, blocks=range(half_blocks, 2 * half_blocks))
        project(4)
        g_a = gelu(range(0, half_blocks))
        project(5)
        g_b = gelu(range(half_blocks, 2 * half_blocks))
        project(6)
        g = jnp.concatenate([g_a, g_b], axis=1)
        z = jnp.dot(g.astype(BF16), wglu_ref[...], preferred_element_type=F32) + bglu_ref[...]
        mix_next[:, :SSM_WIDTH] = (g * (1.0 / (1.0 + jnp.exp(-z)))).astype(BF16)
        project(7)

    pl.when(s % 2 == 0)(lambda: step(mix_park[1], mix_park[0]))
    pl.when(s % 2 == 1)(lambda: step(mix_park[0], mix_park[1]))


def _out_proj(yg, v, x, w_glu, b_glu, w_pool, b_pool, pool_scale, w_out, cast, tm):
    L = x.shape[0]
    n = L // tm
    halo_blocks = tm // HALO
    const = lambda *shape: pl.BlockSpec(shape, lambda s: (0,) * len(shape))
    c_in, c_out, c_shapes = _cast_specs(cast, n)
    cur = lambda s: jnp.minimum(s, n - 1)
    prev = lambda s: (jnp.maximum(s - 1, 0), 0)
    return pl.pallas_call(
        functools.partial(_out_proj_kernel, len(cast)),
        grid=(n + 1,),
        in_specs=[pl.BlockSpec((SSM_GROUPS, tm // CHUNK, CHUNK_LANES), lambda s: (0, cur(s), 0)),
                  pl.BlockSpec((tm, POOL_WIDTH), lambda s: (cur(s), 0)),
                  pl.BlockSpec((HALO, POOL_WIDTH), lambda s: (jnp.maximum(cur(s) * halo_blocks - 1, 0), 0)),
                  pl.BlockSpec((tm, D_MODEL), prev),
                  const(SSM_WIDTH, SSM_WIDTH), const(1, SSM_WIDTH),
                  const(len(POOL_WINDOWS), POOL_GROUP, POOL_GROUP), const(1, POOL_WIDTH),
                  const(1, POOL_WIDTH),
                  pl.BlockSpec((SSM_WIDTH, D_MODEL), lambda s: (0, 0)),
                  pl.BlockSpec((POOL_WIDTH, D_MODEL), lambda s: (1, 0))] + c_in,
        out_specs=[pl.BlockSpec((tm, D_MODEL), prev)] + c_out,
        out_shape=[jax.ShapeDtypeStruct((L, D_MODEL), F32)] + c_shapes,
        scratch_shapes=[pltpu.VMEM((tm, SSM_WIDTH + POOL_WIDTH), BF16)] * 2
                       + [pltpu.VMEM((SSM_WIDTH // LANES, tm // CHUNK * PARK_PITCH, LANES), F32),
                          pltpu.VMEM((HALO + tm, POOL_WIDTH), F32)],
        compiler_params=pltpu.CompilerParams(dimension_semantics=("arbitrary",),
                                             vmem_limit_bytes=VMEM_LIMIT_BYTES),
        name="out_proj",
    )(yg, v, v, x, w_glu, b_glu, w_pool, b_pool, pool_scale, w_out, w_out, *cast)


def _ffn_kernel(x_ref, gain_ref, wg_ref, wu_ref, wd_ref, gfin_ref, o_ref, h_ref, r_ref):
    j = pl.program_id(1)
    last = pl.num_programs(1) - 1
    lane_tiles = wg_ref.shape[1] // LANES

    def slab(first, final):
        if first:
            x = x_ref[...]
            h = (x * gain_ref[...]).astype(BF16)
            h_ref[...] = h
            r = lax.rsqrt(jnp.mean(x * x, axis=-1, keepdims=True) + EPS)
            r = jnp.broadcast_to(r, r_ref.shape)
            r_ref[...] = r
        else:
            h, r = h_ref[...], r_ref[...]
        rt = jnp.concatenate([r] * lane_tiles, axis=1)
        a = jnp.dot(h, wg_ref[...], preferred_element_type=F32) * rt
        b = jnp.dot(h, wu_ref[...], preferred_element_type=F32) * rt
        ff = (a * (1.0 / (1.0 + jnp.exp(-a))) * b).astype(BF16)
        acc = (x if first else o_ref[...]) + jnp.dot(ff, wd_ref[...], preferred_element_type=F32)
        o_ref[...] = _rms_norm(acc, gfin_ref[...]) if final else acc

    pl.when(j == 0)(lambda: slab(True, False))
    pl.when(jnp.logical_and(j > 0, j < last))(lambda: slab(False, False))
    pl.when(j == last)(lambda: slab(False, True))


def _ffn(x, gain, w_gate, w_up, w_down, gain_final, tm, tf):
    L = x.shape[0]
    d_ff = w_gate.shape[1]
    assert d_ff // tf >= 2
    return pl.pallas_call(
        _ffn_kernel,
        grid=(L // tm, d_ff // tf),
        in_specs=[pl.BlockSpec((tm, D_MODEL), lambda i, j: (i, 0)),
                  pl.BlockSpec((1, D_MODEL), lambda i, j: (0, 0)),
                  pl.BlockSpec((D_MODEL, tf), lambda i, j: (0, j)),
                  pl.BlockSpec((D_MODEL, tf), lambda i, j: (0, j)),
                  pl.BlockSpec((tf, D_MODEL), lambda i, j: (j, 0)),
                  pl.BlockSpec((1, D_MODEL), lambda i, j: (0, 0))],
        out_specs=pl.BlockSpec((tm, D_MODEL), lambda i, j: (i, 0)),
        out_shape=jax.ShapeDtypeStruct((L, D_MODEL), F32),
        scratch_shapes=[pltpu.VMEM((tm, D_MODEL), BF16), pltpu.VMEM((tm, LANES), F32)],
        compiler_params=pltpu.CompilerParams(dimension_semantics=("parallel", "arbitrary"),
                                             vmem_limit_bytes=VMEM_LIMIT_BYTES),
        name="ffn",
    )(x, gain, w_gate, w_up, w_down, gain_final)


def kernel(x, norm_mix, w_in, lambda_re, lambda_im, log_step, b_re, b_im, c_re, c_im, d_skip,
           w_glu, b_glu, w_pool, b_pool, pool_scale, w_out, norm_ffn, w_gate, w_up, w_down,
           norm_final):
    bsz, L, D = x.shape
    assert bsz == 1 and D == D_MODEL and w_in.shape[0] == 1 and L % 1024 == 0
    G, P = SSM_GROUPS, SSM_STATE
    xs = x.reshape(L, D).astype(F32)

    n_pool = len(POOL_WINDOWS)
    m, bw_pair, cw_pair, w_in_b, w_out_b, w_glu_b, w_pool_b = _ssm_prep(
        lambda_re[0].astype(F32), lambda_im[0].astype(F32), log_step[0].astype(F32),
        b_re[0].astype(F32), b_im[0].astype(F32), c_re[0].astype(F32), c_im[0].astype(F32),
        d_skip[0].astype(F32),
        cast=[w_in[0], w_out[0], w_glu[0], w_pool[0].reshape(n_pool * POOL_GROUP, POOL_GROUP)])
    lre_row = lambda_re[0].astype(F32).reshape(G // 2, 1, 2 * P)
    lim_row = lambda_im[0].astype(F32).reshape(G // 2, 1, 2 * P)
    ls_row = jnp.repeat(log_step[0].astype(F32), P).reshape(G // 2, 1, 2 * P)

    xg, v, w_gate_b = _in_proj(xs, norm_mix[0].astype(F32).reshape(1, D), w_in_b, cast=[w_gate[0]], tm=512)

    yg = _ssm(xg, m, bw_pair, cw_pair, lre_row, lim_row, ls_row)

    x1, w_up_b, w_down_b = _out_proj(
        yg, v, xs, w_glu_b, b_glu[0].astype(F32).reshape(1, SSM_WIDTH),
        w_pool_b.reshape(n_pool, POOL_GROUP, POOL_GROUP), b_pool[0].astype(F32).reshape(1, POOL_WIDTH),
        pool_scale[0].astype(F32).reshape(1, POOL_WIDTH), w_out_b, cast=[w_up[0], w_down[0]], tm=512)

    out = _ffn(x1, norm_ffn[0].astype(F32).reshape(1, D), w_gate_b, w_up_b, w_down_b,
               norm_final.astype(F32).reshape(1, D), tm=1024, tf=512)
    return out.reshape(bsz, L, D).astype(x.dtype)
```

```python
import functools
import math

import jax
import jax.numpy as jnp
from jax import lax
from jax.experimental import pallas as pl
from jax.experimental.pallas import tpu as pltpu

F32 = jnp.float32
BF16 = jnp.bfloat16

D_MODEL = 2048
SSM_WIDTH = 1024
POOL_WIDTH = 1024
SSM_GROUP = 16
SSM_GROUPS = 64
SSM_STATE = 64
POOL_WINDOWS = (2, 4, 8, 16)
POOL_GROUP = 256
EPS = 1e-6

CHUNK = 16
CHUNK_LANES = CHUNK * SSM_GROUP
HALO = 16
POOL_PAD = 8
LANES = 128
GROUPS_PER_BLOCK = LANES // SSM_GROUP
RELAYOUT_ROWS = 16
RELAYOUT_UNITS = 4
SCAN_BLOCK = 8
PARK_PITCH = 20

VMEM_LIMIT_BYTES = 60 * 1024 * 1024

HIGHEST = lax.Precision.HIGHEST


def _rms_norm(x, gain):
    ms = jnp.mean(x * x, axis=-1, keepdims=True)
    return x * lax.rsqrt(ms + EPS) * gain


def _cast_specs(weights, n_steps):
    in_specs, out_specs, out_shapes = [], [], []
    for w in weights:
        rows, cols = w.shape
        slab = pl.BlockSpec((rows // n_steps, cols), lambda i: (jnp.minimum(i, n_steps - 1), 0))
        in_specs.append(slab)
        out_specs.append(slab)
        out_shapes.append(jax.ShapeDtypeStruct((rows, cols), BF16))
    return in_specs, out_specs, out_shapes


def _cast_slabs(src_refs, dst_refs):
    for src, dst in zip(src_refs, dst_refs):
        dst[...] = src[...].astype(dst.dtype)


def _ssm_prep_kernel(n_cast, lre_ref, lim_ref, ls_ref, b_re_ref, b_im_ref, c_re_ref, c_im_ref, d_ref, *refs):
    cast_src, (m_ref, bw_ref, cw_ref), cast_dst = refs[:n_cast], refs[n_cast:n_cast + 3], refs[n_cast + 3:]
    _cast_slabs(cast_src, cast_dst)
    P, H, T, GB = SSM_STATE, SSM_GROUP, CHUNK, GROUPS_PER_BLOCK
    lre, lim = lre_ref[...], lim_ref[...]
    step = jnp.exp(ls_ref[...])
    mag = jnp.exp(lre * step)
    lbr, lbi = mag * jnp.cos(lim * step), mag * jnp.sin(lim * step)
    den = lre * lre + lim * lim
    cfr = ((lbr - 1.0) * lre + lbi * lim) / den
    cfi = (lbi * lre - (lbr - 1.0) * lim) / den
    eye = (lax.broadcasted_iota(jnp.int32, (P, P), 0)
           == lax.broadcasted_iota(jnp.int32, (P, P), 1)).astype(F32)
    rows = jnp.concatenate([lbr, lbi, cfr, cfi], axis=0)
    cols = lax.dot_general(eye, rows, (((1,), (1,)), ((), ())), precision=HIGHEST,
                           preferred_element_type=F32)

    lane = lax.broadcasted_iota(jnp.int32, (1, CHUNK_LANES), 1)
    k_idx = lane // H
    sub = lax.broadcasted_iota(jnp.int32, (H, CHUNK_LANES), 0)
    lane_h = lax.broadcasted_iota(jnp.int32, (H, CHUNK_LANES), 1)
    rep = (lane_h % H == sub).astype(F32)

    def tile(a):
        return jnp.dot(a, rep, precision=HIGHEST, preferred_element_type=F32)

    def tile_t(a):
        return lax.dot_general(a, rep, (((0,), (0,)), ((), ())), precision=HIGHEST,
                               preferred_element_type=F32)

    def dot_t(a, b):
        return lax.dot_general(a, b, (((0,), (0,)), ((), ())), precision=HIGHEST,
                               preferred_element_type=F32)

    def cmul(ar, ai, br, bi):
        return ar * br - ai * bi, ar * bi + ai * br

    bw_ref[...] = jnp.zeros_like(bw_ref)
    cw_ref[...] = jnp.zeros_like(cw_ref)
    for g in range(GB):
        col = lambda quantity: cols[:, quantity * GB + g:quantity * GB + g + 1]
        l1r, l1i, cr, ci = col(0), col(1), col(2), col(3)
        sq = [(l1r, l1i)]
        for _ in range(3):
            sq.append(cmul(*sq[-1], *sq[-1]))
        pkr = pki = prr = pri = None
        for bit, (fr, fi) in enumerate(sq):
            on = ((k_idx >> bit) & 1) == 1
            ur, ui = jnp.where(on, fr, 1.0), jnp.where(on, fi, 0.0)
            dr, di = jnp.where(on, 1.0, fr), jnp.where(on, 0.0, fi)
            if bit == 0:
                pkr, pki, prr, pri = ur, ui, dr, di
            else:
                pkr, pki = cmul(pkr, pki, ur, ui)
                prr, pri = cmul(prr, pri, dr, di)

        bb_re, bb_im = cmul(cr, ci, b_re_ref[g], b_im_ref[g])

        q, j = g // 2, g % 2
        r_re = slice(j * P, (j + 1) * P)
        r_im = slice((2 + j) * P, (3 + j) * P)
        c_g = slice(j * CHUNK_LANES, (j + 1) * CHUNK_LANES)
        w_re, w_im = cmul(prr, pri, tile(bb_re), tile(bb_im))
        bw_ref[q, r_re, c_g] = w_re.astype(bw_ref.dtype)
        bw_ref[q, r_im, c_g] = w_im.astype(bw_ref.dtype)
        g_re, g_im = cmul(pkr, pki, tile_t(c_re_ref[g]), tile_t(c_im_ref[g]))
        g1_re, g1_im = cmul(l1r, l1i, g_re, g_im)
        cw_ref[q, r_re, c_g] = g1_re.astype(cw_ref.dtype)
        cw_ref[q, r_im, c_g] = (-g1_im).astype(cw_ref.dtype)
        kt = dot_t(bb_re, g_re) - dot_t(bb_im, g_im)
        kt = kt + jnp.where(lane_h == sub, d_ref[g], 0.0)
        for tau in range(T):
            blk = kt if tau == 0 else jnp.where(lane_h >= tau * H, pltpu.roll(kt, tau * H, axis=1), 0.0)
            m_ref[g, tau * H:(tau + 1) * H, :] = blk.astype(m_ref.dtype)


def _ssm_prep(lambda_re, lambda_im, log_step, b_re, b_im, c_re, c_im, d_skip, cast):
    G, P, H, GB = SSM_GROUPS, SSM_STATE, SSM_GROUP, GROUPS_PER_BLOCK
    n_steps = G // GB
    blk = lambda *shape: pl.BlockSpec(shape, lambda i: (i,) + (0,) * (len(shape) - 1))
    c_in, c_out, c_shapes = _cast_specs(cast, n_steps)
    return pl.pallas_call(
        functools.partial(_ssm_prep_kernel, len(cast)),
        grid=(n_steps,),
        in_specs=[blk(GB, P), blk(GB, P), blk(GB, 1),
                  blk(GB, P, H), blk(GB, P, H), blk(GB, H, P), blk(GB, H, P), blk(GB, H, 1)] + c_in,
        out_specs=[blk(GB, CHUNK_LANES, CHUNK_LANES), blk(GB // 2, 4 * P, 2 * CHUNK_LANES),
                   blk(GB // 2, 4 * P, 2 * CHUNK_LANES)] + c_out,
        out_shape=[jax.ShapeDtypeStruct((G, CHUNK_LANES, CHUNK_LANES), BF16),
                   jax.ShapeDtypeStruct((G // 2, 4 * P, 2 * CHUNK_LANES), BF16),
                   jax.ShapeDtypeStruct((G // 2, 4 * P, 2 * CHUNK_LANES), BF16)] + c_shapes,
        compiler_params=pltpu.CompilerParams(dimension_semantics=("parallel",),
                                             vmem_limit_bytes=VMEM_LIMIT_BYTES),
        name="ssm_prep",
    )(lambda_re, lambda_im, log_step.reshape(G, 1),
      b_re, b_im, c_re, c_im, d_skip.reshape(G, H, 1), *cast)


def _granule_transpose(units):
    lane = lax.broadcasted_iota(jnp.int32, units[0][0].shape, 1)
    units = [list(a) for a in units]
    for s in (4, 2, 1):
        upper = ((lane // SSM_GROUP) & s) != 0
        for a in units:
            for i in range(GROUPS_PER_BLOCK):
                if i & s:
                    continue
                lo, hi = a[i], a[i + s]
                a[i] = jnp.where(upper, pltpu.roll(hi, s * SSM_GROUP, axis=1), lo)
                a[i + s] = jnp.where(upper, hi, pltpu.roll(lo, LANES - s * SSM_GROUP, axis=1))
    return units


def _relayout_units(n_chunks):
    return [(b, cb, hf) for b in range(SSM_WIDTH // LANES)
            for cb in range(n_chunks // RELAYOUT_ROWS) for hf in range(2)]


def _relayout_to_groups(src_ref, dst_ref, n_chunks):
    half = CHUNK // 2
    rb = RELAYOUT_ROWS
    where = _relayout_units(n_chunks)
    batch = 2 * RELAYOUT_UNITS
    for i in range(0, len(where), batch):
        units = [[src_ref[b, pl.ds(cb * rb * PARK_PITCH + hf * half + t, rb, stride=PARK_PITCH), :].astype(BF16)
                  for t in range(half)] for (b, cb, hf) in where[i:i + batch]]
        units = _granule_transpose(units)
        for (b, cb, hf), unit in zip(where[i:i + batch], units):
            for g in range(GROUPS_PER_BLOCK):
                dst_ref[b * GROUPS_PER_BLOCK + g, cb * rb:(cb + 1) * rb, hf * LANES:(hf + 1) * LANES] = unit[g]


def _relayout_from_groups(src_ref, dst_ref, n_chunks, blocks=range(SSM_WIDTH // LANES)):
    half = CHUNK // 2
    rb = RELAYOUT_ROWS
    where = [u for u in _relayout_units(n_chunks) if u[0] in blocks]
    batch = 2 * RELAYOUT_UNITS
    for i in range(0, len(where), batch):
        units = [[src_ref[b * GROUPS_PER_BLOCK + g, cb * rb:(cb + 1) * rb, hf * LANES:(hf + 1) * LANES]
                  for g in range(GROUPS_PER_BLOCK)] for (b, cb, hf) in where[i:i + batch]]
        units = _granule_transpose(units)
        for (b, cb, hf), unit in zip(where[i:i + batch], units):
            for t in range(half):
                dst_ref[b, pl.ds(cb * rb * PARK_PITCH + hf * half + t, rb, stride=PARK_PITCH), :] = unit[t].astype(F32)


def _in_proj_kernel(n_cast, x_ref, gain_ref, w_ref, *refs):
    cast_src, (xg_ref, v_ref), cast_dst = refs[:n_cast], refs[n_cast:n_cast + 2], refs[n_cast + 2:-2]
    park = refs[-2:]
    _cast_slabs(cast_src, cast_dst)
    s = pl.program_id(0)

    @pl.when(s == 0)
    def _():
        park[1][...] = jnp.zeros_like(park[1])

    def step(p_prev, p_next):
        _relayout_to_groups(p_prev, xg_ref, x_ref.shape[0] // CHUNK)
        h = _rms_norm(x_ref[...], gain_ref[...]).astype(BF16)
        p = jnp.dot(h, w_ref[...], preferred_element_type=F32)
        v_ref[...] = p[:, SSM_WIDTH:].astype(v_ref.dtype)
        for b in range(SSM_WIDTH // LANES):
            for c in range(x_ref.shape[0] // CHUNK):
                p_next[b, c * PARK_PITCH:c * PARK_PITCH + CHUNK, :] = p[c * CHUNK:(c + 1) * CHUNK, b * LANES:(b + 1) * LANES]

    pl.when(s % 2 == 0)(lambda: step(park[1], park[0]))
    pl.when(s % 2 == 1)(lambda: step(park[0], park[1]))


def _in_proj(x, gain, w_in, cast, tm):
    L = x.shape[0]
    n = L // tm
    c_in, c_out, c_shapes = _cast_specs(cast, n)
    cur = lambda s: (jnp.minimum(s, n - 1), 0)
    return pl.pallas_call(
        functools.partial(_in_proj_kernel, len(cast)),
        grid=(n + 1,),
        in_specs=[pl.BlockSpec((tm, D_MODEL), cur),
                  pl.BlockSpec((1, D_MODEL), lambda s: (0, 0)),
                  pl.BlockSpec((D_MODEL, D_MODEL), lambda s: (0, 0))] + c_in,
        out_specs=[pl.BlockSpec((SSM_GROUPS, tm // CHUNK, CHUNK_LANES), lambda s: (0, jnp.maximum(s - 1, 0), 0)),
                   pl.BlockSpec((tm, POOL_WIDTH), cur)] + c_out,
        out_shape=[jax.ShapeDtypeStruct((SSM_GROUPS, L // CHUNK, CHUNK_LANES), BF16),
                   jax.ShapeDtypeStruct((L, POOL_WIDTH), BF16)] + c_shapes,
        scratch_shapes=[pltpu.VMEM((SSM_WIDTH // LANES, tm // CHUNK * PARK_PITCH, LANES), F32)] * 2,
        compiler_params=pltpu.CompilerParams(dimension_semantics=("arbitrary",),
                                             vmem_limit_bytes=VMEM_LIMIT_BYTES),
        name="in_proj",
    )(x, gain, w_in, *cast)


def _cmul(ar, ai, br, bi):
    return ar * br - ai * bi, ar * bi + ai * br


def _ssm_kernel(n_cast, xg_ref, m_ref, bw_ref, cw_ref, lre_ref, lim_ref, ls_ref, *refs):
    cast_src, yg_ref, cast_dst = refs[:n_cast], refs[n_cast], refs[n_cast + 1:-4]
    fr_ref, fi_ref, cr_ref, ci_ref = refs[-4:]
    _cast_slabs(cast_src, cast_dst)
    n_chunks = xg_ref.shape[1]
    sb = SCAN_BLOCK
    n_sb = n_chunks // sb

    blk3 = (n_sb, sb, 2 * SSM_STATE)
    row_in = lax.broadcasted_iota(jnp.int32, blk3, 1)
    brow = lax.broadcasted_iota(jnp.int32, (n_sb, 2 * SSM_STATE), 0)
    for q in range(GROUPS_PER_BLOCK // 2):
        x0, x1 = xg_ref[2 * q], xg_ref[2 * q + 1]
        xp = jnp.concatenate([x0, x1], axis=1)
        e = lax.dot_general(xp, bw_ref[q], (((1,), (1,)), ((), ())), preferred_element_type=F32)
        step = jnp.exp(ls_ref[q])
        zr, zi = lre_ref[q] * step * CHUNK, lim_ref[q] * step * CHUNK
        mag = jnp.exp(zr)
        a1r, a1i = mag * jnp.cos(zi), mag * jnp.sin(zi)

        fr = e[:, :2 * SSM_STATE].reshape(blk3)
        fi = e[:, 2 * SSM_STATE:].reshape(blk3)
        ar, ai = a1r, a1i
        d = 1
        while d < sb:
            tr = jnp.where(row_in >= d, pltpu.roll(fr, d, axis=1), 0.0)
            ti = jnp.where(row_in >= d, pltpu.roll(fi, d, axis=1), 0.0)
            fr, fi = fr + ar * tr - ai * ti, fi + ar * ti + ai * tr
            ar, ai = _cmul(ar, ai, ar, ai)
            d *= 2
        fr_ref[...] = fr.reshape(n_chunks, 2 * SSM_STATE)
        fi_ref[...] = fi.reshape(n_chunks, 2 * SSM_STATE)
        gr = fr_ref[pl.ds(sb - 1, n_sb, stride=sb), :]
        gi = fi_ref[pl.ds(sb - 1, n_sb, stride=sb), :]
        gr = jnp.where(brow >= 1, pltpu.roll(gr, 1, axis=0), 0.0)
        gi = jnp.where(brow >= 1, pltpu.roll(gi, 1, axis=0), 0.0)
        d = 1
        while d < n_sb:
            tr = jnp.where(brow >= d, pltpu.roll(gr, d, axis=0), 0.0)
            ti = jnp.where(brow >= d, pltpu.roll(gi, d, axis=0), 0.0)
            gr, gi = gr + ar * tr - ai * ti, gi + ar * ti + ai * tr
            ar, ai = _cmul(ar, ai, ar, ai)
            d *= 2
        pr, pi = jnp.ones_like(a1r), jnp.zeros_like(a1i)
        for r in range(sb):
            cr_ref[pl.ds(r, n_sb, stride=sb), :] = pr * gr - pi * gi
            ci_ref[pl.ds(r, n_sb, stride=sb), :] = pr * gi + pi * gr
            pr, pi = _cmul(pr, pi, a1r, a1i)
        sr = jnp.where(row_in >= 1, pltpu.roll(fr, 1, axis=1), 0.0).reshape(n_chunks, 2 * SSM_STATE)
        si = jnp.where(row_in >= 1, pltpu.roll(fi, 1, axis=1), 0.0).reshape(n_chunks, 2 * SSM_STATE)
        sr = sr + cr_ref[...]
        si = si + ci_ref[...]
        s = jnp.concatenate([sr, si], axis=1).astype(BF16)
        yc = jnp.dot(s, cw_ref[q], preferred_element_type=F32)
        y0 = jnp.dot(x0, m_ref[2 * q], preferred_element_type=F32) + yc[:, :CHUNK_LANES]
        y1 = jnp.dot(x1, m_ref[2 * q + 1], preferred_element_type=F32) + yc[:, CHUNK_LANES:]
        yg_ref[2 * q] = y0.astype(yg_ref.dtype)
        yg_ref[2 * q + 1] = y1.astype(yg_ref.dtype)


def _ssm(xg, m, bw_pair, cw_pair, lre_row, lim_row, ls_row, cast):
    n_chunks = xg.shape[1]
    n_blocks = SSM_WIDTH // LANES
    ppb = GROUPS_PER_BLOCK // 2
    c_in, c_out, c_shapes = _cast_specs(cast, n_blocks)
    return pl.pallas_call(
        functools.partial(_ssm_kernel, len(cast)),
        grid=(n_blocks,),
        in_specs=[pl.BlockSpec((GROUPS_PER_BLOCK, n_chunks, CHUNK_LANES), lambda b: (b, 0, 0)),
                  pl.BlockSpec((GROUPS_PER_BLOCK, CHUNK_LANES, CHUNK_LANES), lambda b: (b, 0, 0)),
                  pl.BlockSpec((ppb, 4 * SSM_STATE, 2 * CHUNK_LANES), lambda b: (b, 0, 0)),
                  pl.BlockSpec((ppb, 4 * SSM_STATE, 2 * CHUNK_LANES), lambda b: (b, 0, 0)),
                  pl.BlockSpec((ppb, 1, 2 * SSM_STATE), lambda b: (b, 0, 0)),
                  pl.BlockSpec((ppb, 1, 2 * SSM_STATE), lambda b: (b, 0, 0)),
                  pl.BlockSpec((ppb, 1, 2 * SSM_STATE), lambda b: (b, 0, 0))] + c_in,
        out_specs=[pl.BlockSpec((GROUPS_PER_BLOCK, n_chunks, CHUNK_LANES), lambda b: (b, 0, 0))] + c_out,
        out_shape=[jax.ShapeDtypeStruct(xg.shape, BF16)] + c_shapes,
        scratch_shapes=[pltpu.VMEM((n_chunks, 2 * SSM_STATE), F32)] * 4,
        compiler_params=pltpu.CompilerParams(dimension_semantics=("parallel",),
                                             vmem_limit_bytes=VMEM_LIMIT_BYTES),
        name="ssm",
    )(xg, m, bw_pair, cw_pair, lre_row, lim_row, ls_row, *cast)


def _out_proj_kernel(n_cast, yg_ref, v_ref, vh_ref, x_ref, wglu_ref, bglu_ref, wpool_ref, bpool_ref,
                     pscale_ref, wtop_ref, wbot_ref, *refs):
    cast_src, o_ref, cast_dst = refs[:n_cast], refs[n_cast], refs[n_cast + 1:-5]
    mix_park, y_ref, vbuf_ref, psum_ref = refs[-5:-3], refs[-3], refs[-2], refs[-1]
    _cast_slabs(cast_src, cast_dst)
    s = pl.program_id(0)
    tm = x_ref.shape[0]

    @pl.when(s == 0)
    def _():
        mix_park[1][...] = jnp.zeros_like(mix_park[1])

    def step(mix_prev, mix_next):
        n_tile = 2 * LANES

        def project(n):
            c = slice(n * n_tile, (n + 1) * n_tile)
            o_ref[:, c] = (x_ref[:, c]
                           + jnp.dot(mix_prev[:, :SSM_WIDTH], wtop_ref[:, c], preferred_element_type=F32)
                           + jnp.dot(mix_prev[:, SSM_WIDTH:], wbot_ref[:, c], preferred_element_type=F32))

        t0 = POOL_PAD + HALO
        n_lb = POOL_GROUP // LANES
        vh = jnp.where(s > 0, vh_ref[...].astype(F32), 0.0)
        vt = v_ref[...].astype(F32)
        for j in range(POOL_WIDTH // LANES):
            vbuf_ref[j, 0:POOL_PAD, :] = jnp.zeros((POOL_PAD, LANES), F32)
            vbuf_ref[j, POOL_PAD:t0, :] = vh[:, j * LANES:(j + 1) * LANES]
            vbuf_ref[j, t0:, :] = vt[:, j * LANES:(j + 1) * LANES]
        t1 = (lax.broadcasted_iota(jnp.int32, (tm, 1), 0) + (s * tm + 1)).astype(F32)

        def window_sum(slab, first, rows, terms, step):
            acc = slab[first:first + rows, :]
            for i in range(1, terms):
                acc = acc + slab[first - i * step:first - i * step + rows, :]
            return acc

        def pool_window(k):
            w = POOL_WINDOWS[k]
            out = []
            for j in range(k * n_lb, (k + 1) * n_lb):
                slab = vbuf_ref.at[j]
                if w <= 4:
                    total = window_sum(slab, t0, tm, w, 1)
                else:
                    part = psum_ref.at[j % (2 * n_lb)]
                    part[POOL_PAD:, :] = window_sum(slab, POOL_PAD, HALO + tm, w // 4, 1)
                    total = window_sum(part, t0, tm, 4, w // 4)
                mean = total / jnp.minimum(t1, float(w))
                out.append(mean - slab[t0:, :])
            return jnp.concatenate(out, axis=1).astype(BF16)

        def gelu(blocks):
            y = jnp.concatenate(
                [jnp.concatenate([y_ref[b, c * PARK_PITCH:c * PARK_PITCH + CHUNK, :] for c in range(tm // CHUNK)], axis=0)
                 for b in blocks], axis=1)
            return 0.5 * y * (1.0 + lax.erf(y * (1.0 / math.sqrt(2.0))))

        half_blocks = SSM_WIDTH // LANES // 2
        pk = [pool_window(0), pool_window(1)]
        project(0)
        pk.append(pool_window(2))
        project(1)
        pk.append(pool_window(3))
        project(2)
        pooled = [jnp.dot(pk[k], wpool_ref[k], preferred_element_type=F32) for k in range(len(POOL_WINDOWS))]
        yp = (jnp.concatenate(pooled, axis=1) + bpool_ref[...]) * pscale_ref[...]
        mix_next[:, SSM_WIDTH:] = yp.astype(BF16)
        _relayout_from_groups(yg_ref, y_ref, tm // CHUNK, blocks=range(0, half_blocks))
        project(3)
        _relayout_from_groups(yg_ref, y_ref, tm // CHUNK, blocks=range(half_blocks, 2 * half_blocks))
        project(4)
        g_a = gelu(range(0, half_blocks))
        project(5)
        g_b = gelu(range(half_blocks, 2 * half_blocks))
        project(6)
        g = jnp.concatenate([g_a, g_b], axis=1)
        z = jnp.dot(g.astype(BF16), wglu_ref[...], preferred_element_type=F32) + bglu_ref[...]
        mix_next[:, :SSM_WIDTH] = (g * (1.0 / (1.0 + jnp.exp(-z)))).astype(BF16)
        project(7)

    pl.when(s % 2 == 0)(lambda: step(mix_park[1], mix_park[0]))
    pl.when(s % 2 == 1)(lambda: step(mix_park[0], mix_park[1]))


def _out_proj(yg, v, x, w_glu, b_glu, w_pool, b_pool, pool_scale, w_out, cast, tm):
    L = x.shape[0]
    n = L // tm
    halo_blocks = tm // HALO
    const = lambda *shape: pl.BlockSpec(shape, lambda s: (0,) * len(shape))
    c_in, c_out, c_shapes = _cast_specs(cast, n)
    cur = lambda s: jnp.minimum(s, n - 1)
    prev = lambda s: (jnp.maximum(s - 1, 0), 0)
    return pl.pallas_call(
        functools.partial(_out_proj_kernel, len(cast)),
        grid=(n + 1,),
        in_specs=[pl.BlockSpec((SSM_GROUPS, tm // CHUNK, CHUNK_LANES), lambda s: (0, cur(s), 0)),
                  pl.BlockSpec((tm, POOL_WIDTH), lambda s: (cur(s), 0)),
                  pl.BlockSpec((HALO, POOL_WIDTH), lambda s: (jnp.maximum(cur(s) * halo_blocks - 1, 0), 0)),
                  pl.BlockSpec((tm, D_MODEL), prev),
                  const(SSM_WIDTH, SSM_WIDTH), const(1, SSM_WIDTH),
                  const(len(POOL_WINDOWS), POOL_GROUP, POOL_GROUP), const(1, POOL_WIDTH),
                  const(1, POOL_WIDTH),
                  pl.BlockSpec((SSM_WIDTH, D_MODEL), lambda s: (0, 0)),
                  pl.BlockSpec((POOL_WIDTH, D_MODEL), lambda s: (1, 0))] + c_in,
        out_specs=[pl.BlockSpec((tm, D_MODEL), prev)] + c_out,
        out_shape=[jax.ShapeDtypeStruct((L, D_MODEL), F32)] + c_shapes,
        scratch_shapes=[pltpu.VMEM((tm, SSM_WIDTH + POOL_WIDTH), BF16)] * 2
                       + [pltpu.VMEM((SSM_WIDTH // LANES, tm // CHUNK * PARK_PITCH, LANES), F32),
                          pltpu.VMEM((POOL_WIDTH // LANES, POOL_PAD + HALO + tm, LANES), F32),
                          pltpu.VMEM((2 * POOL_GROUP // LANES, POOL_PAD + HALO + tm, LANES), F32)],
        compiler_params=pltpu.CompilerParams(dimension_semantics=("arbitrary",),
                                             vmem_limit_bytes=VMEM_LIMIT_BYTES),
        name="out_proj",
    )(yg, v, v, x, w_glu, b_glu, w_pool, b_pool, pool_scale, w_out, w_out, *cast)


def _ffn_kernel(x_ref, gain_ref, wg_ref, wu_ref, wd_ref, gfin_ref, o_ref, h_ref, r_ref):
    j = pl.program_id(1)
    last = pl.num_programs(1) - 1
    lane_tiles = wg_ref.shape[1] // LANES

    def slab(first, final):
        if first:
            x = x_ref[...]
            h = (x * gain_ref[...]).astype(BF16)
            h_ref[...] = h
            r = lax.rsqrt(jnp.mean(x * x, axis=-1, keepdims=True) + EPS)
            r = jnp.broadcast_to(r, r_ref.shape)
            r_ref[...] = r
        else:
            h, r = h_ref[...], r_ref[...]
        rt = jnp.concatenate([r] * lane_tiles, axis=1)
        a = jnp.dot(h, wg_ref[...], preferred_element_type=F32) * rt
        b = jnp.dot(h, wu_ref[...], preferred_element_type=F32) * rt
        ff = (a * (1.0 / (1.0 + jnp.exp(-a))) * b).astype(BF16)
        acc = (x if first else o_ref[...]) + jnp.dot(ff, wd_ref[...], preferred_element_type=F32)
        o_ref[...] = _rms_norm(acc, gfin_ref[...]) if final else acc

    pl.when(j == 0)(lambda: slab(True, False))
    pl.when(jnp.logical_and(j > 0, j < last))(lambda: slab(False, False))
    pl.when(j == last)(lambda: slab(False, True))


def _ffn(x, gain, w_gate, w_up, w_down, gain_final, tm, tf):
    L = x.shape[0]
    d_ff = w_gate.shape[1]
    assert d_ff // tf >= 2
    return pl.pallas_call(
        _ffn_kernel,
        grid=(L // tm, d_ff // tf),
        in_specs=[pl.BlockSpec((tm, D_MODEL), lambda i, j: (i, 0)),
                  pl.BlockSpec((1, D_MODEL), lambda i, j: (0, 0)),
                  pl.BlockSpec((D_MODEL, tf), lambda i, j: (0, j)),
                  pl.BlockSpec((D_MODEL, tf), lambda i, j: (0, j)),
                  pl.BlockSpec((tf, D_MODEL), lambda i, j: (j, 0)),
                  pl.BlockSpec((1, D_MODEL), lambda i, j: (0, 0))],
        out_specs=pl.BlockSpec((tm, D_MODEL), lambda i, j: (i, 0)),
        out_shape=jax.ShapeDtypeStruct((L, D_MODEL), F32),
        scratch_shapes=[pltpu.VMEM((tm, D_MODEL), BF16), pltpu.VMEM((tm, LANES), F32)],
        compiler_params=pltpu.CompilerParams(dimension_semantics=("parallel", "arbitrary"),
                                             vmem_limit_bytes=VMEM_LIMIT_BYTES),
        name="ffn",
    )(x, gain, w_gate, w_up, w_down, gain_final)


def kernel(x, norm_mix, w_in, lambda_re, lambda_im, log_step, b_re, b_im, c_re, c_im, d_skip,
           w_glu, b_glu, w_pool, b_pool, pool_scale, w_out, norm_ffn, w_gate, w_up, w_down,
           norm_final):
    bsz, L, D = x.shape
    assert bsz == 1 and D == D_MODEL and w_in.shape[0] == 1 and L % 1024 == 0
    G, P = SSM_GROUPS, SSM_STATE
    xs = x.reshape(L, D).astype(F32)

    n_pool = len(POOL_WINDOWS)
    m, bw_pair, cw_pair, w_in_b, w_out_b, w_glu_b, w_pool_b = _ssm_prep(
        lambda_re[0].astype(F32), lambda_im[0].astype(F32), log_step[0].astype(F32),
        b_re[0].astype(F32), b_im[0].astype(F32), c_re[0].astype(F32), c_im[0].astype(F32),
        d_skip[0].astype(F32),
        cast=[w_in[0], w_out[0], w_glu[0], w_pool[0].reshape(n_pool * POOL_GROUP, POOL_GROUP)])
    lre_row = lambda_re[0].astype(F32).reshape(G // 2, 1, 2 * P)
    lim_row = lambda_im[0].astype(F32).reshape(G // 2, 1, 2 * P)
    ls_row = jnp.repeat(log_step[0].astype(F32), P).reshape(G // 2, 1, 2 * P)

    xg, v, w_gate_b = _in_proj(xs, norm_mix[0].astype(F32).reshape(1, D), w_in_b, cast=[w_gate[0]], tm=512)

    yg, w_down_b = _ssm(xg, m, bw_pair, cw_pair, lre_row, lim_row, ls_row, cast=[w_down[0]])

    x1, w_up_b = _out_proj(
        yg, v, xs, w_glu_b, b_glu[0].astype(F32).reshape(1, SSM_WIDTH),
        w_pool_b.reshape(n_pool, POOL_GROUP, POOL_GROUP), b_pool[0].astype(F32).reshape(1, POOL_WIDTH),
        pool_scale[0].astype(F32).reshape(1, POOL_WIDTH), w_out_b, cast=[w_up[0]], tm=512)

    out = _ffn(x1, norm_ffn[0].astype(F32).reshape(1, D), w_gate_b, w_up_b, w_down_b,
               norm_final.astype(F32).reshape(1, D), tm=1024, tf=512)
    return out.reshape(bsz, L, D).astype(x.dtype)
```

```python
import functools
import math

import jax
import jax.numpy as jnp
from jax import lax
from jax.experimental import pallas as pl
from jax.experimental.pallas import tpu as pltpu

F32 = jnp.float32
BF16 = jnp.bfloat16

D_MODEL = 2048
SSM_WIDTH = 1024
POOL_WIDTH = 1024
SSM_GROUP = 16
SSM_GROUPS = 64
SSM_STATE = 64
POOL_WINDOWS = (2, 4, 8, 16)
POOL_GROUP = 256
EPS = 1e-6

CHUNK = 16
CHUNK_LANES = CHUNK * SSM_GROUP
HALO = 16
POOL_PAD = 8
LANES = 128
GROUPS_PER_BLOCK = LANES // SSM_GROUP
RELAYOUT_ROWS = 16
RELAYOUT_UNITS = 4
SCAN_BLOCK = 8
PARK_PITCH = 20

VMEM_LIMIT_BYTES = 60 * 1024 * 1024

HIGHEST = lax.Precision.HIGHEST


def _rms_norm(x, gain):
    ms = jnp.mean(x * x, axis=-1, keepdims=True)
    return x * lax.rsqrt(ms + EPS) * gain


def _cast_specs(weights, n_steps):
    in_specs, out_specs, out_shapes = [], [], []
    for w in weights:
        rows, cols = w.shape
        slab = pl.BlockSpec((rows // n_steps, cols), lambda i: (jnp.minimum(i, n_steps - 1), 0))
        in_specs.append(slab)
        out_specs.append(slab)
        out_shapes.append(jax.ShapeDtypeStruct((rows, cols), BF16))
    return in_specs, out_specs, out_shapes


def _cast_slabs(src_refs, dst_refs):
    for src, dst in zip(src_refs, dst_refs):
        dst[...] = src[...].astype(dst.dtype)


def _ssm_prep_kernel(n_cast, lre_ref, lim_ref, ls_ref, b_re_ref, b_im_ref, c_re_ref, c_im_ref, d_ref, *refs):
    cast_src, (m_ref, bw_ref, cw_ref), cast_dst = refs[:n_cast], refs[n_cast:n_cast + 3], refs[n_cast + 3:]
    _cast_slabs(cast_src, cast_dst)
    P, H, T, GB = SSM_STATE, SSM_GROUP, CHUNK, GROUPS_PER_BLOCK
    lre, lim = lre_ref[...], lim_ref[...]
    step = jnp.exp(ls_ref[...])
    mag = jnp.exp(lre * step)
    lbr, lbi = mag * jnp.cos(lim * step), mag * jnp.sin(lim * step)
    den = lre * lre + lim * lim
    cfr = ((lbr - 1.0) * lre + lbi * lim) / den
    cfi = (lbi * lre - (lbr - 1.0) * lim) / den
    eye = (lax.broadcasted_iota(jnp.int32, (P, P), 0)
           == lax.broadcasted_iota(jnp.int32, (P, P), 1)).astype(F32)
    rows = jnp.concatenate([lbr, lbi, cfr, cfi], axis=0)
    cols = lax.dot_general(eye, rows, (((1,), (1,)), ((), ())), precision=HIGHEST,
                           preferred_element_type=F32)

    lane = lax.broadcasted_iota(jnp.int32, (1, CHUNK_LANES), 1)
    k_idx = lane // H
    sub = lax.broadcasted_iota(jnp.int32, (H, CHUNK_LANES), 0)
    lane_h = lax.broadcasted_iota(jnp.int32, (H, CHUNK_LANES), 1)
    rep = (lane_h % H == sub).astype(F32)

    def tile(a):
        return jnp.dot(a, rep, precision=HIGHEST, preferred_element_type=F32)

    def tile_t(a):
        return lax.dot_general(a, rep, (((0,), (0,)), ((), ())), precision=HIGHEST,
                               preferred_element_type=F32)

    def dot_t(a, b):
        return lax.dot_general(a, b, (((0,), (0,)), ((), ())), precision=HIGHEST,
                               preferred_element_type=F32)

    def cmul(ar, ai, br, bi):
        return ar * br - ai * bi, ar * bi + ai * br

    bw_ref[...] = jnp.zeros_like(bw_ref)
    cw_ref[...] = jnp.zeros_like(cw_ref)
    for g in range(GB):
        col = lambda quantity: cols[:, quantity * GB + g:quantity * GB + g + 1]
        l1r, l1i, cr, ci = col(0), col(1), col(2), col(3)
        sq = [(l1r, l1i)]
        for _ in range(3):
            sq.append(cmul(*sq[-1], *sq[-1]))
        pkr = pki = prr = pri = None
        for bit, (fr, fi) in enumerate(sq):
            on = ((k_idx >> bit) & 1) == 1
            ur, ui = jnp.where(on, fr, 1.0), jnp.where(on, fi, 0.0)
            dr, di = jnp.where(on, 1.0, fr), jnp.where(on, 0.0, fi)
            if bit == 0:
                pkr, pki, prr, pri = ur, ui, dr, di
            else:
                pkr, pki = cmul(pkr, pki, ur, ui)
                prr, pri = cmul(prr, pri, dr, di)

        bb_re, bb_im = cmul(cr, ci, b_re_ref[g], b_im_ref[g])

        q, j = g // 2, g % 2
        r_re = slice(j * P, (j + 1) * P)
        r_im = slice((2 + j) * P, (3 + j) * P)
        c_g = slice(j * CHUNK_LANES, (j + 1) * CHUNK_LANES)
        w_re, w_im = cmul(prr, pri, tile(bb_re), tile(bb_im))
        bw_ref[q, r_re, c_g] = w_re.astype(bw_ref.dtype)
        bw_ref[q, r_im, c_g] = w_im.astype(bw_ref.dtype)
        g_re, g_im = cmul(pkr, pki, tile_t(c_re_ref[g]), tile_t(c_im_ref[g]))
        g1_re, g1_im = cmul(l1r, l1i, g_re, g_im)
        cw_ref[q, r_re, c_g] = g1_re.astype(cw_ref.dtype)
        cw_ref[q, r_im, c_g] = (-g1_im).astype(cw_ref.dtype)
        kt = dot_t(bb_re, g_re) - dot_t(bb_im, g_im)
        kt = kt + jnp.where(lane_h == sub, d_ref[g], 0.0)
        for tau in range(T):
            blk = kt if tau == 0 else jnp.where(lane_h >= tau * H, pltpu.roll(kt, tau * H, axis=1), 0.0)
            m_ref[g, tau * H:(tau + 1) * H, :] = blk.astype(m_ref.dtype)


def _ssm_prep(lambda_re, lambda_im, log_step, b_re, b_im, c_re, c_im, d_skip, cast):
    G, P, H, GB = SSM_GROUPS, SSM_STATE, SSM_GROUP, GROUPS_PER_BLOCK
    n_steps = G // GB
    blk = lambda *shape: pl.BlockSpec(shape, lambda i: (i,) + (0,) * (len(shape) - 1))
    c_in, c_out, c_shapes = _cast_specs(cast, n_steps)
    return pl.pallas_call(
        functools.partial(_ssm_prep_kernel, len(cast)),
        grid=(n_steps,),
        in_specs=[blk(GB, P), blk(GB, P), blk(GB, 1),
                  blk(GB, P, H), blk(GB, P, H), blk(GB, H, P), blk(GB, H, P), blk(GB, H, 1)] + c_in,
        out_specs=[blk(GB, CHUNK_LANES, CHUNK_LANES), blk(GB // 2, 4 * P, 2 * CHUNK_LANES),
                   blk(GB // 2, 4 * P, 2 * CHUNK_LANES)] + c_out,
        out_shape=[jax.ShapeDtypeStruct((G, CHUNK_LANES, CHUNK_LANES), BF16),
                   jax.ShapeDtypeStruct((G // 2, 4 * P, 2 * CHUNK_LANES), BF16),
                   jax.ShapeDtypeStruct((G // 2, 4 * P, 2 * CHUNK_LANES), BF16)] + c_shapes,
        compiler_params=pltpu.CompilerParams(dimension_semantics=("parallel",),
                                             vmem_limit_bytes=VMEM_LIMIT_BYTES),
        name="ssm_prep",
    )(lambda_re, lambda_im, log_step.reshape(G, 1),
      b_re, b_im, c_re, c_im, d_skip.reshape(G, H, 1), *cast)


def _granule_transpose(units):
    lane = lax.broadcasted_iota(jnp.int32, units[0][0].shape, 1)
    units = [list(a) for a in units]
    for s in (4, 2, 1):
        upper = ((lane // SSM_GROUP) & s) != 0
        for a in units:
            for i in range(GROUPS_PER_BLOCK):
                if i & s:
                    continue
                lo, hi = a[i], a[i + s]
                a[i] = jnp.where(upper, pltpu.roll(hi, s * SSM_GROUP, axis=1), lo)
                a[i + s] = jnp.where(upper, hi, pltpu.roll(lo, LANES - s * SSM_GROUP, axis=1))
    return units


def _relayout_units(n_chunks):
    return [(b, cb, hf) for b in range(SSM_WIDTH // LANES)
            for cb in range(n_chunks // RELAYOUT_ROWS) for hf in range(2)]


def _relayout_to_groups(src_ref, dst_ref, n_chunks):
    half = CHUNK // 2
    rb = RELAYOUT_ROWS
    where = _relayout_units(n_chunks)
    batch = 2 * RELAYOUT_UNITS
    for i in range(0, len(where), batch):
        units = [[src_ref[b, pl.ds(cb * rb * PARK_PITCH + hf * half + t, rb, stride=PARK_PITCH), :].astype(BF16)
                  for t in range(half)] for (b, cb, hf) in where[i:i + batch]]
        units = _granule_transpose(units)
        for (b, cb, hf), unit in zip(where[i:i + batch], units):
            for g in range(GROUPS_PER_BLOCK):
                dst_ref[b * GROUPS_PER_BLOCK + g, cb * rb:(cb + 1) * rb, hf * LANES:(hf + 1) * LANES] = unit[g]


def _relayout_from_groups(src_ref, dst_ref, n_chunks, blocks=range(SSM_WIDTH // LANES)):
    half = CHUNK // 2
    rb = RELAYOUT_ROWS
    where = [u for u in _relayout_units(n_chunks) if u[0] in blocks]
    batch = 2 * RELAYOUT_UNITS
    for i in range(0, len(where), batch):
        units = [[src_ref[b * GROUPS_PER_BLOCK + g, cb * rb:(cb + 1) * rb, hf * LANES:(hf + 1) * LANES]
                  for g in range(GROUPS_PER_BLOCK)] for (b, cb, hf) in where[i:i + batch]]
        units = _granule_transpose(units)
        for (b, cb, hf), unit in zip(where[i:i + batch], units):
            for t in range(half):
                dst_ref[b, pl.ds(cb * rb * PARK_PITCH + hf * half + t, rb, stride=PARK_PITCH), :] = unit[t].astype(F32)


def _in_proj_kernel(n_cast, n, x_ref, gain_ref, w_ref, *refs):
    cast_src, (xg_ref, v_ref), cast_dst = refs[:n_cast], refs[n_cast:n_cast + 2], refs[n_cast + 2:-2]
    park = refs[-2:]
    s = pl.program_id(0)

    def step(p_prev, p_next):
        if p_prev is not None:
            _relayout_to_groups(p_prev, xg_ref, x_ref.shape[0] // CHUNK)
        if p_next is not None:
            _cast_slabs(cast_src, cast_dst)
            h = _rms_norm(x_ref[...], gain_ref[...]).astype(BF16)
            p = jnp.dot(h, w_ref[...], preferred_element_type=F32)
            v_ref[...] = p[:, SSM_WIDTH:].astype(v_ref.dtype)
            for b in range(SSM_WIDTH // LANES):
                for c in range(x_ref.shape[0] // CHUNK):
                    p_next[b, c * PARK_PITCH:c * PARK_PITCH + CHUNK, :] = (
                        p[c * CHUNK:(c + 1) * CHUNK, b * LANES:(b + 1) * LANES])

    middle = jnp.logical_and(s > 0, s < n)
    pl.when(s == 0)(lambda: step(None, park[0]))
    pl.when(jnp.logical_and(middle, s % 2 == 0))(lambda: step(park[1], park[0]))
    pl.when(jnp.logical_and(middle, s % 2 == 1))(lambda: step(park[0], park[1]))
    pl.when(s == n)(lambda: step(park[(n - 1) % 2], None))


def _in_proj(x, gain, w_in, cast, tm):
    L = x.shape[0]
    n = L // tm
    c_in, c_out, c_shapes = _cast_specs(cast, n)
    cur = lambda s: (jnp.minimum(s, n - 1), 0)
    return pl.pallas_call(
        functools.partial(_in_proj_kernel, len(cast), n),
        grid=(n + 1,),
        in_specs=[pl.BlockSpec((tm, D_MODEL), cur),
                  pl.BlockSpec((1, D_MODEL), lambda s: (0, 0)),
                  pl.BlockSpec((D_MODEL, D_MODEL), lambda s: (0, 0))] + c_in,
        out_specs=[pl.BlockSpec((SSM_GROUPS, tm // CHUNK, CHUNK_LANES), lambda s: (0, jnp.maximum(s - 1, 0), 0)),
                   pl.BlockSpec((tm, POOL_WIDTH), cur)] + c_out,
        out_shape=[jax.ShapeDtypeStruct((SSM_GROUPS, L // CHUNK, CHUNK_LANES), BF16),
                   jax.ShapeDtypeStruct((L, POOL_WIDTH), BF16)] + c_shapes,
        scratch_shapes=[pltpu.VMEM((SSM_WIDTH // LANES, tm // CHUNK * PARK_PITCH, LANES), F32)] * 2,
        compiler_params=pltpu.CompilerParams(dimension_semantics=("arbitrary",),
                                             vmem_limit_bytes=VMEM_LIMIT_BYTES),
        name="in_proj",
    )(x, gain, w_in, *cast)


def _cmul(ar, ai, br, bi):
    return ar * br - ai * bi, ar * bi + ai * br


def _ssm_kernel(n_cast, xg_ref, m_ref, bw_ref, cw_ref, lre_ref, lim_ref, ls_ref, *refs):
    cast_src, yg_ref, cast_dst = refs[:n_cast], refs[n_cast], refs[n_cast + 1:-4]
    fr_ref, fi_ref, cr_ref, ci_ref = refs[-4:]
    _cast_slabs(cast_src, cast_dst)
    n_chunks = xg_ref.shape[1]
    sb = SCAN_BLOCK
    n_sb = n_chunks // sb

    blk3 = (n_sb, sb, 2 * SSM_STATE)
    row_in = lax.broadcasted_iota(jnp.int32, blk3, 1)
    brow = lax.broadcasted_iota(jnp.int32, (n_sb, 2 * SSM_STATE), 0)
    for q in range(GROUPS_PER_BLOCK // 2):
        x0, x1 = xg_ref[2 * q], xg_ref[2 * q + 1]
        xp = jnp.concatenate([x0, x1], axis=1)
        e = lax.dot_general(xp, bw_ref[q], (((1,), (1,)), ((), ())), preferred_element_type=F32)
        step = jnp.exp(ls_ref[q])
        zr, zi = lre_ref[q] * step * CHUNK, lim_ref[q] * step * CHUNK
        mag = jnp.exp(zr)
        a1r, a1i = mag * jnp.cos(zi), mag * jnp.sin(zi)

        fr = e[:, :2 * SSM_STATE].reshape(blk3)
        fi = e[:, 2 * SSM_STATE:].reshape(blk3)
        ar, ai = a1r, a1i
        d = 1
        while d < sb:
            tr = jnp.where(row_in >= d, pltpu.roll(fr, d, axis=1), 0.0)
            ti = jnp.where(row_in >= d, pltpu.roll(fi, d, axis=1), 0.0)
            fr, fi = fr + ar * tr - ai * ti, fi + ar * ti + ai * tr
            ar, ai = _cmul(ar, ai, ar, ai)
            d *= 2
        fr_ref[...] = fr.reshape(n_chunks, 2 * SSM_STATE)
        fi_ref[...] = fi.reshape(n_chunks, 2 * SSM_STATE)
        gr = fr_ref[pl.ds(sb - 1, n_sb, stride=sb), :]
        gi = fi_ref[pl.ds(sb - 1, n_sb, stride=sb), :]
        gr = jnp.where(brow >= 1, pltpu.roll(gr, 1, axis=0), 0.0)
        gi = jnp.where(brow >= 1, pltpu.roll(gi, 1, axis=0), 0.0)
        d = 1
        while d < n_sb:
            tr = jnp.where(brow >= d, pltpu.roll(gr, d, axis=0), 0.0)
            ti = jnp.where(brow >= d, pltpu.roll(gi, d, axis=0), 0.0)
            gr, gi = gr + ar * tr - ai * ti, gi + ar * ti + ai * tr
            ar, ai = _cmul(ar, ai, ar, ai)
            d *= 2
        pr, pi = jnp.ones_like(a1r), jnp.zeros_like(a1i)
        for r in range(sb):
            cr_ref[pl.ds(r, n_sb, stride=sb), :] = pr * gr - pi * gi
            ci_ref[pl.ds(r, n_sb, stride=sb), :] = pr * gi + pi * gr
            pr, pi = _cmul(pr, pi, a1r, a1i)
        sr = jnp.where(row_in >= 1, pltpu.roll(fr, 1, axis=1), 0.0).reshape(n_chunks, 2 * SSM_STATE)
        si = jnp.where(row_in >= 1, pltpu.roll(fi, 1, axis=1), 0.0).reshape(n_chunks, 2 * SSM_STATE)
        sr = sr + cr_ref[...]
        si = si + ci_ref[...]
        s = jnp.concatenate([sr, si], axis=1).astype(BF16)
        yc = jnp.dot(s, cw_ref[q], preferred_element_type=F32)
        y0 = jnp.dot(x0, m_ref[2 * q], preferred_element_type=F32) + yc[:, :CHUNK_LANES]
        y1 = jnp.dot(x1, m_ref[2 * q + 1], preferred_element_type=F32) + yc[:, CHUNK_LANES:]
        yg_ref[2 * q] = y0.astype(yg_ref.dtype)
        yg_ref[2 * q + 1] = y1.astype(yg_ref.dtype)


def _ssm(xg, m, bw_pair, cw_pair, lre_row, lim_row, ls_row, cast):
    n_chunks = xg.shape[1]
    n_blocks = SSM_WIDTH // LANES
    ppb = GROUPS_PER_BLOCK // 2
    c_in, c_out, c_shapes = _cast_specs(cast, n_blocks)
    return pl.pallas_call(
        functools.partial(_ssm_kernel, len(cast)),
        grid=(n_blocks,),
        in_specs=[pl.BlockSpec((GROUPS_PER_BLOCK, n_chunks, CHUNK_LANES), lambda b: (b, 0, 0)),
                  pl.BlockSpec((GROUPS_PER_BLOCK, CHUNK_LANES, CHUNK_LANES), lambda b: (b, 0, 0)),
                  pl.BlockSpec((ppb, 4 * SSM_STATE, 2 * CHUNK_LANES), lambda b: (b, 0, 0)),
                  pl.BlockSpec((ppb, 4 * SSM_STATE, 2 * CHUNK_LANES), lambda b: (b, 0, 0)),
                  pl.BlockSpec((ppb, 1, 2 * SSM_STATE), lambda b: (b, 0, 0)),
                  pl.BlockSpec((ppb, 1, 2 * SSM_STATE), lambda b: (b, 0, 0)),
                  pl.BlockSpec((ppb, 1, 2 * SSM_STATE), lambda b: (b, 0, 0))] + c_in,
        out_specs=[pl.BlockSpec((GROUPS_PER_BLOCK, n_chunks, CHUNK_LANES), lambda b: (b, 0, 0))] + c_out,
        out_shape=[jax.ShapeDtypeStruct(xg.shape, BF16)] + c_shapes,
        scratch_shapes=[pltpu.VMEM((n_chunks, 2 * SSM_STATE), F32)] * 4,
        compiler_params=pltpu.CompilerParams(dimension_semantics=("parallel",),
                                             vmem_limit_bytes=VMEM_LIMIT_BYTES),
        name="ssm",
    )(xg, m, bw_pair, cw_pair, lre_row, lim_row, ls_row, *cast)


def _out_proj_kernel(n_cast, yg_ref, v_ref, vh_ref, x_ref, wglu_ref, bglu_ref, wpool_ref, bpool_ref,
                     pscale_ref, wtop_ref, wbot_ref, *refs):
    cast_src, o_ref, cast_dst = refs[:n_cast], refs[n_cast], refs[n_cast + 1:-5]
    mix_park, y_ref, vbuf_ref, psum_ref = refs[-5:-3], refs[-3], refs[-2], refs[-1]
    _cast_slabs(cast_src, cast_dst)
    s = pl.program_id(0)
    tm = x_ref.shape[0]

    @pl.when(s == 0)
    def _():
        mix_park[1][...] = jnp.zeros_like(mix_park[1])

    def step(mix_prev, mix_next):
        n_tile = 2 * LANES

        def project(n):
            c = slice(n * n_tile, (n + 1) * n_tile)
            o_ref[:, c] = (x_ref[:, c]
                           + jnp.dot(mix_prev[:, :SSM_WIDTH], wtop_ref[:, c], preferred_element_type=F32)
                           + jnp.dot(mix_prev[:, SSM_WIDTH:], wbot_ref[:, c], preferred_element_type=F32))

        t0 = POOL_PAD + HALO
        n_lb = POOL_GROUP // LANES
        vh = jnp.where(s > 0, vh_ref[...].astype(F32), 0.0)
        vt = v_ref[...].astype(F32)
        for j in range(POOL_WIDTH // LANES):
            vbuf_ref[j, 0:POOL_PAD, :] = jnp.zeros((POOL_PAD, LANES), F32)
            vbuf_ref[j, POOL_PAD:t0, :] = vh[:, j * LANES:(j + 1) * LANES]
            vbuf_ref[j, t0:, :] = vt[:, j * LANES:(j + 1) * LANES]
        t1 = (lax.broadcasted_iota(jnp.int32, (tm, 1), 0) + (s * tm + 1)).astype(F32)

        def window_sum(slab, first, rows, terms, step):
            acc = slab[first:first + rows, :]
            for i in range(1, terms):
                acc = acc + slab[first - i * step:first - i * step + rows, :]
            return acc

        def pool_window(k):
            w = POOL_WINDOWS[k]
            out = []
            for j in range(k * n_lb, (k + 1) * n_lb):
                slab = vbuf_ref.at[j]
                if w <= 4:
                    total = window_sum(slab, t0, tm, w, 1)
                else:
                    part = psum_ref.at[j % (2 * n_lb)]
                    part[POOL_PAD:, :] = window_sum(slab, POOL_PAD, HALO + tm, w // 4, 1)
                    total = window_sum(part, t0, tm, 4, w // 4)
                mean = total / jnp.minimum(t1, float(w))
                out.append(mean - slab[t0:, :])
            return jnp.concatenate(out, axis=1).astype(BF16)

        def gelu(blocks):
            y = jnp.concatenate(
                [jnp.concatenate([y_ref[b, c * PARK_PITCH:c * PARK_PITCH + CHUNK, :] for c in range(tm // CHUNK)], axis=0)
                 for b in blocks], axis=1)
            return 0.5 * y * (1.0 + lax.erf(y * (1.0 / math.sqrt(2.0))))

        half_blocks = SSM_WIDTH // LANES // 2
        pk = [pool_window(0), pool_window(1)]
        project(0)
        pk.append(pool_window(2))
        project(1)
        pk.append(pool_window(3))
        project(2)
        pooled = [jnp.dot(pk[k], wpool_ref[k], preferred_element_type=F32) for k in range(len(POOL_WINDOWS))]
        yp = (jnp.concatenate(pooled, axis=1) + bpool_ref[...]) * pscale_ref[...]
        mix_next[:, SSM_WIDTH:] = yp.astype(BF16)
        _relayout_from_groups(yg_ref, y_ref, tm // CHUNK, blocks=range(0, half_blocks))
        project(3)
        _relayout_from_groups(yg_ref, y_ref, tm // CHUNK, blocks=range(half_blocks, 2 * half_blocks))
        project(4)
        g_a = gelu(range(0, half_blocks))
        project(5)
        g_b = gelu(range(half_blocks, 2 * half_blocks))
        project(6)
        g = jnp.concatenate([g_a, g_b], axis=1)
        z = jnp.dot(g.astype(BF16), wglu_ref[...], preferred_element_type=F32) + bglu_ref[...]
        mix_next[:, :SSM_WIDTH] = (g * (1.0 / (1.0 + jnp.exp(-z)))).astype(BF16)
        project(7)

    pl.when(s % 2 == 0)(lambda: step(mix_park[1], mix_park[0]))
    pl.when(s % 2 == 1)(lambda: step(mix_park[0], mix_park[1]))


def _out_proj(yg, v, x, w_glu, b_glu, w_pool, b_pool, pool_scale, w_out, cast, tm):
    L = x.shape[0]
    n = L // tm
    halo_blocks = tm // HALO
    const = lambda *shape: pl.BlockSpec(shape, lambda s: (0,) * len(shape))
    c_in, c_out, c_shapes = _cast_specs(cast, n)
    cur = lambda s: jnp.minimum(s, n - 1)
    prev = lambda s: (jnp.maximum(s - 1, 0), 0)
    return pl.pallas_call(
        functools.partial(_out_proj_kernel, len(cast)),
        grid=(n + 1,),
        in_specs=[pl.BlockSpec((SSM_GROUPS, tm // CHUNK, CHUNK_LANES), lambda s: (0, cur(s), 0)),
                  pl.BlockSpec((tm, POOL_WIDTH), lambda s: (cur(s), 0)),
                  pl.BlockSpec((HALO, POOL_WIDTH), lambda s: (jnp.maximum(cur(s) * halo_blocks - 1, 0), 0)),
                  pl.BlockSpec((tm, D_MODEL), prev),
                  const(SSM_WIDTH, SSM_WIDTH), const(1, SSM_WIDTH),
                  const(len(POOL_WINDOWS), POOL_GROUP, POOL_GROUP), const(1, POOL_WIDTH),
                  const(1, POOL_WIDTH),
                  pl.BlockSpec((SSM_WIDTH, D_MODEL), lambda s: (0, 0)),
                  pl.BlockSpec((POOL_WIDTH, D_MODEL), lambda s: (1, 0))] + c_in,
        out_specs=[pl.BlockSpec((tm, D_MODEL), prev)] + c_out,
        out_shape=[jax.ShapeDtypeStruct((L, D_MODEL), F32)] + c_shapes,
        scratch_shapes=[pltpu.VMEM((tm, SSM_WIDTH + POOL_WIDTH), BF16)] * 2
                       + [pltpu.VMEM((SSM_WIDTH // LANES, tm // CHUNK * PARK_PITCH, LANES), F32),
                          pltpu.VMEM((POOL_WIDTH // LANES, POOL_PAD + HALO + tm, LANES), F32),
                          pltpu.VMEM((2 * POOL_GROUP // LANES, POOL_PAD + HALO + tm, LANES), F32)],
        compiler_params=pltpu.CompilerParams(dimension_semantics=("arbitrary",),
                                             vmem_limit_bytes=VMEM_LIMIT_BYTES),
        name="out_proj",
    )(yg, v, v, x, w_glu, b_glu, w_pool, b_pool, pool_scale, w_out, w_out, *cast)


def _ffn_kernel(x_ref, gain_ref, wg_ref, wu_ref, wd_ref, gfin_ref, o_ref, h_ref, r_ref):
    j = pl.program_id(1)
    last = pl.num_programs(1) - 1
    lane_tiles = wg_ref.shape[1] // LANES

    def slab(first, final):
        if first:
            x = x_ref[...]
            h = (x * gain_ref[...]).astype(BF16)
            h_ref[...] = h
            r = lax.rsqrt(jnp.mean(x * x, axis=-1, keepdims=True) + EPS)
            r = jnp.broadcast_to(r, r_ref.shape)
            r_ref[...] = r
        else:
            h, r = h_ref[...], r_ref[...]
        rt = jnp.concatenate([r] * lane_tiles, axis=1)
        a = jnp.dot(h, wg_ref[...], preferred_element_type=F32) * rt
        b = jnp.dot(h, wu_ref[...], preferred_element_type=F32) * rt
        ff = (a * (1.0 / (1.0 + jnp.exp(-a))) * b).astype(BF16)
        acc = (x if first else o_ref[...]) + jnp.dot(ff, wd_ref[...], preferred_element_type=F32)
        o_ref[...] = _rms_norm(acc, gfin_ref[...]) if final else acc

    pl.when(j == 0)(lambda: slab(True, False))
    pl.when(jnp.logical_and(j > 0, j < last))(lambda: slab(False, False))
    pl.when(j == last)(lambda: slab(False, True))


def _ffn(x, gain, w_gate, w_up, w_down, gain_final, tm, tf):
    L = x.shape[0]
    d_ff = w_gate.shape[1]
    assert d_ff // tf >= 2
    return pl.pallas_call(
        _ffn_kernel,
        grid=(L // tm, d_ff // tf),
        in_specs=[pl.BlockSpec((tm, D_MODEL), lambda i, j: (i, 0)),
                  pl.BlockSpec((1, D_MODEL), lambda i, j: (0, 0)),
                  pl.BlockSpec((D_MODEL, tf), lambda i, j: (0, j)),
                  pl.BlockSpec((D_MODEL, tf), lambda i, j: (0, j)),
                  pl.BlockSpec((tf, D_MODEL), lambda i, j: (j, 0)),
                  pl.BlockSpec((1, D_MODEL), lambda i, j: (0, 0))],
        out_specs=pl.BlockSpec((tm, D_MODEL), lambda i, j: (i, 0)),
        out_shape=jax.ShapeDtypeStruct((L, D_MODEL), F32),
        scratch_shapes=[pltpu.VMEM((tm, D_MODEL), BF16), pltpu.VMEM((tm, LANES), F32)],
        compiler_params=pltpu.CompilerParams(dimension_semantics=("parallel", "arbitrary"),
                                             vmem_limit_bytes=VMEM_LIMIT_BYTES),
        name="ffn",
    )(x, gain, w_gate, w_up, w_down, gain_final)


def kernel(x, norm_mix, w_in, lambda_re, lambda_im, log_step, b_re, b_im, c_re, c_im, d_skip,
           w_glu, b_glu, w_pool, b_pool, pool_scale, w_out, norm_ffn, w_gate, w_up, w_down,
           norm_final):
    bsz, L, D = x.shape
    assert bsz == 1 and D == D_MODEL and w_in.shape[0] == 1 and L % 1024 == 0
    G, P = SSM_GROUPS, SSM_STATE
    xs = x.reshape(L, D).astype(F32)

    n_pool = len(POOL_WINDOWS)
    m, bw_pair, cw_pair, w_in_b, w_out_b, w_glu_b, w_pool_b = _ssm_prep(
        lambda_re[0].astype(F32), lambda_im[0].astype(F32), log_step[0].astype(F32),
        b_re[0].astype(F32), b_im[0].astype(F32), c_re[0].astype(F32), c_im[0].astype(F32),
        d_skip[0].astype(F32),
        cast=[w_in[0], w_out[0], w_glu[0], w_pool[0].reshape(n_pool * POOL_GROUP, POOL_GROUP)])
    lre_row = lambda_re[0].astype(F32).reshape(G // 2, 1, 2 * P)
    lim_row = lambda_im[0].astype(F32).reshape(G // 2, 1, 2 * P)
    ls_row = jnp.repeat(log_step[0].astype(F32), P).reshape(G // 2, 1, 2 * P)

    xg, v, w_gate_b = _in_proj(xs, norm_mix[0].astype(F32).reshape(1, D), w_in_b, cast=[w_gate[0]], tm=512)

    yg, w_down_b = _ssm(xg, m, bw_pair, cw_pair, lre_row, lim_row, ls_row, cast=[w_down[0]])

    x1, w_up_b = _out_proj(
        yg, v, xs, w_glu_b, b_glu[0].astype(F32).reshape(1, SSM_WIDTH),
        w_pool_b.reshape(n_pool, POOL_GROUP, POOL_GROUP), b_pool[0].astype(F32).reshape(1, POOL_WIDTH),
        pool_scale[0].astype(F32).reshape(1, POOL_WIDTH), w_out_b, cast=[w_up[0]], tm=512)

    out = _ffn(x1, norm_ffn[0].astype(F32).reshape(1, D), w_gate_b, w_up_b, w_down_b,
               norm_final.astype(F32).reshape(1, D), tm=1024, tf=512)
    return out.reshape(bsz, L, D).astype(x.dtype)
```

```python
import functools
import math

import jax
import jax.numpy as jnp
from jax import lax
from jax.experimental import pallas as pl
from jax.experimental.pallas import tpu as pltpu

F32 = jnp.float32
BF16 = jnp.bfloat16

D_MODEL = 2048
SSM_WIDTH = 1024
POOL_WIDTH = 1024
SSM_GROUP = 16
SSM_GROUPS = 64
SSM_STATE = 64
POOL_WINDOWS = (2, 4, 8, 16)
POOL_GROUP = 256
EPS = 1e-6

CHUNK = 16
CHUNK_LANES = CHUNK * SSM_GROUP
HALO = 16
POOL_PAD = 8
LANES = 128
GROUPS_PER_BLOCK = LANES // SSM_GROUP
RELAYOUT_ROWS = 16
RELAYOUT_UNITS = 4
SCAN_BLOCK = 8
PARK_PITCH = 20

VMEM_LIMIT_BYTES = 60 * 1024 * 1024

HIGHEST = lax.Precision.HIGHEST


def _rms_norm(x, gain):
    ms = jnp.mean(x * x, axis=-1, keepdims=True)
    return x * lax.rsqrt(ms + EPS) * gain


def _cast_specs(weights, n_steps):
    in_specs, out_specs, out_shapes = [], [], []
    for w in weights:
        rows, cols = w.shape
        slab = pl.BlockSpec((rows // n_steps, cols), lambda i: (jnp.minimum(i, n_steps - 1), 0))
        in_specs.append(slab)
        out_specs.append(slab)
        out_shapes.append(jax.ShapeDtypeStruct((rows, cols), BF16))
    return in_specs, out_specs, out_shapes


def _cast_slabs(src_refs, dst_refs):
    for src, dst in zip(src_refs, dst_refs):
        dst[...] = src[...].astype(dst.dtype)


def _ssm_prep_kernel(n_cast, lre_ref, lim_ref, ls_ref, b_re_ref, b_im_ref, c_re_ref, c_im_ref, d_ref, *refs):
    cast_src, (m_ref, bw_ref, cw_ref), cast_dst = refs[:n_cast], refs[n_cast:n_cast + 3], refs[n_cast + 3:]
    _cast_slabs(cast_src, cast_dst)
    P, H, T, GB = SSM_STATE, SSM_GROUP, CHUNK, GROUPS_PER_BLOCK
    lre, lim = lre_ref[...], lim_ref[...]
    step = jnp.exp(ls_ref[...])
    mag = jnp.exp(lre * step)
    lbr, lbi = mag * jnp.cos(lim * step), mag * jnp.sin(lim * step)
    den = lre * lre + lim * lim
    cfr = ((lbr - 1.0) * lre + lbi * lim) / den
    cfi = (lbi * lre - (lbr - 1.0) * lim) / den
    eye = (lax.broadcasted_iota(jnp.int32, (P, P), 0)
           == lax.broadcasted_iota(jnp.int32, (P, P), 1)).astype(F32)
    rows = jnp.concatenate([lbr, lbi, cfr, cfi], axis=0)
    cols = lax.dot_general(eye, rows, (((1,), (1,)), ((), ())), precision=HIGHEST,
                           preferred_element_type=F32)

    lane = lax.broadcasted_iota(jnp.int32, (1, CHUNK_LANES), 1)
    k_idx = lane // H
    sub = lax.broadcasted_iota(jnp.int32, (H, CHUNK_LANES), 0)
    lane_h = lax.broadcasted_iota(jnp.int32, (H, CHUNK_LANES), 1)
    rep = (lane_h % H == sub).astype(F32)

    def tile(a):
        return jnp.dot(a, rep, precision=HIGHEST, preferred_element_type=F32)

    def tile_t(a):
        return lax.dot_general(a, rep, (((0,), (0,)), ((), ())), precision=HIGHEST,
                               preferred_element_type=F32)

    def dot_t(a, b):
        return lax.dot_general(a, b, (((0,), (0,)), ((), ())), precision=HIGHEST,
                               preferred_element_type=F32)

    def cmul(ar, ai, br, bi):
        return ar * br - ai * bi, ar * bi + ai * br

    bw_ref[...] = jnp.zeros_like(bw_ref)
    cw_ref[...] = jnp.zeros_like(cw_ref)
    for g in range(GB):
        col = lambda quantity: cols[:, quantity * GB + g:quantity * GB + g + 1]
        l1r, l1i, cr, ci = col(0), col(1), col(2), col(3)
        sq = [(l1r, l1i)]
        for _ in range(3):
            sq.append(cmul(*sq[-1], *sq[-1]))
        pkr = pki = prr = pri = None
        for bit, (fr, fi) in enumerate(sq):
            on = ((k_idx >> bit) & 1) == 1
            ur, ui = jnp.where(on, fr, 1.0), jnp.where(on, fi, 0.0)
            dr, di = jnp.where(on, 1.0, fr), jnp.where(on, 0.0, fi)
            if bit == 0:
                pkr, pki, prr, pri = ur, ui, dr, di
            else:
                pkr, pki = cmul(pkr, pki, ur, ui)
                prr, pri = cmul(prr, pri, dr, di)

        bb_re, bb_im = cmul(cr, ci, b_re_ref[g], b_im_ref[g])

        q, j = g // 2, g % 2
        r_re = slice(j * P, (j + 1) * P)
        r_im = slice((2 + j) * P, (3 + j) * P)
        c_g = slice(j * CHUNK_LANES, (j + 1) * CHUNK_LANES)
        w_re, w_im = cmul(prr, pri, tile(bb_re), tile(bb_im))
        bw_ref[q, r_re, c_g] = w_re.astype(bw_ref.dtype)
        bw_ref[q, r_im, c_g] = w_im.astype(bw_ref.dtype)
        g_re, g_im = cmul(pkr, pki, tile_t(c_re_ref[g]), tile_t(c_im_ref[g]))
        g1_re, g1_im = cmul(l1r, l1i, g_re, g_im)
        cw_ref[q, r_re, c_g] = g1_re.astype(cw_ref.dtype)
        cw_ref[q, r_im, c_g] = (-g1_im).astype(cw_ref.dtype)
        kt = dot_t(bb_re, g_re) - dot_t(bb_im, g_im)
        kt = kt + jnp.where(lane_h == sub, d_ref[g], 0.0)
        for tau in range(T):
            blk = kt if tau == 0 else jnp.where(lane_h >= tau * H, pltpu.roll(kt, tau * H, axis=1), 0.0)
            m_ref[g, tau * H:(tau + 1) * H, :] = blk.astype(m_ref.dtype)


def _ssm_prep(lambda_re, lambda_im, log_step, b_re, b_im, c_re, c_im, d_skip, cast):
    G, P, H, GB = SSM_GROUPS, SSM_STATE, SSM_GROUP, GROUPS_PER_BLOCK
    n_steps = G // GB
    blk = lambda *shape: pl.BlockSpec(shape, lambda i: (i,) + (0,) * (len(shape) - 1))
    c_in, c_out, c_shapes = _cast_specs(cast, n_steps)
    return pl.pallas_call(
        functools.partial(_ssm_prep_kernel, len(cast)),
        grid=(n_steps,),
        in_specs=[blk(GB, P), blk(GB, P), blk(GB, 1),
                  blk(GB, P, H), blk(GB, P, H), blk(GB, H, P), blk(GB, H, P), blk(GB, H, 1)] + c_in,
        out_specs=[blk(GB, CHUNK_LANES, CHUNK_LANES), blk(GB // 2, 4 * P, 2 * CHUNK_LANES),
                   blk(GB // 2, 4 * P, 2 * CHUNK_LANES)] + c_out,
        out_shape=[jax.ShapeDtypeStruct((G, CHUNK_LANES, CHUNK_LANES), BF16),
                   jax.ShapeDtypeStruct((G // 2, 4 * P, 2 * CHUNK_LANES), BF16),
                   jax.ShapeDtypeStruct((G // 2, 4 * P, 2 * CHUNK_LANES), BF16)] + c_shapes,
        compiler_params=pltpu.CompilerParams(dimension_semantics=("parallel",),
                                             vmem_limit_bytes=VMEM_LIMIT_BYTES),
        name="ssm_prep",
    )(lambda_re, lambda_im, log_step.reshape(G, 1),
      b_re, b_im, c_re, c_im, d_skip.reshape(G, H, 1), *cast)


def _granule_transpose(units):
    lane = lax.broadcasted_iota(jnp.int32, units[0][0].shape, 1)
    units = [list(a) for a in units]
    for s in (4, 2, 1):
        upper = ((lane // SSM_GROUP) & s) != 0
        for a in units:
            for i in range(GROUPS_PER_BLOCK):
                if i & s:
                    continue
                lo, hi = a[i], a[i + s]
                a[i] = jnp.where(upper, pltpu.roll(hi, s * SSM_GROUP, axis=1), lo)
                a[i + s] = jnp.where(upper, hi, pltpu.roll(lo, LANES - s * SSM_GROUP, axis=1))
    return units


def _relayout_units(n_chunks):
    return [(b, cb, hf) for b in range(SSM_WIDTH // LANES)
            for cb in range(n_chunks // RELAYOUT_ROWS) for hf in range(2)]


def _relayout_to_groups(src_ref, dst_ref, n_chunks):
    half = CHUNK // 2
    rb = RELAYOUT_ROWS
    where = _relayout_units(n_chunks)
    batch = 2 * RELAYOUT_UNITS
    for i in range(0, len(where), batch):
        units = [[src_ref[b, pl.ds(cb * rb * PARK_PITCH + hf * half + t, rb, stride=PARK_PITCH), :].astype(BF16)
                  for t in range(half)] for (b, cb, hf) in where[i:i + batch]]
        units = _granule_transpose(units)
        for (b, cb, hf), unit in zip(where[i:i + batch], units):
            for g in range(GROUPS_PER_BLOCK):
                dst_ref[b * GROUPS_PER_BLOCK + g, cb * rb:(cb + 1) * rb, hf * LANES:(hf + 1) * LANES] = unit[g]


def _relayout_from_groups(src_ref, dst_ref, n_chunks, blocks=range(SSM_WIDTH // LANES)):
    half = CHUNK // 2
    rb = RELAYOUT_ROWS
    where = [u for u in _relayout_units(n_chunks) if u[0] in blocks]
    batch = 2 * RELAYOUT_UNITS
    for i in range(0, len(where), batch):
        units = [[src_ref[b * GROUPS_PER_BLOCK + g, cb * rb:(cb + 1) * rb, hf * LANES:(hf + 1) * LANES]
                  for g in range(GROUPS_PER_BLOCK)] for (b, cb, hf) in where[i:i + batch]]
        units = _granule_transpose(units)
        for (b, cb, hf), unit in zip(where[i:i + batch], units):
            for t in range(half):
                dst_ref[b, pl.ds(cb * rb * PARK_PITCH + hf * half + t, rb, stride=PARK_PITCH), :] = unit[t].astype(F32)


def _in_proj_kernel(n_cast, n, x_ref, gain_ref, w_ref, *refs):
    cast_src, (xg_ref, v_ref), cast_dst = refs[:n_cast], refs[n_cast:n_cast + 2], refs[n_cast + 2:-2]
    park = refs[-2:]
    s = pl.program_id(0)

    def step(p_prev, p_next):
        if p_prev is not None:
            _relayout_to_groups(p_prev, xg_ref, x_ref.shape[0] // CHUNK)
        if p_next is not None:
            _cast_slabs(cast_src, cast_dst)
            h = _rms_norm(x_ref[...], gain_ref[...]).astype(BF16)
            p = jnp.dot(h, w_ref[...], preferred_element_type=F32)
            v_ref[...] = p[:, SSM_WIDTH:].astype(v_ref.dtype)
            for b in range(SSM_WIDTH // LANES):
                for c in range(x_ref.shape[0] // CHUNK):
                    p_next[b, c * PARK_PITCH:c * PARK_PITCH + CHUNK, :] = (
                        p[c * CHUNK:(c + 1) * CHUNK, b * LANES:(b + 1) * LANES])

    middle = jnp.logical_and(s > 0, s < n)
    pl.when(s == 0)(lambda: step(None, park[0]))
    pl.when(jnp.logical_and(middle, s % 2 == 0))(lambda: step(park[1], park[0]))
    pl.when(jnp.logical_and(middle, s % 2 == 1))(lambda: step(park[0], park[1]))
    pl.when(s == n)(lambda: step(park[(n - 1) % 2], None))


def _in_proj(x, gain, w_in, cast, tm):
    L = x.shape[0]
    n = L // tm
    c_in, c_out, c_shapes = _cast_specs(cast, n)
    cur = lambda s: (jnp.minimum(s, n - 1), 0)
    return pl.pallas_call(
        functools.partial(_in_proj_kernel, len(cast), n),
        grid=(n + 1,),
        in_specs=[pl.BlockSpec((tm, D_MODEL), cur),
                  pl.BlockSpec((1, D_MODEL), lambda s: (0, 0)),
                  pl.BlockSpec((D_MODEL, D_MODEL), lambda s: (0, 0))] + c_in,
        out_specs=[pl.BlockSpec((SSM_GROUPS, tm // CHUNK, CHUNK_LANES), lambda s: (0, jnp.maximum(s - 1, 0), 0)),
                   pl.BlockSpec((tm, POOL_WIDTH), cur)] + c_out,
        out_shape=[jax.ShapeDtypeStruct((SSM_GROUPS, L // CHUNK, CHUNK_LANES), BF16),
                   jax.ShapeDtypeStruct((L, POOL_WIDTH), BF16)] + c_shapes,
        scratch_shapes=[pltpu.VMEM((SSM_WIDTH // LANES, tm // CHUNK * PARK_PITCH, LANES), F32)] * 2,
        compiler_params=pltpu.CompilerParams(dimension_semantics=("arbitrary",),
                                             vmem_limit_bytes=VMEM_LIMIT_BYTES),
        name="in_proj",
    )(x, gain, w_in, *cast)


def _cmul(ar, ai, br, bi):
    return ar * br - ai * bi, ar * bi + ai * br


def _ssm_kernel(n_cast, xg_ref, m_ref, bw_ref, cw_ref, lre_ref, lim_ref, ls_ref, *refs):
    cast_src, yg_ref, cast_dst = refs[:n_cast], refs[n_cast], refs[n_cast + 1:-4]
    fr_ref, fi_ref, cr_ref, ci_ref = refs[-4:]
    _cast_slabs(cast_src, cast_dst)
    n_chunks = xg_ref.shape[1]
    sb = SCAN_BLOCK
    n_sb = n_chunks // sb

    blk3 = (n_sb, sb, 2 * SSM_STATE)
    row_in = lax.broadcasted_iota(jnp.int32, blk3, 1)
    brow = lax.broadcasted_iota(jnp.int32, (n_sb, 2 * SSM_STATE), 0)
    for q in range(GROUPS_PER_BLOCK // 2):
        x0, x1 = xg_ref[2 * q], xg_ref[2 * q + 1]
        xp = jnp.concatenate([x0, x1], axis=1)
        e = lax.dot_general(xp, bw_ref[q], (((1,), (1,)), ((), ())), preferred_element_type=F32)
        step = jnp.exp(ls_ref[q])
        zr, zi = lre_ref[q] * step * CHUNK, lim_ref[q] * step * CHUNK
        mag = jnp.exp(zr)
        a1r, a1i = mag * jnp.cos(zi), mag * jnp.sin(zi)

        fr = e[:, :2 * SSM_STATE].reshape(blk3)
        fi = e[:, 2 * SSM_STATE:].reshape(blk3)
        ar, ai = a1r, a1i
        d = 1
        while d < sb:
            tr = jnp.where(row_in >= d, pltpu.roll(fr, d, axis=1), 0.0)
            ti = jnp.where(row_in >= d, pltpu.roll(fi, d, axis=1), 0.0)
            fr, fi = fr + ar * tr - ai * ti, fi + ar * ti + ai * tr
            ar, ai = _cmul(ar, ai, ar, ai)
            d *= 2
        fr_ref[...] = fr.reshape(n_chunks, 2 * SSM_STATE)
        fi_ref[...] = fi.reshape(n_chunks, 2 * SSM_STATE)
        gr = fr_ref[pl.ds(sb - 1, n_sb, stride=sb), :]
        gi = fi_ref[pl.ds(sb - 1, n_sb, stride=sb), :]
        gr = jnp.where(brow >= 1, pltpu.roll(gr, 1, axis=0), 0.0)
        gi = jnp.where(brow >= 1, pltpu.roll(gi, 1, axis=0), 0.0)
        d = 1
        while d < n_sb:
            tr = jnp.where(brow >= d, pltpu.roll(gr, d, axis=0), 0.0)
            ti = jnp.where(brow >= d, pltpu.roll(gi, d, axis=0), 0.0)
            gr, gi = gr + ar * tr - ai * ti, gi + ar * ti + ai * tr
            ar, ai = _cmul(ar, ai, ar, ai)
            d *= 2
        pr, pi = jnp.ones_like(a1r), jnp.zeros_like(a1i)
        for r in range(sb):
            cr_ref[pl.ds(r, n_sb, stride=sb), :] = pr * gr - pi * gi
            ci_ref[pl.ds(r, n_sb, stride=sb), :] = pr * gi + pi * gr
            pr, pi = _cmul(pr, pi, a1r, a1i)
        sr = jnp.where(row_in >= 1, pltpu.roll(fr, 1, axis=1), 0.0).reshape(n_chunks, 2 * SSM_STATE)
        si = jnp.where(row_in >= 1, pltpu.roll(fi, 1, axis=1), 0.0).reshape(n_chunks, 2 * SSM_STATE)
        sr = sr + cr_ref[...]
        si = si + ci_ref[...]
        s = jnp.concatenate([sr, si], axis=1).astype(BF16)
        cw = cw_ref[q]
        for j, xj in enumerate((x0, x1)):
            lhs = jnp.concatenate([xj, s], axis=1)
            rhs = jnp.concatenate([m_ref[2 * q + j], cw[:, j * CHUNK_LANES:(j + 1) * CHUNK_LANES]], axis=0)
            yg_ref[2 * q + j] = jnp.dot(lhs, rhs, preferred_element_type=F32).astype(yg_ref.dtype)


def _ssm(xg, m, bw_pair, cw_pair, lre_row, lim_row, ls_row, cast):
    n_chunks = xg.shape[1]
    n_blocks = SSM_WIDTH // LANES
    ppb = GROUPS_PER_BLOCK // 2
    c_in, c_out, c_shapes = _cast_specs(cast, n_blocks)
    return pl.pallas_call(
        functools.partial(_ssm_kernel, len(cast)),
        grid=(n_blocks,),
        in_specs=[pl.BlockSpec((GROUPS_PER_BLOCK, n_chunks, CHUNK_LANES), lambda b: (b, 0, 0)),
                  pl.BlockSpec((GROUPS_PER_BLOCK, CHUNK_LANES, CHUNK_LANES), lambda b: (b, 0, 0)),
                  pl.BlockSpec((ppb, 4 * SSM_STATE, 2 * CHUNK_LANES), lambda b: (b, 0, 0)),
                  pl.BlockSpec((ppb, 4 * SSM_STATE, 2 * CHUNK_LANES), lambda b: (b, 0, 0)),
                  pl.BlockSpec((ppb, 1, 2 * SSM_STATE), lambda b: (b, 0, 0)),
                  pl.BlockSpec((ppb, 1, 2 * SSM_STATE), lambda b: (b, 0, 0)),
                  pl.BlockSpec((ppb, 1, 2 * SSM_STATE), lambda b: (b, 0, 0))] + c_in,
        out_specs=[pl.BlockSpec((GROUPS_PER_BLOCK, n_chunks, CHUNK_LANES), lambda b: (b, 0, 0))] + c_out,
        out_shape=[jax.ShapeDtypeStruct(xg.shape, BF16)] + c_shapes,
        scratch_shapes=[pltpu.VMEM((n_chunks, 2 * SSM_STATE), F32)] * 4,
        compiler_params=pltpu.CompilerParams(dimension_semantics=("parallel",),
                                             vmem_limit_bytes=VMEM_LIMIT_BYTES),
        name="ssm",
    )(xg, m, bw_pair, cw_pair, lre_row, lim_row, ls_row, *cast)


def _out_proj_kernel(n_cast, yg_ref, v_ref, vh_ref, x_ref, wglu_ref, bglu_ref, wpool_ref, bpool_ref,
                     pscale_ref, wtop_ref, wbot_ref, *refs):
    cast_src, o_ref, cast_dst = refs[:n_cast], refs[n_cast], refs[n_cast + 1:-5]
    mix_park, y_ref, vbuf_ref, psum_ref = refs[-5:-3], refs[-3], refs[-2], refs[-1]
    _cast_slabs(cast_src, cast_dst)
    s = pl.program_id(0)
    tm = x_ref.shape[0]

    @pl.when(s == 0)
    def _():
        mix_park[1][...] = jnp.zeros_like(mix_park[1])

    def step(mix_prev, mix_next):
        n_tile = 2 * LANES

        def project(n):
            c = slice(n * n_tile, (n + 1) * n_tile)
            o_ref[:, c] = (x_ref[:, c]
                           + jnp.dot(mix_prev[:, :SSM_WIDTH], wtop_ref[:, c], preferred_element_type=F32)
                           + jnp.dot(mix_prev[:, SSM_WIDTH:], wbot_ref[:, c], preferred_element_type=F32))

        t0 = POOL_PAD + HALO
        n_lb = POOL_GROUP // LANES
        vh = jnp.where(s > 0, vh_ref[...].astype(F32), 0.0)
        vt = v_ref[...].astype(F32)
        for j in range(POOL_WIDTH // LANES):
            vbuf_ref[j, 0:POOL_PAD, :] = jnp.zeros((POOL_PAD, LANES), F32)
            vbuf_ref[j, POOL_PAD:t0, :] = vh[:, j * LANES:(j + 1) * LANES]
            vbuf_ref[j, t0:, :] = vt[:, j * LANES:(j + 1) * LANES]
        t1 = (lax.broadcasted_iota(jnp.int32, (tm, 1), 0) + (s * tm + 1)).astype(F32)

        def window_sum(slab, first, rows, terms, step):
            acc = slab[first:first + rows, :]
            for i in range(1, terms):
                acc = acc + slab[first - i * step:first - i * step + rows, :]
            return acc

        def pool_window(k):
            w = POOL_WINDOWS[k]
            out = []
            for j in range(k * n_lb, (k + 1) * n_lb):
                slab = vbuf_ref.at[j]
                if w <= 4:
                    total = window_sum(slab, t0, tm, w, 1)
                else:
                    part = psum_ref.at[j % (2 * n_lb)]
                    part[POOL_PAD:, :] = window_sum(slab, POOL_PAD, HALO + tm, w // 4, 1)
                    total = window_sum(part, t0, tm, 4, w // 4)
                mean = total / jnp.minimum(t1, float(w))
                out.append(mean - slab[t0:, :])
            return jnp.concatenate(out, axis=1).astype(BF16)

        def gelu(blocks):
            y = jnp.concatenate(
                [jnp.concatenate([y_ref[b, c * PARK_PITCH:c * PARK_PITCH + CHUNK, :] for c in range(tm // CHUNK)], axis=0)
                 for b in blocks], axis=1)
            return 0.5 * y * (1.0 + lax.erf(y * (1.0 / math.sqrt(2.0))))

        half_blocks = SSM_WIDTH // LANES // 2
        pk = [pool_window(0), pool_window(1)]
        project(0)
        pk.append(pool_window(2))
        project(1)
        pk.append(pool_window(3))
        project(2)
        pooled = [jnp.dot(pk[k], wpool_ref[k], preferred_element_type=F32) for k in range(len(POOL_WINDOWS))]
        yp = (jnp.concatenate(pooled, axis=1) + bpool_ref[...]) * pscale_ref[...]
        mix_next[:, SSM_WIDTH:] = yp.astype(BF16)
        _relayout_from_groups(yg_ref, y_ref, tm // CHUNK, blocks=range(0, half_blocks))
        project(3)
        _relayout_from_groups(yg_ref, y_ref, tm // CHUNK, blocks=range(half_blocks, 2 * half_blocks))
        project(4)
        g_a = gelu(range(0, half_blocks))
        project(5)
        g_b = gelu(range(half_blocks, 2 * half_blocks))
        project(6)
        g = jnp.concatenate([g_a, g_b], axis=1)
        z = jnp.dot(g.astype(BF16), wglu_ref[...], preferred_element_type=F32) + bglu_ref[...]
        mix_next[:, :SSM_WIDTH] = (g * (1.0 / (1.0 + jnp.exp(-z)))).astype(BF16)
        project(7)

    pl.when(s % 2 == 0)(lambda: step(mix_park[1], mix_park[0]))
    pl.when(s % 2 == 1)(lambda: step(mix_park[0], mix_park[1]))


def _out_proj(yg, v, x, w_glu, b_glu, w_pool, b_pool, pool_scale, w_out, cast, tm):
    L = x.shape[0]
    n = L // tm
    halo_blocks = tm // HALO
    const = lambda *shape: pl.BlockSpec(shape, lambda s: (0,) * len(shape))
    c_in, c_out, c_shapes = _cast_specs(cast, n)
    cur = lambda s: jnp.minimum(s, n - 1)
    prev = lambda s: (jnp.maximum(s - 1, 0), 0)
    return pl.pallas_call(
        functools.partial(_out_proj_kernel, len(cast)),
        grid=(n + 1,),
        in_specs=[pl.BlockSpec((SSM_GROUPS, tm // CHUNK, CHUNK_LANES), lambda s: (0, cur(s), 0)),
                  pl.BlockSpec((tm, POOL_WIDTH), lambda s: (cur(s), 0)),
                  pl.BlockSpec((HALO, POOL_WIDTH), lambda s: (jnp.maximum(cur(s) * halo_blocks - 1, 0), 0)),
                  pl.BlockSpec((tm, D_MODEL), prev),
                  const(SSM_WIDTH, SSM_WIDTH), const(1, SSM_WIDTH),
                  const(len(POOL_WINDOWS), POOL_GROUP, POOL_GROUP), const(1, POOL_WIDTH),
                  const(1, POOL_WIDTH),
                  pl.BlockSpec((SSM_WIDTH, D_MODEL), lambda s: (0, 0)),
                  pl.BlockSpec((POOL_WIDTH, D_MODEL), lambda s: (1, 0))] + c_in,
        out_specs=[pl.BlockSpec((tm, D_MODEL), prev)] + c_out,
        out_shape=[jax.ShapeDtypeStruct((L, D_MODEL), F32)] + c_shapes,
        scratch_shapes=[pltpu.VMEM((tm, SSM_WIDTH + POOL_WIDTH), BF16)] * 2
                       + [pltpu.VMEM((SSM_WIDTH // LANES, tm // CHUNK * PARK_PITCH, LANES), F32),
                          pltpu.VMEM((POOL_WIDTH // LANES, POOL_PAD + HALO + tm, LANES), F32),
                          pltpu.VMEM((2 * POOL_GROUP // LANES, POOL_PAD + HALO + tm, LANES), F32)],
        compiler_params=pltpu.CompilerParams(dimension_semantics=("arbitrary",),
                                             vmem_limit_bytes=VMEM_LIMIT_BYTES),
        name="out_proj",
    )(yg, v, v, x, w_glu, b_glu, w_pool, b_pool, pool_scale, w_out, w_out, *cast)


def _ffn_kernel(x_ref, gain_ref, wg_ref, wu_ref, wd_ref, gfin_ref, o_ref, h_ref, r_ref):
    j = pl.program_id(1)
    last = pl.num_programs(1) - 1
    lane_tiles = wg_ref.shape[1] // LANES

    def slab(first, final):
        if first:
            x = x_ref[...]
            h = (x * gain_ref[...]).astype(BF16)
            h_ref[...] = h
            r = lax.rsqrt(jnp.mean(x * x, axis=-1, keepdims=True) + EPS)
            r = jnp.broadcast_to(r, r_ref.shape)
            r_ref[...] = r
        else:
            h, r = h_ref[...], r_ref[...]
        rt = jnp.concatenate([r] * lane_tiles, axis=1)
        a = jnp.dot(h, wg_ref[...], preferred_element_type=F32) * rt
        b = jnp.dot(h, wu_ref[...], preferred_element_type=F32) * rt
        ff = (a * (1.0 / (1.0 + jnp.exp(-a))) * b).astype(BF16)
        acc = (x if first else o_ref[...]) + jnp.dot(ff, wd_ref[...], preferred_element_type=F32)
        o_ref[...] = _rms_norm(acc, gfin_ref[...]) if final else acc

    pl.when(j == 0)(lambda: slab(True, False))
    pl.when(jnp.logical_and(j > 0, j < last))(lambda: slab(False, False))
    pl.when(j == last)(lambda: slab(False, True))


def _ffn(x, gain, w_gate, w_up, w_down, gain_final, tm, tf):
    L = x.shape[0]
    d_ff = w_gate.shape[1]
    assert d_ff // tf >= 2
    return pl.pallas_call(
        _ffn_kernel,
        grid=(L // tm, d_ff // tf),
        in_specs=[pl.BlockSpec((tm, D_MODEL), lambda i, j: (i, 0)),
                  pl.BlockSpec((1, D_MODEL), lambda i, j: (0, 0)),
                  pl.BlockSpec((D_MODEL, tf), lambda i, j: (0, j)),
                  pl.BlockSpec((D_MODEL, tf), lambda i, j: (0, j)),
                  pl.BlockSpec((tf, D_MODEL), lambda i, j: (j, 0)),
                  pl.BlockSpec((1, D_MODEL), lambda i, j: (0, 0))],
        out_specs=pl.BlockSpec((tm, D_MODEL), lambda i, j: (i, 0)),
        out_shape=jax.ShapeDtypeStruct((L, D_MODEL), F32),
        scratch_shapes=[pltpu.VMEM((tm, D_MODEL), BF16), pltpu.VMEM((tm, LANES), F32)],
        compiler_params=pltpu.CompilerParams(dimension_semantics=("parallel", "arbitrary"),
                                             vmem_limit_bytes=VMEM_LIMIT_BYTES),
        name="ffn",
    )(x, gain, w_gate, w_up, w_down, gain_final)


def kernel(x, norm_mix, w_in, lambda_re, lambda_im, log_step, b_re, b_im, c_re, c_im, d_skip,
           w_glu, b_glu, w_pool, b_pool, pool_scale, w_out, norm_ffn, w_gate, w_up, w_down,
           norm_final):
    bsz, L, D = x.shape
    assert bsz == 1 and D == D_MODEL and w_in.shape[0] == 1 and L % 1024 == 0
    G, P = SSM_GROUPS, SSM_STATE
    xs = x.reshape(L, D).astype(F32)

    n_pool = len(POOL_WINDOWS)
    m, bw_pair, cw_pair, w_in_b, w_out_b, w_glu_b, w_pool_b = _ssm_prep(
        lambda_re[0].astype(F32), lambda_im[0].astype(F32), log_step[0].astype(F32),
        b_re[0].astype(F32), b_im[0].astype(F32), c_re[0].astype(F32), c_im[0].astype(F32),
        d_skip[0].astype(F32),
        cast=[w_in[0], w_out[0], w_glu[0], w_pool[0].reshape(n_pool * POOL_GROUP, POOL_GROUP)])
    lre_row = lambda_re[0].astype(F32).reshape(G // 2, 1, 2 * P)
    lim_row = lambda_im[0].astype(F32).reshape(G // 2, 1, 2 * P)
    ls_row = jnp.repeat(log_step[0].astype(F32), P).reshape(G // 2, 1, 2 * P)

    xg, v, w_gate_b = _in_proj(xs, norm_mix[0].astype(F32).reshape(1, D), w_in_b, cast=[w_gate[0]], tm=512)

    (yg,) = _ssm(xg, m, bw_pair, cw_pair, lre_row, lim_row, ls_row, cast=[])

    x1, w_up_b, w_down_b = _out_proj(
        yg, v, xs, w_glu_b, b_glu[0].astype(F32).reshape(1, SSM_WIDTH),
        w_pool_b.reshape(n_pool, POOL_GROUP, POOL_GROUP), b_pool[0].astype(F32).reshape(1, POOL_WIDTH),
        pool_scale[0].astype(F32).reshape(1, POOL_WIDTH), w_out_b, cast=[w_up[0], w_down[0]], tm=512)

    out = _ffn(x1, norm_ffn[0].astype(F32).reshape(1, D), w_gate_b, w_up_b, w_down_b,
               norm_final.astype(F32).reshape(1, D), tm=1024, tf=512)
    return out.reshape(bsz, L, D).astype(x.dtype)
```

```python
import functools
import math

import jax
import jax.numpy as jnp
from jax import lax
from jax.experimental import pallas as pl
from jax.experimental.pallas import tpu as pltpu

F32 = jnp.float32
BF16 = jnp.bfloat16

D_MODEL = 2048
SSM_WIDTH = 1024
POOL_WIDTH = 1024
SSM_GROUP = 16
SSM_GROUPS = 64
SSM_STATE = 64
POOL_WINDOWS = (2, 4, 8, 16)
POOL_GROUP = 256
EPS = 1e-6

CHUNK = 16
CHUNK_LANES = CHUNK * SSM_GROUP
HALO = 16
POOL_PAD = 8
LANES = 128
GROUPS_PER_BLOCK = LANES // SSM_GROUP
RELAYOUT_ROWS = 16
RELAYOUT_UNITS = 4
SCAN_BLOCK = 8
PARK_PITCH = 20

VMEM_LIMIT_BYTES = 60 * 1024 * 1024
PROJ_ROW_TILE = 512
FFN_ROW_TILE = 1024
FFN_FF_TILE = 512

HIGHEST = lax.Precision.HIGHEST


def _rms_norm(x, gain):
    ms = jnp.mean(x * x, axis=-1, keepdims=True)
    return x * lax.rsqrt(ms + EPS) * gain


def _cast_specs(weights, n_steps):
    in_specs, out_specs, out_shapes = [], [], []
    for w in weights:
        rows, cols = w.shape
        slab = pl.BlockSpec((rows // n_steps, cols), lambda i: (jnp.minimum(i, n_steps - 1), 0))
        in_specs.append(slab)
        out_specs.append(slab)
        out_shapes.append(jax.ShapeDtypeStruct((rows, cols), BF16))
    return in_specs, out_specs, out_shapes


def _cast_slabs(src_refs, dst_refs):
    for src, dst in zip(src_refs, dst_refs):
        dst[...] = src[...].astype(dst.dtype)


def _ssm_prep_kernel(n_cast, lre_ref, lim_ref, ls_ref, b_re_ref, b_im_ref, c_re_ref, c_im_ref, d_ref, *refs):
    cast_src, (m_ref, bw_ref, cw_ref), cast_dst = refs[:n_cast], refs[n_cast:n_cast + 3], refs[n_cast + 3:]
    _cast_slabs(cast_src, cast_dst)
    P, H, T, GB = SSM_STATE, SSM_GROUP, CHUNK, GROUPS_PER_BLOCK
    lre, lim = lre_ref[...], lim_ref[...]
    step = jnp.exp(ls_ref[...])
    mag = jnp.exp(lre * step)
    lbr, lbi = mag * jnp.cos(lim * step), mag * jnp.sin(lim * step)
    den = lre * lre + lim * lim
    cfr = ((lbr - 1.0) * lre + lbi * lim) / den
    cfi = (lbi * lre - (lbr - 1.0) * lim) / den
    eye = (lax.broadcasted_iota(jnp.int32, (P, P), 0)
           == lax.broadcasted_iota(jnp.int32, (P, P), 1)).astype(F32)
    rows = jnp.concatenate([lbr, lbi, cfr, cfi], axis=0)
    cols = lax.dot_general(eye, rows, (((1,), (1,)), ((), ())), precision=HIGHEST,
                           preferred_element_type=F32)

    lane = lax.broadcasted_iota(jnp.int32, (1, CHUNK_LANES), 1)
    k_idx = lane // H
    sub = lax.broadcasted_iota(jnp.int32, (H, CHUNK_LANES), 0)
    lane_h = lax.broadcasted_iota(jnp.int32, (H, CHUNK_LANES), 1)
    rep = (lane_h % H == sub).astype(F32)

    def tile(a):
        return pltpu.repeat(a, T, axis=1)

    def tile_t(a):
        return lax.dot_general(a, rep, (((0,), (0,)), ((), ())), precision=HIGHEST,
                               preferred_element_type=F32)

    def dot_t(a, b):
        return lax.dot_general(a, b, (((0,), (0,)), ((), ())), precision=HIGHEST,
                               preferred_element_type=F32)

    def cmul(ar, ai, br, bi):
        return ar * br - ai * bi, ar * bi + ai * br

    bw_ref[...] = jnp.zeros_like(bw_ref)
    cw_ref[...] = jnp.zeros_like(cw_ref)
    for g in range(GB):
        col = lambda quantity: cols[:, quantity * GB + g:quantity * GB + g + 1]
        l1r, l1i, cr, ci = col(0), col(1), col(2), col(3)
        sq = [(l1r, l1i)]
        for _ in range(3):
            sq.append(cmul(*sq[-1], *sq[-1]))
        pkr = pki = prr = pri = None
        for bit, (fr, fi) in enumerate(sq):
            on = ((k_idx >> bit) & 1) == 1
            ur, ui = jnp.where(on, fr, 1.0), jnp.where(on, fi, 0.0)
            dr, di = jnp.where(on, 1.0, fr), jnp.where(on, 0.0, fi)
            if bit == 0:
                pkr, pki, prr, pri = ur, ui, dr, di
            else:
                pkr, pki = cmul(pkr, pki, ur, ui)
                prr, pri = cmul(prr, pri, dr, di)

        bb_re, bb_im = cmul(cr, ci, b_re_ref[g], b_im_ref[g])

        q, j = g // 2, g % 2
        r_re = slice(j * P, (j + 1) * P)
        r_im = slice((2 + j) * P, (3 + j) * P)
        c_g = slice(j * CHUNK_LANES, (j + 1) * CHUNK_LANES)
        w_re, w_im = cmul(prr, pri, tile(bb_re), tile(bb_im))
        bw_ref[q, r_re, c_g] = w_re.astype(bw_ref.dtype)
        bw_ref[q, r_im, c_g] = w_im.astype(bw_ref.dtype)
        g_re, g_im = cmul(pkr, pki, tile_t(c_re_ref[g]), tile_t(c_im_ref[g]))
        g1_re, g1_im = cmul(l1r, l1i, g_re, g_im)
        cw_ref[q, r_re, c_g] = g1_re.astype(cw_ref.dtype)
        cw_ref[q, r_im, c_g] = (-g1_im).astype(cw_ref.dtype)
        kt = dot_t(bb_re, g_re) - dot_t(bb_im, g_im)
        kt = kt + jnp.where(lane_h == sub, d_ref[g], 0.0)
        for tau in range(T):
            blk = kt if tau == 0 else jnp.where(lane_h >= tau * H, pltpu.roll(kt, tau * H, axis=1), 0.0)
            m_ref[g, tau * H:(tau + 1) * H, :] = blk.astype(m_ref.dtype)


def _ssm_prep(lambda_re, lambda_im, log_step, b_re, b_im, c_re, c_im, d_skip, cast):
    G, P, H, GB = SSM_GROUPS, SSM_STATE, SSM_GROUP, GROUPS_PER_BLOCK
    n_steps = G // GB
    blk = lambda *shape: pl.BlockSpec(shape, lambda i: (i,) + (0,) * (len(shape) - 1))
    c_in, c_out, c_shapes = _cast_specs(cast, n_steps)
    return pl.pallas_call(
        functools.partial(_ssm_prep_kernel, len(cast)),
        grid=(n_steps,),
        in_specs=[blk(GB, P), blk(GB, P), blk(GB, 1),
                  blk(GB, P, H), blk(GB, P, H), blk(GB, H, P), blk(GB, H, P), blk(GB, H, 1)] + c_in,
        out_specs=[blk(GB, CHUNK_LANES, CHUNK_LANES), blk(GB // 2, 4 * P, 2 * CHUNK_LANES),
                   blk(GB // 2, 4 * P, 2 * CHUNK_LANES)] + c_out,
        out_shape=[jax.ShapeDtypeStruct((G, CHUNK_LANES, CHUNK_LANES), BF16),
                   jax.ShapeDtypeStruct((G // 2, 4 * P, 2 * CHUNK_LANES), BF16),
                   jax.ShapeDtypeStruct((G // 2, 4 * P, 2 * CHUNK_LANES), BF16)] + c_shapes,
        compiler_params=pltpu.CompilerParams(dimension_semantics=("parallel",),
                                             vmem_limit_bytes=VMEM_LIMIT_BYTES),
        name="ssm_prep",
    )(lambda_re, lambda_im, log_step.reshape(G, 1),
      b_re, b_im, c_re, c_im, d_skip.reshape(G, H, 1), *cast)


def _granule_transpose(units):
    lane = lax.broadcasted_iota(jnp.int32, units[0][0].shape, 1)
    units = [list(a) for a in units]
    for s in (4, 2, 1):
        upper = ((lane // SSM_GROUP) & s) != 0
        for a in units:
            for i in range(GROUPS_PER_BLOCK):
                if i & s:
                    continue
                lo, hi = a[i], a[i + s]
                a[i] = jnp.where(upper, pltpu.roll(hi, s * SSM_GROUP, axis=1), lo)
                a[i + s] = jnp.where(upper, hi, pltpu.roll(lo, LANES - s * SSM_GROUP, axis=1))
    return units


def _relayout_units(n_chunks):
    return [(b, cb, hf) for b in range(SSM_WIDTH // LANES)
            for cb in range(n_chunks // RELAYOUT_ROWS) for hf in range(2)]


def _relayout_to_groups(src_ref, dst_ref, n_chunks):
    half = CHUNK // 2
    rb = RELAYOUT_ROWS
    where = _relayout_units(n_chunks)
    batch = 2 * RELAYOUT_UNITS
    for i in range(0, len(where), batch):
        units = [[src_ref[b, pl.ds(cb * rb * PARK_PITCH + hf * half + t, rb, stride=PARK_PITCH), :].astype(BF16)
                  for t in range(half)] for (b, cb, hf) in where[i:i + batch]]
        units = _granule_transpose(units)
        for (b, cb, hf), unit in zip(where[i:i + batch], units):
            for g in range(GROUPS_PER_BLOCK):
                dst_ref[b * GROUPS_PER_BLOCK + g, cb * rb:(cb + 1) * rb, hf * LANES:(hf + 1) * LANES] = unit[g]


def _relayout_from_groups(src_ref, dst_ref, n_chunks, blocks=range(SSM_WIDTH // LANES)):
    half = CHUNK // 2
    rb = RELAYOUT_ROWS
    where = [u for u in _relayout_units(n_chunks) if u[0] in blocks]
    batch = 2 * RELAYOUT_UNITS
    for i in range(0, len(where), batch):
        units = [[src_ref[b * GROUPS_PER_BLOCK + g, cb * rb:(cb + 1) * rb, hf * LANES:(hf + 1) * LANES]
                  for g in range(GROUPS_PER_BLOCK)] for (b, cb, hf) in where[i:i + batch]]
        units = _granule_transpose(units)
        for (b, cb, hf), unit in zip(where[i:i + batch], units):
            for t in range(half):
                dst_ref[b, pl.ds(cb * rb * PARK_PITCH + hf * half + t, rb, stride=PARK_PITCH), :] = unit[t].astype(F32)


def _in_proj_kernel(n_cast, n, x_ref, gain_ref, w_ref, *refs):
    cast_src, (xg_ref, v_ref), cast_dst = refs[:n_cast], refs[n_cast:n_cast + 2], refs[n_cast + 2:-2]
    park = refs[-2:]
    s = pl.program_id(0)

    def step(p_prev, p_next):
        if p_prev is not None:
            _relayout_to_groups(p_prev, xg_ref, x_ref.shape[0] // CHUNK)
        if p_next is not None:
            _cast_slabs(cast_src, cast_dst)
            h = _rms_norm(x_ref[...], gain_ref[...]).astype(BF16)
            p = jnp.dot(h, w_ref[...], preferred_element_type=F32)
            v_ref[...] = p[:, SSM_WIDTH:].astype(v_ref.dtype)
            for b in range(SSM_WIDTH // LANES):
                for c in range(x_ref.shape[0] // CHUNK):
                    p_next[b, c * PARK_PITCH:c * PARK_PITCH + CHUNK, :] = (
                        p[c * CHUNK:(c + 1) * CHUNK, b * LANES:(b + 1) * LANES])

    middle = jnp.logical_and(s > 0, s < n)
    pl.when(s == 0)(lambda: step(None, park[0]))
    pl.when(jnp.logical_and(middle, s % 2 == 0))(lambda: step(park[1], park[0]))
    pl.when(jnp.logical_and(middle, s % 2 == 1))(lambda: step(park[0], park[1]))
    pl.when(s == n)(lambda: step(park[(n - 1) % 2], None))


def _in_proj(x, gain, w_in, cast, tm):
    L = x.shape[0]
    n = L // tm
    c_in, c_out, c_shapes = _cast_specs(cast, n)
    cur = lambda s: (jnp.minimum(s, n - 1), 0)
    return pl.pallas_call(
        functools.partial(_in_proj_kernel, len(cast), n),
        grid=(n + 1,),
        in_specs=[pl.BlockSpec((tm, D_MODEL), cur),
                  pl.BlockSpec((1, D_MODEL), lambda s: (0, 0)),
                  pl.BlockSpec((D_MODEL, D_MODEL), lambda s: (0, 0))] + c_in,
        out_specs=[pl.BlockSpec((SSM_GROUPS, tm // CHUNK, CHUNK_LANES), lambda s: (0, jnp.maximum(s - 1, 0), 0)),
                   pl.BlockSpec((tm, POOL_WIDTH), cur)] + c_out,
        out_shape=[jax.ShapeDtypeStruct((SSM_GROUPS, L // CHUNK, CHUNK_LANES), BF16),
                   jax.ShapeDtypeStruct((L, POOL_WIDTH), BF16)] + c_shapes,
        scratch_shapes=[pltpu.VMEM((SSM_WIDTH // LANES, tm // CHUNK * PARK_PITCH, LANES), F32)] * 2,
        compiler_params=pltpu.CompilerParams(dimension_semantics=("arbitrary",),
                                             vmem_limit_bytes=VMEM_LIMIT_BYTES),
        name="in_proj",
    )(x, gain, w_in, *cast)


def _cmul(ar, ai, br, bi):
    return ar * br - ai * bi, ar * bi + ai * br


def _ssm_kernel(n_cast, xg_ref, m_ref, bw_ref, cw_ref, lre_ref, lim_ref, ls_ref, *refs):
    cast_src, yg_ref, cast_dst = refs[:n_cast], refs[n_cast], refs[n_cast + 1:-4]
    fr_ref, fi_ref, cr_ref, ci_ref = refs[-4:]
    _cast_slabs(cast_src, cast_dst)
    n_chunks = xg_ref.shape[1]
    sb = SCAN_BLOCK
    n_sb = n_chunks // sb

    blk3 = (n_sb, sb, 2 * SSM_STATE)
    row_in = lax.broadcasted_iota(jnp.int32, blk3, 1)
    brow = lax.broadcasted_iota(jnp.int32, (n_sb, 2 * SSM_STATE), 0)
    for q in range(GROUPS_PER_BLOCK // 2):
        x0, x1 = xg_ref[2 * q], xg_ref[2 * q + 1]
        xp = jnp.concatenate([x0, x1], axis=1)
        e = lax.dot_general(xp, bw_ref[q], (((1,), (1,)), ((), ())), preferred_element_type=F32)
        step = jnp.exp(ls_ref[q])
        zr, zi = lre_ref[q] * step * CHUNK, lim_ref[q] * step * CHUNK
        mag = jnp.exp(zr)
        a1r, a1i = mag * jnp.cos(zi), mag * jnp.sin(zi)

        fr = e[:, :2 * SSM_STATE].reshape(blk3)
        fi = e[:, 2 * SSM_STATE:].reshape(blk3)
        ar, ai = a1r, a1i
        d = 1
        while d < sb:
            tr = jnp.where(row_in >= d, pltpu.roll(fr, d, axis=1), 0.0)
            ti = jnp.where(row_in >= d, pltpu.roll(fi, d, axis=1), 0.0)
            fr, fi = fr + ar * tr - ai * ti, fi + ar * ti + ai * tr
            ar, ai = _cmul(ar, ai, ar, ai)
            d *= 2
        fr_ref[...] = fr.reshape(n_chunks, 2 * SSM_STATE)
        fi_ref[...] = fi.reshape(n_chunks, 2 * SSM_STATE)
        gr = fr_ref[pl.ds(sb - 1, n_sb, stride=sb), :]
        gi = fi_ref[pl.ds(sb - 1, n_sb, stride=sb), :]
        gr = jnp.where(brow >= 1, pltpu.roll(gr, 1, axis=0), 0.0)
        gi = jnp.where(brow >= 1, pltpu.roll(gi, 1, axis=0), 0.0)
        d = 1
        while d < n_sb:
            tr = jnp.where(brow >= d, pltpu.roll(gr, d, axis=0), 0.0)
            ti = jnp.where(brow >= d, pltpu.roll(gi, d, axis=0), 0.0)
            gr, gi = gr + ar * tr - ai * ti, gi + ar * ti + ai * tr
            ar, ai = _cmul(ar, ai, ar, ai)
            d *= 2
        pr, pi = jnp.ones_like(a1r), jnp.zeros_like(a1i)
        for r in range(sb):
            cr_ref[pl.ds(r, n_sb, stride=sb), :] = pr * gr - pi * gi
            ci_ref[pl.ds(r, n_sb, stride=sb), :] = pr * gi + pi * gr
            pr, pi = _cmul(pr, pi, a1r, a1i)
        sr = jnp.where(row_in >= 1, pltpu.roll(fr, 1, axis=1), 0.0).reshape(n_chunks, 2 * SSM_STATE)
        si = jnp.where(row_in >= 1, pltpu.roll(fi, 1, axis=1), 0.0).reshape(n_chunks, 2 * SSM_STATE)
        sr = sr + cr_ref[...]
        si = si + ci_ref[...]
        s = jnp.concatenate([sr, si], axis=1).astype(BF16)
        cw = cw_ref[q]
        for j, xj in enumerate((x0, x1)):
            lhs = jnp.concatenate([xj, s], axis=1)
            rhs = jnp.concatenate([m_ref[2 * q + j], cw[:, j * CHUNK_LANES:(j + 1) * CHUNK_LANES]], axis=0)
            yg_ref[2 * q + j] = jnp.dot(lhs, rhs, preferred_element_type=F32).astype(yg_ref.dtype)


def _ssm(xg, m, bw_pair, cw_pair, lre_row, lim_row, ls_row, cast):
    n_chunks = xg.shape[1]
    n_blocks = SSM_WIDTH // LANES
    ppb = GROUPS_PER_BLOCK // 2
    c_in, c_out, c_shapes = _cast_specs(cast, n_blocks)
    return pl.pallas_call(
        functools.partial(_ssm_kernel, len(cast)),
        grid=(n_blocks,),
        in_specs=[pl.BlockSpec((GROUPS_PER_BLOCK, n_chunks, CHUNK_LANES), lambda b: (b, 0, 0)),
                  pl.BlockSpec((GROUPS_PER_BLOCK, CHUNK_LANES, CHUNK_LANES), lambda b: (b, 0, 0)),
                  pl.BlockSpec((ppb, 4 * SSM_STATE, 2 * CHUNK_LANES), lambda b: (b, 0, 0)),
                  pl.BlockSpec((ppb, 4 * SSM_STATE, 2 * CHUNK_LANES), lambda b: (b, 0, 0)),
                  pl.BlockSpec((ppb, 1, 2 * SSM_STATE), lambda b: (b, 0, 0)),
                  pl.BlockSpec((ppb, 1, 2 * SSM_STATE), lambda b: (b, 0, 0)),
                  pl.BlockSpec((ppb, 1, 2 * SSM_STATE), lambda b: (b, 0, 0))] + c_in,
        out_specs=[pl.BlockSpec((GROUPS_PER_BLOCK, n_chunks, CHUNK_LANES), lambda b: (b, 0, 0))] + c_out,
        out_shape=[jax.ShapeDtypeStruct(xg.shape, BF16)] + c_shapes,
        scratch_shapes=[pltpu.VMEM((n_chunks, 2 * SSM_STATE), F32)] * 4,
        compiler_params=pltpu.CompilerParams(dimension_semantics=("parallel",),
                                             vmem_limit_bytes=VMEM_LIMIT_BYTES),
        name="ssm",
    )(xg, m, bw_pair, cw_pair, lre_row, lim_row, ls_row, *cast)


def _out_proj_kernel(n_cast, yg_ref, v_ref, vh_ref, x_ref, wglu_ref, bglu_ref, wpool_ref, bpool_ref,
                     pscale_ref, wtop_ref, wbot_ref, *refs):
    cast_src, o_ref, cast_dst = refs[:n_cast], refs[n_cast], refs[n_cast + 1:-5]
    mix_park, y_ref, vbuf_ref, psum_ref = refs[-5:-3], refs[-3], refs[-2], refs[-1]
    _cast_slabs(cast_src, cast_dst)
    s = pl.program_id(0)
    tm = x_ref.shape[0]

    @pl.when(s == 0)
    def _():
        mix_park[1][...] = jnp.zeros_like(mix_park[1])

    def step(mix_prev, mix_next):
        n_tile = 2 * LANES

        def project(n):
            c = slice(n * n_tile, (n + 1) * n_tile)
            o_ref[:, c] = (x_ref[:, c]
                           + jnp.dot(mix_prev[:, :SSM_WIDTH], wtop_ref[:, c], preferred_element_type=F32)
                           + jnp.dot(mix_prev[:, SSM_WIDTH:], wbot_ref[:, c], preferred_element_type=F32))

        t0 = POOL_PAD + HALO
        n_lb = POOL_GROUP // LANES
        vh = jnp.where(s > 0, vh_ref[...].astype(F32), 0.0)
        vt = v_ref[...].astype(F32)
        for j in range(POOL_WIDTH // LANES):
            vbuf_ref[j, 0:POOL_PAD, :] = jnp.zeros((POOL_PAD, LANES), F32)
            vbuf_ref[j, POOL_PAD:t0, :] = vh[:, j * LANES:(j + 1) * LANES]
            vbuf_ref[j, t0:, :] = vt[:, j * LANES:(j + 1) * LANES]
        t1 = (lax.broadcasted_iota(jnp.int32, (tm, 1), 0) + (s * tm + 1)).astype(F32)

        def window_sum(slab, first, rows, terms, step):
            acc = slab[first:first + rows, :]
            for i in range(1, terms):
                acc = acc + slab[first - i * step:first - i * step + rows, :]
            return acc

        def pool_window(k):
            w = POOL_WINDOWS[k]
            out = []
            for j in range(k * n_lb, (k + 1) * n_lb):
                slab = vbuf_ref.at[j]
                if w <= 4:
                    total = window_sum(slab, t0, tm, w, 1)
                else:
                    part = psum_ref.at[j % (2 * n_lb)]
                    part[POOL_PAD:, :] = window_sum(slab, POOL_PAD, HALO + tm, w // 4, 1)
                    total = window_sum(part, t0, tm, 4, w // 4)
                mean = total / jnp.minimum(t1, float(w))
                out.append(mean - slab[t0:, :])
            return jnp.concatenate(out, axis=1).astype(BF16)

        def gelu(blocks):
            y = jnp.concatenate(
                [jnp.concatenate([y_ref[b, c * PARK_PITCH:c * PARK_PITCH + CHUNK, :] for c in range(tm // CHUNK)], axis=0)
                 for b in blocks], axis=1)
            return 0.5 * y * (1.0 + lax.erf(y * (1.0 / math.sqrt(2.0))))

        half_blocks = SSM_WIDTH // LANES // 2
        pk = [pool_window(0), pool_window(1)]
        project(0)
        pk.append(pool_window(2))
        project(1)
        pk.append(pool_window(3))
        project(2)
        pooled = [jnp.dot(pk[k], wpool_ref[k], preferred_element_type=F32) for k in range(len(POOL_WINDOWS))]
        yp = (jnp.concatenate(pooled, axis=1) + bpool_ref[...]) * pscale_ref[...]
        mix_next[:, SSM_WIDTH:] = yp.astype(BF16)
        _relayout_from_groups(yg_ref, y_ref, tm // CHUNK, blocks=range(0, half_blocks))
        project(3)
        _relayout_from_groups(yg_ref, y_ref, tm // CHUNK, blocks=range(half_blocks, 2 * half_blocks))
        project(4)
        g_a = gelu(range(0, half_blocks))
        project(5)
        g_b = gelu(range(half_blocks, 2 * half_blocks))
        project(6)
        g = jnp.concatenate([g_a, g_b], axis=1)
        z = jnp.dot(g.astype(BF16), wglu_ref[...], preferred_element_type=F32) + bglu_ref[...]
        mix_next[:, :SSM_WIDTH] = (g * (1.0 / (1.0 + jnp.exp(-z)))).astype(BF16)
        project(7)

    pl.when(s % 2 == 0)(lambda: step(mix_park[1], mix_park[0]))
    pl.when(s % 2 == 1)(lambda: step(mix_park[0], mix_park[1]))


def _out_proj(yg, v, x, w_glu, b_glu, w_pool, b_pool, pool_scale, w_out, cast, tm):
    L = x.shape[0]
    n = L // tm
    halo_blocks = tm // HALO
    const = lambda *shape: pl.BlockSpec(shape, lambda s: (0,) * len(shape))
    c_in, c_out, c_shapes = _cast_specs(cast, n)
    cur = lambda s: jnp.minimum(s, n - 1)
    prev = lambda s: (jnp.maximum(s - 1, 0), 0)
    return pl.pallas_call(
        functools.partial(_out_proj_kernel, len(cast)),
        grid=(n + 1,),
        in_specs=[pl.BlockSpec((SSM_GROUPS, tm // CHUNK, CHUNK_LANES), lambda s: (0, cur(s), 0)),
                  pl.BlockSpec((tm, POOL_WIDTH), lambda s: (cur(s), 0)),
                  pl.BlockSpec((HALO, POOL_WIDTH), lambda s: (jnp.maximum(cur(s) * halo_blocks - 1, 0), 0)),
                  pl.BlockSpec((tm, D_MODEL), prev),
                  const(SSM_WIDTH, SSM_WIDTH), const(1, SSM_WIDTH),
                  const(len(POOL_WINDOWS), POOL_GROUP, POOL_GROUP), const(1, POOL_WIDTH),
                  const(1, POOL_WIDTH),
                  pl.BlockSpec((SSM_WIDTH, D_MODEL), lambda s: (0, 0)),
                  pl.BlockSpec((POOL_WIDTH, D_MODEL), lambda s: (1, 0))] + c_in,
        out_specs=[pl.BlockSpec((tm, D_MODEL), prev)] + c_out,
        out_shape=[jax.ShapeDtypeStruct((L, D_MODEL), F32)] + c_shapes,
        scratch_shapes=[pltpu.VMEM((tm, SSM_WIDTH + POOL_WIDTH), BF16)] * 2
                       + [pltpu.VMEM((SSM_WIDTH // LANES, tm // CHUNK * PARK_PITCH, LANES), F32),
                          pltpu.VMEM((POOL_WIDTH // LANES, POOL_PAD + HALO + tm, LANES), F32),
                          pltpu.VMEM((2 * POOL_GROUP // LANES, POOL_PAD + HALO + tm, LANES), F32)],
        compiler_params=pltpu.CompilerParams(dimension_semantics=("arbitrary",),
                                             vmem_limit_bytes=VMEM_LIMIT_BYTES),
        name="out_proj",
    )(yg, v, v, x, w_glu, b_glu, w_pool, b_pool, pool_scale, w_out, w_out, *cast)


def _ffn_kernel(x_ref, gain_ref, wg_ref, wu_ref, wd_ref, gfin_ref, o_ref, h_ref, r_ref):
    j = pl.program_id(1)
    last = pl.num_programs(1) - 1
    lane_tiles = wg_ref.shape[1] // LANES

    def slab(first, final):
        if first:
            x = x_ref[...]
            h = (x * gain_ref[...]).astype(BF16)
            h_ref[...] = h
            r = lax.rsqrt(jnp.mean(x * x, axis=-1, keepdims=True) + EPS)
            r = jnp.broadcast_to(r, r_ref.shape)
            r_ref[...] = r
        else:
            h, r = h_ref[...], r_ref[...]
        rt = jnp.concatenate([r] * lane_tiles, axis=1)
        a = jnp.dot(h, wg_ref[...], preferred_element_type=F32) * rt
        b = jnp.dot(h, wu_ref[...], preferred_element_type=F32) * rt
        ff = (a * (1.0 / (1.0 + jnp.exp(-a))) * b).astype(BF16)
        acc = (x if first else o_ref[...]) + jnp.dot(ff, wd_ref[...], preferred_element_type=F32)
        o_ref[...] = _rms_norm(acc, gfin_ref[...]) if final else acc

    pl.when(j == 0)(lambda: slab(True, False))
    pl.when(jnp.logical_and(j > 0, j < last))(lambda: slab(False, False))
    pl.when(j == last)(lambda: slab(False, True))


def _ffn(x, gain, w_gate, w_up, w_down, gain_final, tm, tf):
    L = x.shape[0]
    d_ff = w_gate.shape[1]
    assert d_ff // tf >= 2
    return pl.pallas_call(
        _ffn_kernel,
        grid=(L // tm, d_ff // tf),
        in_specs=[pl.BlockSpec((tm, D_MODEL), lambda i, j: (i, 0)),
                  pl.BlockSpec((1, D_MODEL), lambda i, j: (0, 0)),
                  pl.BlockSpec((D_MODEL, tf), lambda i, j: (0, j)),
                  pl.BlockSpec((D_MODEL, tf), lambda i, j: (0, j)),
                  pl.BlockSpec((tf, D_MODEL), lambda i, j: (j, 0)),
                  pl.BlockSpec((1, D_MODEL), lambda i, j: (0, 0))],
        out_specs=pl.BlockSpec((tm, D_MODEL), lambda i, j: (i, 0)),
        out_shape=jax.ShapeDtypeStruct((L, D_MODEL), F32),
        scratch_shapes=[pltpu.VMEM((tm, D_MODEL), BF16), pltpu.VMEM((tm, LANES), F32)],
        compiler_params=pltpu.CompilerParams(dimension_semantics=("parallel", "arbitrary"),
                                             vmem_limit_bytes=VMEM_LIMIT_BYTES),
        name="ffn",
    )(x, gain, w_gate, w_up, w_down, gain_final)


def kernel(x, norm_mix, w_in, lambda_re, lambda_im, log_step, b_re, b_im, c_re, c_im, d_skip,
           w_glu, b_glu, w_pool, b_pool, pool_scale, w_out, norm_ffn, w_gate, w_up, w_down,
           norm_final):
    bsz, L, D = x.shape
    assert bsz == 1 and D == D_MODEL and w_in.shape[0] == 1 and L % 1024 == 0
    G, P = SSM_GROUPS, SSM_STATE
    xs = x.reshape(L, D).astype(F32)

    n_pool = len(POOL_WINDOWS)
    m, bw_pair, cw_pair, w_in_b, w_out_b, w_glu_b, w_pool_b = _ssm_prep(
        lambda_re[0].astype(F32), lambda_im[0].astype(F32), log_step[0].astype(F32),
        b_re[0].astype(F32), b_im[0].astype(F32), c_re[0].astype(F32), c_im[0].astype(F32),
        d_skip[0].astype(F32),
        cast=[w_in[0], w_out[0], w_glu[0], w_pool[0].reshape(n_pool * POOL_GROUP, POOL_GROUP)])
    lre_row = lambda_re[0].astype(F32).reshape(G // 2, 1, 2 * P)
    lim_row = lambda_im[0].astype(F32).reshape(G // 2, 1, 2 * P)
    ls_row = jnp.repeat(log_step[0].astype(F32), P).reshape(G // 2, 1, 2 * P)

    xg, v, w_gate_b = _in_proj(xs, norm_mix[0].astype(F32).reshape(1, D), w_in_b, cast=[w_gate[0]],
                               tm=PROJ_ROW_TILE)

    (yg,) = _ssm(xg, m, bw_pair, cw_pair, lre_row, lim_row, ls_row, cast=[])

    x1, w_up_b, w_down_b = _out_proj(
        yg, v, xs, w_glu_b, b_glu[0].astype(F32).reshape(1, SSM_WIDTH),
        w_pool_b.reshape(n_pool, POOL_GROUP, POOL_GROUP), b_pool[0].astype(F32).reshape(1, POOL_WIDTH),
        pool_scale[0].astype(F32).reshape(1, POOL_WIDTH), w_out_b, cast=[w_up[0], w_down[0]],
        tm=PROJ_ROW_TILE)

    out = _ffn(x1, norm_ffn[0].astype(F32).reshape(1, D), w_gate_b, w_up_b, w_down_b,
               norm_final.astype(F32).reshape(1, D), tm=FFN_ROW_TILE, tf=FFN_FF_TILE)
    return out.reshape(bsz, L, D).astype(x.dtype)
```

```python
import functools
import math

import jax
import jax.numpy as jnp
from jax import lax
from jax.experimental import pallas as pl
from jax.experimental.pallas import tpu as pltpu

F32 = jnp.float32
BF16 = jnp.bfloat16

D_MODEL = 2048
SSM_WIDTH = 1024
POOL_WIDTH = 1024
SSM_GROUP = 16
SSM_GROUPS = 64
SSM_STATE = 64
POOL_WINDOWS = (2, 4, 8, 16)
POOL_GROUP = 256
EPS = 1e-6

CHUNK = 16
CHUNK_LANES = CHUNK * SSM_GROUP
HALO = 16
POOL_PAD = 8
LANES = 128
GROUPS_PER_BLOCK = LANES // SSM_GROUP
RELAYOUT_ROWS = 16
RELAYOUT_UNITS = 4
SCAN_BLOCK = 8
PARK_PITCH = 20

VMEM_LIMIT_BYTES = 60 * 1024 * 1024
PROJ_ROW_TILE = 512
FFN_ROW_TILE = 1024
FFN_FF_TILE = 512

HIGHEST = lax.Precision.HIGHEST


def _rms_norm(x, gain):
    ms = jnp.mean(x * x, axis=-1, keepdims=True)
    return x * lax.rsqrt(ms + EPS) * gain


def _cast_specs(weights, n_steps):
    in_specs, out_specs, out_shapes = [], [], []
    for w in weights:
        rows, cols = w.shape
        slab = pl.BlockSpec((rows // n_steps, cols), lambda i: (jnp.minimum(i, n_steps - 1), 0))
        in_specs.append(slab)
        out_specs.append(slab)
        out_shapes.append(jax.ShapeDtypeStruct((rows, cols), BF16))
    return in_specs, out_specs, out_shapes


def _cast_slabs(src_refs, dst_refs):
    for src, dst in zip(src_refs, dst_refs):
        dst[...] = src[...].astype(dst.dtype)


def _ssm_prep_kernel(n_cast, lre_ref, lim_ref, ls_ref, b_re_ref, b_im_ref, c_re_ref, c_im_ref, d_ref, *refs):
    cast_src, (m_ref, bw_ref, cw_ref), cast_dst = refs[:n_cast], refs[n_cast:n_cast + 3], refs[n_cast + 3:]
    _cast_slabs(cast_src, cast_dst)
    P, H, T, GB = SSM_STATE, SSM_GROUP, CHUNK, GROUPS_PER_BLOCK
    lre, lim = lre_ref[...], lim_ref[...]
    step = jnp.exp(ls_ref[...])
    mag = jnp.exp(lre * step)
    lbr, lbi = mag * jnp.cos(lim * step), mag * jnp.sin(lim * step)
    den = lre * lre + lim * lim
    cfr = ((lbr - 1.0) * lre + lbi * lim) / den
    cfi = (lbi * lre - (lbr - 1.0) * lim) / den
    eye = (lax.broadcasted_iota(jnp.int32, (P, P), 0)
           == lax.broadcasted_iota(jnp.int32, (P, P), 1)).astype(F32)
    rows = jnp.concatenate([lbr, lbi, cfr, cfi], axis=0)
    cols = lax.dot_general(eye, rows, (((1,), (1,)), ((), ())), precision=HIGHEST,
                           preferred_element_type=F32)

    lane = lax.broadcasted_iota(jnp.int32, (1, CHUNK_LANES), 1)
    k_idx = lane // H
    sub = lax.broadcasted_iota(jnp.int32, (H, CHUNK_LANES), 0)
    lane_h = lax.broadcasted_iota(jnp.int32, (H, CHUNK_LANES), 1)
    rep = (lane_h % H == sub).astype(F32)

    def tile(a):
        return jnp.dot(a, rep, precision=HIGHEST, preferred_element_type=F32)

    def tile_t(a):
        return lax.dot_general(a, rep, (((0,), (0,)), ((), ())), precision=HIGHEST,
                               preferred_element_type=F32)

    def dot_t(a, b):
        return lax.dot_general(a, b, (((0,), (0,)), ((), ())), precision=HIGHEST,
                               preferred_element_type=F32)

    def cmul(ar, ai, br, bi):
        return ar * br - ai * bi, ar * bi + ai * br

    bw_ref[...] = jnp.zeros_like(bw_ref)
    cw_ref[...] = jnp.zeros_like(cw_ref)
    for g in range(GB):
        col = lambda quantity: cols[:, quantity * GB + g:quantity * GB + g + 1]
        l1r, l1i, cr, ci = col(0), col(1), col(2), col(3)
        sq = [(l1r, l1i)]
        for _ in range(3):
            sq.append(cmul(*sq[-1], *sq[-1]))
        pkr = pki = prr = pri = None
        for bit, (fr, fi) in enumerate(sq):
            on = ((k_idx >> bit) & 1) == 1
            ur, ui = jnp.where(on, fr, 1.0), jnp.where(on, fi, 0.0)
            dr, di = jnp.where(on, 1.0, fr), jnp.where(on, 0.0, fi)
            if bit == 0:
                pkr, pki, prr, pri = ur, ui, dr, di
            else:
                pkr, pki = cmul(pkr, pki, ur, ui)
                prr, pri = cmul(prr, pri, dr, di)

        bb_re, bb_im = cmul(cr, ci, b_re_ref[g], b_im_ref[g])

        q, j = g // 2, g % 2
        r_re = slice(j * P, (j + 1) * P)
        r_im = slice((2 + j) * P, (3 + j) * P)
        c_g = slice(j * CHUNK_LANES, (j + 1) * CHUNK_LANES)
        w_re, w_im = cmul(prr, pri, tile(bb_re), tile(bb_im))
        bw_ref[q, r_re, c_g] = w_re.astype(bw_ref.dtype)
        bw_ref[q, r_im, c_g] = w_im.astype(bw_ref.dtype)
        g_re, g_im = cmul(pkr, pki, tile_t(c_re_ref[g]), tile_t(c_im_ref[g]))
        g1_re, g1_im = cmul(l1r, l1i, g_re, g_im)
        cw_ref[q, r_re, c_g] = g1_re.astype(cw_ref.dtype)
        cw_ref[q, r_im, c_g] = (-g1_im).astype(cw_ref.dtype)
        kt = dot_t(bb_re, g_re) - dot_t(bb_im, g_im)
        kt = kt + jnp.where(lane_h == sub, d_ref[g], 0.0)
        for tau in range(T):
            blk = kt if tau == 0 else jnp.where(lane_h >= tau * H, pltpu.roll(kt, tau * H, axis=1), 0.0)
            m_ref[g, tau * H:(tau + 1) * H, :] = blk.astype(m_ref.dtype)


def _ssm_prep(lambda_re, lambda_im, log_step, b_re, b_im, c_re, c_im, d_skip, cast):
    G, P, H, GB = SSM_GROUPS, SSM_STATE, SSM_GROUP, GROUPS_PER_BLOCK
    n_steps = G // GB
    blk = lambda *shape: pl.BlockSpec(shape, lambda i: (i,) + (0,) * (len(shape) - 1))
    c_in, c_out, c_shapes = _cast_specs(cast, n_steps)
    return pl.pallas_call(
        functools.partial(_ssm_prep_kernel, len(cast)),
        grid=(n_steps,),
        in_specs=[blk(GB, P), blk(GB, P), blk(GB, 1),
                  blk(GB, P, H), blk(GB, P, H), blk(GB, H, P), blk(GB, H, P), blk(GB, H, 1)] + c_in,
        out_specs=[blk(GB, CHUNK_LANES, CHUNK_LANES), blk(GB // 2, 4 * P, 2 * CHUNK_LANES),
                   blk(GB // 2, 4 * P, 2 * CHUNK_LANES)] + c_out,
        out_shape=[jax.ShapeDtypeStruct((G, CHUNK_LANES, CHUNK_LANES), BF16),
                   jax.ShapeDtypeStruct((G // 2, 4 * P, 2 * CHUNK_LANES), BF16),
                   jax.ShapeDtypeStruct((G // 2, 4 * P, 2 * CHUNK_LANES), BF16)] + c_shapes,
        compiler_params=pltpu.CompilerParams(dimension_semantics=("parallel",),
                                             vmem_limit_bytes=VMEM_LIMIT_BYTES),
        name="ssm_prep",
    )(lambda_re, lambda_im, log_step.reshape(G, 1),
      b_re, b_im, c_re, c_im, d_skip.reshape(G, H, 1), *cast)


def _granule_transpose(units):
    lane = lax.broadcasted_iota(jnp.int32, units[0][0].shape, 1)
    units = [list(a) for a in units]
    for s in (4, 2, 1):
        upper = ((lane // SSM_GROUP) & s) != 0
        for a in units:
            for i in range(GROUPS_PER_BLOCK):
                if i & s:
                    continue
                lo, hi = a[i], a[i + s]
                a[i] = jnp.where(upper, pltpu.roll(hi, s * SSM_GROUP, axis=1), lo)
                a[i + s] = jnp.where(upper, hi, pltpu.roll(lo, LANES - s * SSM_GROUP, axis=1))
    return units


def _relayout_units(n_chunks):
    return [(b, cb, hf) for b in range(SSM_WIDTH // LANES)
            for cb in range(n_chunks // RELAYOUT_ROWS) for hf in range(2)]


def _relayout_to_groups(src_ref, dst_ref, n_chunks):
    half = CHUNK // 2
    rb = RELAYOUT_ROWS
    where = _relayout_units(n_chunks)
    batch = 2 * RELAYOUT_UNITS
    for i in range(0, len(where), batch):
        units = [[src_ref[b, pl.ds(cb * rb * PARK_PITCH + hf * half + t, rb, stride=PARK_PITCH), :].astype(BF16)
                  for t in range(half)] for (b, cb, hf) in where[i:i + batch]]
        units = _granule_transpose(units)
        for (b, cb, hf), unit in zip(where[i:i + batch], units):
            for g in range(GROUPS_PER_BLOCK):
                dst_ref[b * GROUPS_PER_BLOCK + g, cb * rb:(cb + 1) * rb, hf * LANES:(hf + 1) * LANES] = unit[g]


def _relayout_from_groups(src_ref, dst_ref, n_chunks, blocks=range(SSM_WIDTH // LANES)):
    half = CHUNK // 2
    rb = RELAYOUT_ROWS
    where = [u for u in _relayout_units(n_chunks) if u[0] in blocks]
    batch = 2 * RELAYOUT_UNITS
    for i in range(0, len(where), batch):
        units = [[src_ref[b * GROUPS_PER_BLOCK + g, cb * rb:(cb + 1) * rb, hf * LANES:(hf + 1) * LANES]
                  for g in range(GROUPS_PER_BLOCK)] for (b, cb, hf) in where[i:i + batch]]
        units = _granule_transpose(units)
        for (b, cb, hf), unit in zip(where[i:i + batch], units):
            for t in range(half):
                dst_ref[b, pl.ds(cb * rb * PARK_PITCH + hf * half + t, rb, stride=PARK_PITCH), :] = unit[t].astype(F32)


def _in_proj_kernel(n_cast, n, x_ref, gain_ref, w_ref, *refs):
    cast_src, (xg_ref, v_ref), cast_dst = refs[:n_cast], refs[n_cast:n_cast + 2], refs[n_cast + 2:-2]
    park = refs[-2:]
    s = pl.program_id(0)

    def step(p_prev, p_next):
        if p_prev is not None:
            _relayout_to_groups(p_prev, xg_ref, x_ref.shape[0] // CHUNK)
        if p_next is not None:
            _cast_slabs(cast_src, cast_dst)
            h = _rms_norm(x_ref[...], gain_ref[...]).astype(BF16)
            p = jnp.dot(h, w_ref[...], preferred_element_type=F32)
            v_ref[...] = p[:, SSM_WIDTH:].astype(v_ref.dtype)
            for b in range(SSM_WIDTH // LANES):
                for c in range(x_ref.shape[0] // CHUNK):
                    p_next[b, c * PARK_PITCH:c * PARK_PITCH + CHUNK, :] = (
                        p[c * CHUNK:(c + 1) * CHUNK, b * LANES:(b + 1) * LANES])

    middle = jnp.logical_and(s > 0, s < n)
    pl.when(s == 0)(lambda: step(None, park[0]))
    pl.when(jnp.logical_and(middle, s % 2 == 0))(lambda: step(park[1], park[0]))
    pl.when(jnp.logical_and(middle, s % 2 == 1))(lambda: step(park[0], park[1]))
    pl.when(s == n)(lambda: step(park[(n - 1) % 2], None))


def _in_proj(x, gain, w_in, cast, tm):
    L = x.shape[0]
    n = L // tm
    c_in, c_out, c_shapes = _cast_specs(cast, n)
    cur = lambda s: (jnp.minimum(s, n - 1), 0)
    return pl.pallas_call(
        functools.partial(_in_proj_kernel, len(cast), n),
        grid=(n + 1,),
        in_specs=[pl.BlockSpec((tm, D_MODEL), cur),
                  pl.BlockSpec((1, D_MODEL), lambda s: (0, 0)),
                  pl.BlockSpec((D_MODEL, D_MODEL), lambda s: (0, 0))] + c_in,
        out_specs=[pl.BlockSpec((SSM_GROUPS, tm // CHUNK, CHUNK_LANES), lambda s: (0, jnp.maximum(s - 1, 0), 0)),
                   pl.BlockSpec((tm, POOL_WIDTH), cur)] + c_out,
        out_shape=[jax.ShapeDtypeStruct((SSM_GROUPS, L // CHUNK, CHUNK_LANES), BF16),
                   jax.ShapeDtypeStruct((L, POOL_WIDTH), BF16)] + c_shapes,
        scratch_shapes=[pltpu.VMEM((SSM_WIDTH // LANES, tm // CHUNK * PARK_PITCH, LANES), F32)] * 2,
        compiler_params=pltpu.CompilerParams(dimension_semantics=("arbitrary",),
                                             vmem_limit_bytes=VMEM_LIMIT_BYTES),
        name="in_proj",
    )(x, gain, w_in, *cast)


def _cmul(ar, ai, br, bi):
    return ar * br - ai * bi, ar * bi + ai * br


def _ssm_kernel(n_cast, xg_ref, m_ref, bw_ref, cw_ref, lre_ref, lim_ref, ls_ref, *refs):
    cast_src, yg_ref, cast_dst = refs[:n_cast], refs[n_cast], refs[n_cast + 1:-4]
    fr_ref, fi_ref, cr_ref, ci_ref = refs[-4:]
    _cast_slabs(cast_src, cast_dst)
    n_chunks = xg_ref.shape[1]
    sb = SCAN_BLOCK
    n_sb = n_chunks // sb

    blk3 = (n_sb, sb, 2 * SSM_STATE)
    row_in = lax.broadcasted_iota(jnp.int32, blk3, 1)
    brow = lax.broadcasted_iota(jnp.int32, (n_sb, 2 * SSM_STATE), 0)
    for q in range(GROUPS_PER_BLOCK // 2):
        x0, x1 = xg_ref[2 * q], xg_ref[2 * q + 1]
        xp = jnp.concatenate([x0, x1], axis=1)
        e = lax.dot_general(xp, bw_ref[q], (((1,), (1,)), ((), ())), preferred_element_type=F32)
        step = jnp.exp(ls_ref[q])
        zr, zi = lre_ref[q] * step * CHUNK, lim_ref[q] * step * CHUNK
        mag = jnp.exp(zr)
        a1r, a1i = mag * jnp.cos(zi), mag * jnp.sin(zi)

        fr = e[:, :2 * SSM_STATE].reshape(blk3)
        fi = e[:, 2 * SSM_STATE:].reshape(blk3)
        ar, ai = a1r, a1i
        d = 1
        while d < sb:
            tr = jnp.where(row_in >= d, pltpu.roll(fr, d, axis=1), 0.0)
            ti = jnp.where(row_in >= d, pltpu.roll(fi, d, axis=1), 0.0)
            fr, fi = fr + ar * tr - ai * ti, fi + ar * ti + ai * tr
            ar, ai = _cmul(ar, ai, ar, ai)
            d *= 2
        fr_ref[...] = fr.reshape(n_chunks, 2 * SSM_STATE)
        fi_ref[...] = fi.reshape(n_chunks, 2 * SSM_STATE)
        gr = fr_ref[pl.ds(sb - 1, n_sb, stride=sb), :]
        gi = fi_ref[pl.ds(sb - 1, n_sb, stride=sb), :]
        gr = jnp.where(brow >= 1, pltpu.roll(gr, 1, axis=0), 0.0)
        gi = jnp.where(brow >= 1, pltpu.roll(gi, 1, axis=0), 0.0)
        d = 1
        while d < n_sb:
            tr = jnp.where(brow >= d, pltpu.roll(gr, d, axis=0), 0.0)
            ti = jnp.where(brow >= d, pltpu.roll(gi, d, axis=0), 0.0)
            gr, gi = gr + ar * tr - ai * ti, gi + ar * ti + ai * tr
            ar, ai = _cmul(ar, ai, ar, ai)
            d *= 2
        pr, pi = jnp.ones_like(a1r), jnp.zeros_like(a1i)
        for r in range(sb):
            cr_ref[pl.ds(r, n_sb, stride=sb), :] = pr * gr - pi * gi
            ci_ref[pl.ds(r, n_sb, stride=sb), :] = pr * gi + pi * gr
            pr, pi = _cmul(pr, pi, a1r, a1i)
        sr = jnp.where(row_in >= 1, pltpu.roll(fr, 1, axis=1), 0.0).reshape(n_chunks, 2 * SSM_STATE)
        si = jnp.where(row_in >= 1, pltpu.roll(fi, 1, axis=1), 0.0).reshape(n_chunks, 2 * SSM_STATE)
        sr = sr + cr_ref[...]
        si = si + ci_ref[...]
        s = jnp.concatenate([sr, si], axis=1).astype(BF16)
        cw = cw_ref[q]
        for j, xj in enumerate((x0, x1)):
            lhs = jnp.concatenate([xj, s], axis=1)
            rhs = jnp.concatenate([m_ref[2 * q + j], cw[:, j * CHUNK_LANES:(j + 1) * CHUNK_LANES]], axis=0)
            yg_ref[2 * q + j] = jnp.dot(lhs, rhs, preferred_element_type=F32).astype(yg_ref.dtype)


def _ssm(xg, m, bw_pair, cw_pair, lre_row, lim_row, ls_row, cast):
    n_chunks = xg.shape[1]
    n_blocks = SSM_WIDTH // LANES
    ppb = GROUPS_PER_BLOCK // 2
    c_in, c_out, c_shapes = _cast_specs(cast, n_blocks)
    return pl.pallas_call(
        functools.partial(_ssm_kernel, len(cast)),
        grid=(n_blocks,),
        in_specs=[pl.BlockSpec((GROUPS_PER_BLOCK, n_chunks, CHUNK_LANES), lambda b: (b, 0, 0)),
                  pl.BlockSpec((GROUPS_PER_BLOCK, CHUNK_LANES, CHUNK_LANES), lambda b: (b, 0, 0)),
                  pl.BlockSpec((ppb, 4 * SSM_STATE, 2 * CHUNK_LANES), lambda b: (b, 0, 0)),
                  pl.BlockSpec((ppb, 4 * SSM_STATE, 2 * CHUNK_LANES), lambda b: (b, 0, 0)),
                  pl.BlockSpec((ppb, 1, 2 * SSM_STATE), lambda b: (b, 0, 0)),
                  pl.BlockSpec((ppb, 1, 2 * SSM_STATE), lambda b: (b, 0, 0)),
                  pl.BlockSpec((ppb, 1, 2 * SSM_STATE), lambda b: (b, 0, 0))] + c_in,
        out_specs=[pl.BlockSpec((GROUPS_PER_BLOCK, n_chunks, CHUNK_LANES), lambda b: (b, 0, 0))] + c_out,
        out_shape=[jax.ShapeDtypeStruct(xg.shape, BF16)] + c_shapes,
        scratch_shapes=[pltpu.VMEM((n_chunks, 2 * SSM_STATE), F32)] * 4,
        compiler_params=pltpu.CompilerParams(dimension_semantics=("parallel",),
                                             vmem_limit_bytes=VMEM_LIMIT_BYTES),
        name="ssm",
    )(xg, m, bw_pair, cw_pair, lre_row, lim_row, ls_row, *cast)


def _out_proj_kernel(n_cast, yg_ref, v_ref, vh_ref, x_ref, wglu_ref, bglu_ref, wpool_ref, bpool_ref,
                     pscale_ref, wtop_ref, wbot_ref, *refs):
    cast_src, o_ref, cast_dst = refs[:n_cast], refs[n_cast], refs[n_cast + 1:-5]
    mix_park, y_ref, vbuf_ref, psum_ref = refs[-5:-3], refs[-3], refs[-2], refs[-1]
    _cast_slabs(cast_src, cast_dst)
    s = pl.program_id(0)
    tm = x_ref.shape[0]

    @pl.when(s == 0)
    def _():
        mix_park[1][...] = jnp.zeros_like(mix_park[1])

    def step(mix_prev, mix_next):
        n_tile = 2 * LANES

        def project(n):
            c = slice(n * n_tile, (n + 1) * n_tile)
            o_ref[:, c] = (x_ref[:, c]
                           + jnp.dot(mix_prev[:, :SSM_WIDTH], wtop_ref[:, c], preferred_element_type=F32)
                           + jnp.dot(mix_prev[:, SSM_WIDTH:], wbot_ref[:, c], preferred_element_type=F32))

        t0 = POOL_PAD + HALO
        n_lb = POOL_GROUP // LANES
        vh = jnp.where(s > 0, vh_ref[...].astype(F32), 0.0)
        vt = v_ref[...].astype(F32)
        for j in range(POOL_WIDTH // LANES):
            vbuf_ref[j, 0:POOL_PAD, :] = jnp.zeros((POOL_PAD, LANES), F32)
            vbuf_ref[j, POOL_PAD:t0, :] = vh[:, j * LANES:(j + 1) * LANES]
            vbuf_ref[j, t0:, :] = vt[:, j * LANES:(j + 1) * LANES]
        t1 = (lax.broadcasted_iota(jnp.int32, (tm, 1), 0) + (s * tm + 1)).astype(F32)

        def window_sum(slab, first, rows, terms, step):
            acc = slab[first:first + rows, :]
            for i in range(1, terms):
                acc = acc + slab[first - i * step:first - i * step + rows, :]
            return acc

        def pool_window(k):
            w = POOL_WINDOWS[k]
            out = []
            for j in range(k * n_lb, (k + 1) * n_lb):
                slab = vbuf_ref.at[j]
                if w <= 4:
                    total = window_sum(slab, t0, tm, w, 1)
                else:
                    part = psum_ref.at[j % (2 * n_lb)]
                    part[POOL_PAD:, :] = window_sum(slab, POOL_PAD, HALO + tm, w // 4, 1)
                    total = window_sum(part, t0, tm, 4, w // 4)
                mean = total / jnp.minimum(t1, float(w))
                out.append(mean - slab[t0:, :])
            return jnp.concatenate(out, axis=1).astype(BF16)

        def gelu(blocks):
            y = jnp.concatenate(
                [jnp.concatenate([y_ref[b, c * PARK_PITCH:c * PARK_PITCH + CHUNK, :] for c in range(tm // CHUNK)], axis=0)
                 for b in blocks], axis=1)
            return 0.5 * y * (1.0 + lax.erf(y * (1.0 / math.sqrt(2.0))))

        half_blocks = SSM_WIDTH // LANES // 2
        pk = [pool_window(0), pool_window(1)]
        project(0)
        pk.append(pool_window(2))
        project(1)
        pk.append(pool_window(3))
        project(2)
        pooled = [jnp.dot(pk[k], wpool_ref[k], preferred_element_type=F32) for k in range(len(POOL_WINDOWS))]
        yp = (jnp.concatenate(pooled, axis=1) + bpool_ref[...]) * pscale_ref[...]
        mix_next[:, SSM_WIDTH:] = yp.astype(BF16)
        _relayout_from_groups(yg_ref, y_ref, tm // CHUNK, blocks=range(0, half_blocks))
        project(3)
        _relayout_from_groups(yg_ref, y_ref, tm // CHUNK, blocks=range(half_blocks, 2 * half_blocks))
        project(4)
        g_a = gelu(range(0, half_blocks))
        project(5)
        g_b = gelu(range(half_blocks, 2 * half_blocks))
        project(6)
        g = jnp.concatenate([g_a, g_b], axis=1)
        z = jnp.dot(g.astype(BF16), wglu_ref[...], preferred_element_type=F32) + bglu_ref[...]
        mix_next[:, :SSM_WIDTH] = (g * (1.0 / (1.0 + jnp.exp(-z)))).astype(BF16)
        project(7)

    pl.when(s % 2 == 0)(lambda: step(mix_park[1], mix_park[0]))
    pl.when(s % 2 == 1)(lambda: step(mix_park[0], mix_park[1]))


def _out_proj(yg, v, x, w_glu, b_glu, w_pool, b_pool, pool_scale, w_out, cast, tm):
    L = x.shape[0]
    n = L // tm
    halo_blocks = tm // HALO
    const = lambda *shape: pl.BlockSpec(shape, lambda s: (0,) * len(shape))
    c_in, c_out, c_shapes = _cast_specs(cast, n)
    cur = lambda s: jnp.minimum(s, n - 1)
    prev = lambda s: (jnp.maximum(s - 1, 0), 0)
    return pl.pallas_call(
        functools.partial(_out_proj_kernel, len(cast)),
        grid=(n + 1,),
        in_specs=[pl.BlockSpec((SSM_GROUPS, tm // CHUNK, CHUNK_LANES), lambda s: (0, cur(s), 0)),
                  pl.BlockSpec((tm, POOL_WIDTH), lambda s: (cur(s), 0)),
                  pl.BlockSpec((HALO, POOL_WIDTH), lambda s: (jnp.maximum(cur(s) * halo_blocks - 1, 0), 0)),
                  pl.BlockSpec((tm, D_MODEL), prev),
                  const(SSM_WIDTH, SSM_WIDTH), const(1, SSM_WIDTH),
                  const(len(POOL_WINDOWS), POOL_GROUP, POOL_GROUP), const(1, POOL_WIDTH),
                  const(1, POOL_WIDTH),
                  pl.BlockSpec((SSM_WIDTH, D_MODEL), lambda s: (0, 0)),
                  pl.BlockSpec((POOL_WIDTH, D_MODEL), lambda s: (1, 0))] + c_in,
        out_specs=[pl.BlockSpec((tm, D_MODEL), prev)] + c_out,
        out_shape=[jax.ShapeDtypeStruct((L, D_MODEL), F32)] + c_shapes,
        scratch_shapes=[pltpu.VMEM((tm, SSM_WIDTH + POOL_WIDTH), BF16)] * 2
                       + [pltpu.VMEM((SSM_WIDTH // LANES, tm // CHUNK * PARK_PITCH, LANES), F32),
                          pltpu.VMEM((POOL_WIDTH // LANES, POOL_PAD + HALO + tm, LANES), F32),
                          pltpu.VMEM((2 * POOL_GROUP // LANES, POOL_PAD + HALO + tm, LANES), F32)],
        compiler_params=pltpu.CompilerParams(dimension_semantics=("arbitrary",),
                                             vmem_limit_bytes=VMEM_LIMIT_BYTES),
        name="out_proj",
    )(yg, v, v, x, w_glu, b_glu, w_pool, b_pool, pool_scale, w_out, w_out, *cast)


def _ffn_kernel(x_ref, gain_ref, wg_ref, wu_ref, wd_ref, gfin_ref, o_ref, h_ref, r_ref):
    j = pl.program_id(1)
    last = pl.num_programs(1) - 1
    lane_tiles = wg_ref.shape[1] // LANES

    def slab(first, final):
        if first:
            x = x_ref[...]
            h = (x * gain_ref[...]).astype(BF16)
            h_ref[...] = h
            r = lax.rsqrt(jnp.mean(x * x, axis=-1, keepdims=True) + EPS)
            r = jnp.broadcast_to(r, r_ref.shape)
            r_ref[...] = r
        else:
            h, r = h_ref[...], r_ref[...]
        rt = jnp.concatenate([r] * lane_tiles, axis=1)
        a = jnp.dot(h, wg_ref[...], preferred_element_type=F32) * rt
        b = jnp.dot(h, wu_ref[...], preferred_element_type=F32) * rt
        ff = (a * (1.0 / (1.0 + jnp.exp(-a))) * b).astype(BF16)
        acc = (x if first else o_ref[...]) + jnp.dot(ff, wd_ref[...], preferred_element_type=F32)
        o_ref[...] = _rms_norm(acc, gfin_ref[...]) if final else acc

    pl.when(j == 0)(lambda: slab(True, False))
    pl.when(jnp.logical_and(j > 0, j < last))(lambda: slab(False, False))
    pl.when(j == last)(lambda: slab(False, True))


def _ffn(x, gain, w_gate, w_up, w_down, gain_final, tm, tf):
    L = x.shape[0]
    d_ff = w_gate.shape[1]
    assert d_ff // tf >= 2
    return pl.pallas_call(
        _ffn_kernel,
        grid=(L // tm, d_ff // tf),
        in_specs=[pl.BlockSpec((tm, D_MODEL), lambda i, j: (i, 0)),
                  pl.BlockSpec((1, D_MODEL), lambda i, j: (0, 0)),
                  pl.BlockSpec((D_MODEL, tf), lambda i, j: (0, j)),
                  pl.BlockSpec((D_MODEL, tf), lambda i, j: (0, j)),
                  pl.BlockSpec((tf, D_MODEL), lambda i, j: (j, 0)),
                  pl.BlockSpec((1, D_MODEL), lambda i, j: (0, 0))],
        out_specs=pl.BlockSpec((tm, D_MODEL), lambda i, j: (i, 0)),
        out_shape=jax.ShapeDtypeStruct((L, D_MODEL), F32),
        scratch_shapes=[pltpu.VMEM((tm, D_MODEL), BF16), pltpu.VMEM((tm, LANES), F32)],
        compiler_params=pltpu.CompilerParams(dimension_semantics=("parallel", "arbitrary"),
                                             vmem_limit_bytes=VMEM_LIMIT_BYTES),
        name="ffn",
    )(x, gain, w_gate, w_up, w_down, gain_final)


def kernel(x, norm_mix, w_in, lambda_re, lambda_im, log_step, b_re, b_im, c_re, c_im, d_skip,
           w_glu, b_glu, w_pool, b_pool, pool_scale, w_out, norm_ffn, w_gate, w_up, w_down,
           norm_final):
    bsz, L, D = x.shape
    assert bsz == 1 and D == D_MODEL and w_in.shape[0] == 1 and L % 1024 == 0
    G, P = SSM_GROUPS, SSM_STATE
    xs = x.reshape(L, D).astype(F32)

    n_pool = len(POOL_WINDOWS)
    m, bw_pair, cw_pair, w_in_b, w_out_b, w_glu_b, w_pool_b = _ssm_prep(
        lambda_re[0].astype(F32), lambda_im[0].astype(F32), log_step[0].astype(F32),
        b_re[0].astype(F32), b_im[0].astype(F32), c_re[0].astype(F32), c_im[0].astype(F32),
        d_skip[0].astype(F32),
        cast=[w_in[0], w_out[0], w_glu[0], w_pool[0].reshape(n_pool * POOL_GROUP, POOL_GROUP)])
    lre_row = lambda_re[0].astype(F32).reshape(G // 2, 1, 2 * P)
    lim_row = lambda_im[0].astype(F32).reshape(G // 2, 1, 2 * P)
    ls_row = jnp.repeat(log_step[0].astype(F32), P).reshape(G // 2, 1, 2 * P)

    xg, v, w_gate_b = _in_proj(xs, norm_mix[0].astype(F32).reshape(1, D), w_in_b, cast=[w_gate[0]],
                               tm=PROJ_ROW_TILE)

    (yg,) = _ssm(xg, m, bw_pair, cw_pair, lre_row, lim_row, ls_row, cast=[])

    x1, w_up_b, w_down_b = _out_proj(
        yg, v, xs, w_glu_b, b_glu[0].astype(F32).reshape(1, SSM_WIDTH),
        w_pool_b.reshape(n_pool, POOL_GROUP, POOL_GROUP), b_pool[0].astype(F32).reshape(1, POOL_WIDTH),
        pool_scale[0].astype(F32).reshape(1, POOL_WIDTH), w_out_b, cast=[w_up[0], w_down[0]],
        tm=PROJ_ROW_TILE)

    out = _ffn(x1, norm_ffn[0].astype(F32).reshape(1, D), w_gate_b, w_up_b, w_down_b,
               norm_final.astype(F32).reshape(1, D), tm=FFN_ROW_TILE, tf=FFN_FF_TILE)
    return out.reshape(bsz, L, D).astype(x.dtype)
```

```python
import functools
import math

import jax
import jax.numpy as jnp
from jax import lax
from jax.experimental import pallas as pl
from jax.experimental.pallas import tpu as pltpu

F32 = jnp.float32
BF16 = jnp.bfloat16

D_MODEL = 2048
SSM_WIDTH = 1024
POOL_WIDTH = 1024
SSM_GROUP = 16
SSM_GROUPS = 64
SSM_STATE = 64
POOL_WINDOWS = (2, 4, 8, 16)
POOL_GROUP = 256
EPS = 1e-6

CHUNK = 16
CHUNK_LANES = CHUNK * SSM_GROUP
HALO = 16
POOL_PAD = 8
LANES = 128
GROUPS_PER_BLOCK = LANES // SSM_GROUP
RELAYOUT_ROWS = 16
RELAYOUT_UNITS = 4
SCAN_BLOCK = 8
PARK_PITCH = 20

VMEM_LIMIT_BYTES = 60 * 1024 * 1024
PROJ_ROW_TILE = 512
FFN_ROW_TILE = 1024
FFN_FF_TILE = 512

HIGHEST = lax.Precision.HIGHEST


def _rms_norm(x, gain):
    ms = jnp.mean(x * x, axis=-1, keepdims=True)
    return x * lax.rsqrt(ms + EPS) * gain


def _cast_specs(weights, n_steps):
    in_specs, out_specs, out_shapes = [], [], []
    for w in weights:
        rows, cols = w.shape
        slab = pl.BlockSpec((rows // n_steps, cols), lambda i: (jnp.minimum(i, n_steps - 1), 0))
        in_specs.append(slab)
        out_specs.append(slab)
        out_shapes.append(jax.ShapeDtypeStruct((rows, cols), BF16))
    return in_specs, out_specs, out_shapes


def _cast_slabs(src_refs, dst_refs):
    for src, dst in zip(src_refs, dst_refs):
        dst[...] = src[...].astype(dst.dtype)


def _ssm_prep_kernel(n_cast, lre_ref, lim_ref, ls_ref, b_re_ref, b_im_ref, c_re_ref, c_im_ref, d_ref, *refs):
    cast_src, (m_ref, bw_ref, cw_ref), cast_dst = refs[:n_cast], refs[n_cast:n_cast + 3], refs[n_cast + 3:]
    _cast_slabs(cast_src, cast_dst)
    P, H, T, GB = SSM_STATE, SSM_GROUP, CHUNK, GROUPS_PER_BLOCK
    lre, lim = lre_ref[...], lim_ref[...]
    step = jnp.exp(ls_ref[...])
    mag = jnp.exp(lre * step)
    lbr, lbi = mag * jnp.cos(lim * step), mag * jnp.sin(lim * step)
    den = lre * lre + lim * lim
    cfr = ((lbr - 1.0) * lre + lbi * lim) / den
    cfi = (lbi * lre - (lbr - 1.0) * lim) / den
    eye = (lax.broadcasted_iota(jnp.int32, (P, P), 0)
           == lax.broadcasted_iota(jnp.int32, (P, P), 1)).astype(F32)
    rows = jnp.concatenate([lbr, lbi, cfr, cfi], axis=0)
    cols = lax.dot_general(eye, rows, (((1,), (1,)), ((), ())), precision=HIGHEST,
                           preferred_element_type=F32)

    lane = lax.broadcasted_iota(jnp.int32, (1, CHUNK_LANES), 1)
    k_idx = lane // H
    sub = lax.broadcasted_iota(jnp.int32, (H, CHUNK_LANES), 0)
    lane_h = lax.broadcasted_iota(jnp.int32, (H, CHUNK_LANES), 1)
    rep = (lane_h % H == sub).astype(F32)

    def tile(a):
        return jnp.dot(a, rep, precision=HIGHEST, preferred_element_type=F32)

    def tile_t(a):
        return lax.dot_general(a, rep, (((0,), (0,)), ((), ())), precision=HIGHEST,
                               preferred_element_type=F32)

    def dot_t(a, b):
        return lax.dot_general(a, b, (((0,), (0,)), ((), ())), precision=HIGHEST,
                               preferred_element_type=F32)

    def cmul(ar, ai, br, bi):
        return ar * br - ai * bi, ar * bi + ai * br

    bw_ref[...] = jnp.zeros_like(bw_ref)
    cw_ref[...] = jnp.zeros_like(cw_ref)
    st = [dict() for _ in range(GB)]
    for g in range(GB):
        col = lambda quantity: cols[:, quantity * GB + g:quantity * GB + g + 1]
        l1r, l1i, cr, ci = col(0), col(1), col(2), col(3)
        sq = [(l1r, l1i)]
        for _ in range(3):
            sq.append(cmul(*sq[-1], *sq[-1]))
        pkr = pki = prr = pri = None
        for bit, (fr, fi) in enumerate(sq):
            on = ((k_idx >> bit) & 1) == 1
            ur, ui = jnp.where(on, fr, 1.0), jnp.where(on, fi, 0.0)
            dr, di = jnp.where(on, 1.0, fr), jnp.where(on, 0.0, fi)
            if bit == 0:
                pkr, pki, prr, pri = ur, ui, dr, di
            else:
                pkr, pki = cmul(pkr, pki, ur, ui)
                prr, pri = cmul(prr, pri, dr, di)
        st[g].update(l1=(l1r, l1i), pk=(pkr, pki), pr=(prr, pri),
                     bb=cmul(cr, ci, b_re_ref[g], b_im_ref[g]))
    for g in range(GB):
        st[g]["tb"] = (tile(st[g]["bb"][0]), tile(st[g]["bb"][1]))
        st[g]["tc"] = (tile_t(c_re_ref[g]), tile_t(c_im_ref[g]))
    for g in range(GB):
        v = st[g]
        q, j = g // 2, g % 2
        r_re = slice(j * P, (j + 1) * P)
        r_im = slice((2 + j) * P, (3 + j) * P)
        c_g = slice(j * CHUNK_LANES, (j + 1) * CHUNK_LANES)
        w_re, w_im = cmul(*v["pr"], *v["tb"])
        bw_ref[q, r_re, c_g] = w_re.astype(bw_ref.dtype)
        bw_ref[q, r_im, c_g] = w_im.astype(bw_ref.dtype)
        g_re, g_im = cmul(*v["pk"], *v["tc"])
        g1_re, g1_im = cmul(*v["l1"], g_re, g_im)
        cw_ref[q, r_re, c_g] = g1_re.astype(cw_ref.dtype)
        cw_ref[q, r_im, c_g] = (-g1_im).astype(cw_ref.dtype)
        v["g"] = (g_re, g_im)
    for g in range(GB):
        v = st[g]
        kt = dot_t(v["bb"][0], v["g"][0]) - dot_t(v["bb"][1], v["g"][1])
        v["kt"] = kt + jnp.where(lane_h == sub, d_ref[g], 0.0)
    for g in range(GB):
        kt = st[g]["kt"]
        for tau in range(T):
            blk = kt if tau == 0 else jnp.where(lane_h >= tau * H, pltpu.roll(kt, tau * H, axis=1), 0.0)
            m_ref[g, tau * H:(tau + 1) * H, :] = blk.astype(m_ref.dtype)


def _ssm_prep(lambda_re, lambda_im, log_step, b_re, b_im, c_re, c_im, d_skip, cast):
    G, P, H, GB = SSM_GROUPS, SSM_STATE, SSM_GROUP, GROUPS_PER_BLOCK
    n_steps = G // GB
    blk = lambda *shape: pl.BlockSpec(shape, lambda i: (i,) + (0,) * (len(shape) - 1))
    c_in, c_out, c_shapes = _cast_specs(cast, n_steps)
    return pl.pallas_call(
        functools.partial(_ssm_prep_kernel, len(cast)),
        grid=(n_steps,),
        in_specs=[blk(GB, P), blk(GB, P), blk(GB, 1),
                  blk(GB, P, H), blk(GB, P, H), blk(GB, H, P), blk(GB, H, P), blk(GB, H, 1)] + c_in,
        out_specs=[blk(GB, CHUNK_LANES, CHUNK_LANES), blk(GB // 2, 4 * P, 2 * CHUNK_LANES),
                   blk(GB // 2, 4 * P, 2 * CHUNK_LANES)] + c_out,
        out_shape=[jax.ShapeDtypeStruct((G, CHUNK_LANES, CHUNK_LANES), BF16),
                   jax.ShapeDtypeStruct((G // 2, 4 * P, 2 * CHUNK_LANES), BF16),
                   jax.ShapeDtypeStruct((G // 2, 4 * P, 2 * CHUNK_LANES), BF16)] + c_shapes,
        compiler_params=pltpu.CompilerParams(dimension_semantics=("parallel",),
                                             vmem_limit_bytes=VMEM_LIMIT_BYTES),
        name="ssm_prep",
    )(lambda_re, lambda_im, log_step.reshape(G, 1),
      b_re, b_im, c_re, c_im, d_skip.reshape(G, H, 1), *cast)


def _granule_transpose(units):
    lane = lax.broadcasted_iota(jnp.int32, units[0][0].shape, 1)
    units = [list(a) for a in units]
    for s in (4, 2, 1):
        upper = ((lane // SSM_GROUP) & s) != 0
        for a in units:
            for i in range(GROUPS_PER_BLOCK):
                if i & s:
                    continue
                lo, hi = a[i], a[i + s]
                a[i] = jnp.where(upper, pltpu.roll(hi, s * SSM_GROUP, axis=1), lo)
                a[i + s] = jnp.where(upper, hi, pltpu.roll(lo, LANES - s * SSM_GROUP, axis=1))
    return units


def _relayout_units(n_chunks):
    return [(b, cb, hf) for b in range(SSM_WIDTH // LANES)
            for cb in range(n_chunks // RELAYOUT_ROWS) for hf in range(2)]


def _relayout_to_groups(src_ref, dst_ref, n_chunks):
    half = CHUNK // 2
    rb = RELAYOUT_ROWS
    where = _relayout_units(n_chunks)
    batch = 2 * RELAYOUT_UNITS
    for i in range(0, len(where), batch):
        units = [[src_ref[b, pl.ds(cb * rb * PARK_PITCH + hf * half + t, rb, stride=PARK_PITCH), :].astype(BF16)
                  for t in range(half)] for (b, cb, hf) in where[i:i + batch]]
        units = _granule_transpose(units)
        for (b, cb, hf), unit in zip(where[i:i + batch], units):
            for g in range(GROUPS_PER_BLOCK):
                dst_ref[b * GROUPS_PER_BLOCK + g, cb * rb:(cb + 1) * rb, hf * LANES:(hf + 1) * LANES] = unit[g]


def _relayout_from_groups(src_ref, dst_ref, n_chunks, blocks=range(SSM_WIDTH // LANES)):
    half = CHUNK // 2
    rb = RELAYOUT_ROWS
    where = [u for u in _relayout_units(n_chunks) if u[0] in blocks]
    batch = 2 * RELAYOUT_UNITS
    for i in range(0, len(where), batch):
        units = [[src_ref[b * GROUPS_PER_BLOCK + g, cb * rb:(cb + 1) * rb, hf * LANES:(hf + 1) * LANES]
                  for g in range(GROUPS_PER_BLOCK)] for (b, cb, hf) in where[i:i + batch]]
        units = _granule_transpose(units)
        for (b, cb, hf), unit in zip(where[i:i + batch], units):
            for t in range(half):
                dst_ref[b, pl.ds(cb * rb * PARK_PITCH + hf * half + t, rb, stride=PARK_PITCH), :] = unit[t].astype(F32)


def _in_proj_kernel(n_cast, n, x_ref, gain_ref, w_ref, *refs):
    cast_src, (xg_ref, v_ref), cast_dst = refs[:n_cast], refs[n_cast:n_cast + 2], refs[n_cast + 2:-2]
    park = refs[-2:]
    s = pl.program_id(0)

    def step(p_prev, p_next):
        if p_prev is not None:
            _relayout_to_groups(p_prev, xg_ref, x_ref.shape[0] // CHUNK)
        if p_next is not None:
            _cast_slabs(cast_src, cast_dst)
            h = _rms_norm(x_ref[...], gain_ref[...]).astype(BF16)
            p = jnp.dot(h, w_ref[...], preferred_element_type=F32)
            v_ref[...] = p[:, SSM_WIDTH:].astype(v_ref.dtype)
            for b in range(SSM_WIDTH // LANES):
                for c in range(x_ref.shape[0] // CHUNK):
                    p_next[b, c * PARK_PITCH:c * PARK_PITCH + CHUNK, :] = (
                        p[c * CHUNK:(c + 1) * CHUNK, b * LANES:(b + 1) * LANES])

    middle = jnp.logical_and(s > 0, s < n)
    pl.when(s == 0)(lambda: step(None, park[0]))
    pl.when(jnp.logical_and(middle, s % 2 == 0))(lambda: step(park[1], park[0]))
    pl.when(jnp.logical_and(middle, s % 2 == 1))(lambda: step(park[0], park[1]))
    pl.when(s == n)(lambda: step(park[(n - 1) % 2], None))


def _in_proj(x, gain, w_in, cast, tm):
    L = x.shape[0]
    n = L // tm
    c_in, c_out, c_shapes = _cast_specs(cast, n)
    cur = lambda s: (jnp.minimum(s, n - 1), 0)
    return pl.pallas_call(
        functools.partial(_in_proj_kernel, len(cast), n),
        grid=(n + 1,),
        in_specs=[pl.BlockSpec((tm, D_MODEL), cur),
                  pl.BlockSpec((1, D_MODEL), lambda s: (0, 0)),
                  pl.BlockSpec((D_MODEL, D_MODEL), lambda s: (0, 0))] + c_in,
        out_specs=[pl.BlockSpec((SSM_GROUPS, tm // CHUNK, CHUNK_LANES), lambda s: (0, jnp.maximum(s - 1, 0), 0)),
                   pl.BlockSpec((tm, POOL_WIDTH), cur)] + c_out,
        out_shape=[jax.ShapeDtypeStruct((SSM_GROUPS, L // CHUNK, CHUNK_LANES), BF16),
                   jax.ShapeDtypeStruct((L, POOL_WIDTH), BF16)] + c_shapes,
        scratch_shapes=[pltpu.VMEM((SSM_WIDTH // LANES, tm // CHUNK * PARK_PITCH, LANES), F32)] * 2,
        compiler_params=pltpu.CompilerParams(dimension_semantics=("arbitrary",),
                                             vmem_limit_bytes=VMEM_LIMIT_BYTES),
        name="in_proj",
    )(x, gain, w_in, *cast)


def _cmul(ar, ai, br, bi):
    return ar * br - ai * bi, ar * bi + ai * br


def _ssm_kernel(n_cast, xg_ref, m_ref, bw_ref, cw_ref, lre_ref, lim_ref, ls_ref, *refs):
    cast_src, yg_ref, cast_dst = refs[:n_cast], refs[n_cast], refs[n_cast + 1:-4]
    fr_ref, fi_ref, cr_ref, ci_ref = refs[-4:]
    _cast_slabs(cast_src, cast_dst)
    n_chunks = xg_ref.shape[1]
    sb = SCAN_BLOCK
    n_sb = n_chunks // sb

    blk3 = (n_sb, sb, 2 * SSM_STATE)
    row_in = lax.broadcasted_iota(jnp.int32, blk3, 1)
    brow = lax.broadcasted_iota(jnp.int32, (n_sb, 2 * SSM_STATE), 0)
    for q in range(GROUPS_PER_BLOCK // 2):
        x0, x1 = xg_ref[2 * q], xg_ref[2 * q + 1]
        xp = jnp.concatenate([x0, x1], axis=1)
        e = lax.dot_general(xp, bw_ref[q], (((1,), (1,)), ((), ())), preferred_element_type=F32)
        step = jnp.exp(ls_ref[q])
        zr, zi = lre_ref[q] * step * CHUNK, lim_ref[q] * step * CHUNK
        mag = jnp.exp(zr)
        a1r, a1i = mag * jnp.cos(zi), mag * jnp.sin(zi)

        fr = e[:, :2 * SSM_STATE].reshape(blk3)
        fi = e[:, 2 * SSM_STATE:].reshape(blk3)
        ar, ai = a1r, a1i
        d = 1
        while d < sb:
            tr = jnp.where(row_in >= d, pltpu.roll(fr, d, axis=1), 0.0)
            ti = jnp.where(row_in >= d, pltpu.roll(fi, d, axis=1), 0.0)
            fr, fi = fr + ar * tr - ai * ti, fi + ar * ti + ai * tr
            ar, ai = _cmul(ar, ai, ar, ai)
            d *= 2
        fr_ref[...] = fr.reshape(n_chunks, 2 * SSM_STATE)
        fi_ref[...] = fi.reshape(n_chunks, 2 * SSM_STATE)
        gr = fr_ref[pl.ds(sb - 1, n_sb, stride=sb), :]
        gi = fi_ref[pl.ds(sb - 1, n_sb, stride=sb), :]
        gr = jnp.where(brow >= 1, pltpu.roll(gr, 1, axis=0), 0.0)
        gi = jnp.where(brow >= 1, pltpu.roll(gi, 1, axis=0), 0.0)
        d = 1
        while d < n_sb:
            tr = jnp.where(brow >= d, pltpu.roll(gr, d, axis=0), 0.0)
            ti = jnp.where(brow >= d, pltpu.roll(gi, d, axis=0), 0.0)
            gr, gi = gr + ar * tr - ai * ti, gi + ar * ti + ai * tr
            ar, ai = _cmul(ar, ai, ar, ai)
            d *= 2
        pr, pi = jnp.ones_like(a1r), jnp.zeros_like(a1i)
        for r in range(sb):
            cr_ref[pl.ds(r, n_sb, stride=sb), :] = pr * gr - pi * gi
            ci_ref[pl.ds(r, n_sb, stride=sb), :] = pr * gi + pi * gr
            pr, pi = _cmul(pr, pi, a1r, a1i)
        sr = jnp.where(row_in >= 1, pltpu.roll(fr, 1, axis=1), 0.0).reshape(n_chunks, 2 * SSM_STATE)
        si = jnp.where(row_in >= 1, pltpu.roll(fi, 1, axis=1), 0.0).reshape(n_chunks, 2 * SSM_STATE)
        sr = sr + cr_ref[...]
        si = si + ci_ref[...]
        s = jnp.concatenate([sr, si], axis=1).astype(BF16)
        cw = cw_ref[q]
        for j, xj in enumerate((x0, x1)):
            lhs = jnp.concatenate([xj, s], axis=1)
            rhs = jnp.concatenate([m_ref[2 * q + j], cw[:, j * CHUNK_LANES:(j + 1) * CHUNK_LANES]], axis=0)
            yg_ref[2 * q + j] = jnp.dot(lhs, rhs, preferred_element_type=F32).astype(yg_ref.dtype)


def _ssm(xg, m, bw_pair, cw_pair, lre_row, lim_row, ls_row, cast):
    n_chunks = xg.shape[1]
    n_blocks = SSM_WIDTH // LANES
    ppb = GROUPS_PER_BLOCK // 2
    c_in, c_out, c_shapes = _cast_specs(cast, n_blocks)
    return pl.pallas_call(
        functools.partial(_ssm_kernel, len(cast)),
        grid=(n_blocks,),
        in_specs=[pl.BlockSpec((GROUPS_PER_BLOCK, n_chunks, CHUNK_LANES), lambda b: (b, 0, 0)),
                  pl.BlockSpec((GROUPS_PER_BLOCK, CHUNK_LANES, CHUNK_LANES), lambda b: (b, 0, 0)),
                  pl.BlockSpec((ppb, 4 * SSM_STATE, 2 * CHUNK_LANES), lambda b: (b, 0, 0)),
                  pl.BlockSpec((ppb, 4 * SSM_STATE, 2 * CHUNK_LANES), lambda b: (b, 0, 0)),
                  pl.BlockSpec((ppb, 1, 2 * SSM_STATE), lambda b: (b, 0, 0)),
                  pl.BlockSpec((ppb, 1, 2 * SSM_STATE), lambda b: (b, 0, 0)),
                  pl.BlockSpec((ppb, 1, 2 * SSM_STATE), lambda b: (b, 0, 0))] + c_in,
        out_specs=[pl.BlockSpec((GROUPS_PER_BLOCK, n_chunks, CHUNK_LANES), lambda b: (b, 0, 0))] + c_out,
        out_shape=[jax.ShapeDtypeStruct(xg.shape, BF16)] + c_shapes,
        scratch_shapes=[pltpu.VMEM((n_chunks, 2 * SSM_STATE), F32)] * 4,
        compiler_params=pltpu.CompilerParams(dimension_semantics=("parallel",),
                                             vmem_limit_bytes=VMEM_LIMIT_BYTES),
        name="ssm",
    )(xg, m, bw_pair, cw_pair, lre_row, lim_row, ls_row, *cast)


def _out_proj_kernel(n_cast, yg_ref, v_ref, vh_ref, x_ref, wglu_ref, bglu_ref, wpool_ref, bpool_ref,
                     pscale_ref, wtop_ref, wbot_ref, *refs):
    cast_src, o_ref, cast_dst = refs[:n_cast], refs[n_cast], refs[n_cast + 1:-5]
    mix_park, y_ref, vbuf_ref, psum_ref = refs[-5:-3], refs[-3], refs[-2], refs[-1]
    _cast_slabs(cast_src, cast_dst)
    s = pl.program_id(0)
    tm = x_ref.shape[0]

    @pl.when(s == 0)
    def _():
        mix_park[1][...] = jnp.zeros_like(mix_park[1])

    def step(mix_prev, mix_next):
        n_tile = 2 * LANES

        def project(n):
            c = slice(n * n_tile, (n + 1) * n_tile)
            o_ref[:, c] = (x_ref[:, c]
                           + jnp.dot(mix_prev[:, :SSM_WIDTH], wtop_ref[:, c], preferred_element_type=F32)
                           + jnp.dot(mix_prev[:, SSM_WIDTH:], wbot_ref[:, c], preferred_element_type=F32))

        t0 = POOL_PAD + HALO
        n_lb = POOL_GROUP // LANES
        vh = jnp.where(s > 0, vh_ref[...].astype(F32), 0.0)
        vt = v_ref[...].astype(F32)
        for j in range(POOL_WIDTH // LANES):
            vbuf_ref[j, 0:POOL_PAD, :] = jnp.zeros((POOL_PAD, LANES), F32)
            vbuf_ref[j, POOL_PAD:t0, :] = vh[:, j * LANES:(j + 1) * LANES]
            vbuf_ref[j, t0:, :] = vt[:, j * LANES:(j + 1) * LANES]
        t1 = (lax.broadcasted_iota(jnp.int32, (tm, 1), 0) + (s * tm + 1)).astype(F32)

        def window_sum(slab, first, rows, terms, step):
            acc = slab[first:first + rows, :]
            for i in range(1, terms):
                acc = acc + slab[first - i * step:first - i * step + rows, :]
            return acc

        def pool_window(k):
            w = POOL_WINDOWS[k]
            out = []
            for j in range(k * n_lb, (k + 1) * n_lb):
                slab = vbuf_ref.at[j]
                if w <= 4:
                    total = window_sum(slab, t0, tm, w, 1)
                else:
                    part = psum_ref.at[j % (2 * n_lb)]
                    part[POOL_PAD:, :] = window_sum(slab, POOL_PAD, HALO + tm, w // 4, 1)
                    total = window_sum(part, t0, tm, 4, w // 4)
                mean = total / jnp.minimum(t1, float(w))
                out.append(mean - slab[t0:, :])
            return jnp.concatenate(out, axis=1).astype(BF16)

        def gelu(blocks):
            y = jnp.concatenate(
                [jnp.concatenate([y_ref[b, c * PARK_PITCH:c * PARK_PITCH + CHUNK, :] for c in range(tm // CHUNK)], axis=0)
                 for b in blocks], axis=1)
            return 0.5 * y * (1.0 + lax.erf(y * (1.0 / math.sqrt(2.0))))

        half_blocks = SSM_WIDTH // LANES // 2
        pk = [pool_window(0), pool_window(1)]
        project(0)
        pk.append(pool_window(2))
        project(1)
        pk.append(pool_window(3))
        project(2)
        pooled = [jnp.dot(pk[k], wpool_ref[k], preferred_element_type=F32) for k in range(len(POOL_WINDOWS))]
        yp = (jnp.concatenate(pooled, axis=1) + bpool_ref[...]) * pscale_ref[...]
        mix_next[:, SSM_WIDTH:] = yp.astype(BF16)
        _relayout_from_groups(yg_ref, y_ref, tm // CHUNK, blocks=range(0, half_blocks))
        project(3)
        _relayout_from_groups(yg_ref, y_ref, tm // CHUNK, blocks=range(half_blocks, 2 * half_blocks))
        project(4)
        g_a = gelu(range(0, half_blocks))
        project(5)
        g_b = gelu(range(half_blocks, 2 * half_blocks))
        project(6)
        g = jnp.concatenate([g_a, g_b], axis=1)
        z = jnp.dot(g.astype(BF16), wglu_ref[...], preferred_element_type=F32) + bglu_ref[...]
        mix_next[:, :SSM_WIDTH] = (g * (1.0 / (1.0 + jnp.exp(-z)))).astype(BF16)
        project(7)

    pl.when(s % 2 == 0)(lambda: step(mix_park[1], mix_park[0]))
    pl.when(s % 2 == 1)(lambda: step(mix_park[0], mix_park[1]))


def _out_proj(yg, v, x, w_glu, b_glu, w_pool, b_pool, pool_scale, w_out, cast, tm):
    L = x.shape[0]
    n = L // tm
    halo_blocks = tm // HALO
    const = lambda *shape: pl.BlockSpec(shape, lambda s: (0,) * len(shape))
    c_in, c_out, c_shapes = _cast_specs(cast, n)
    cur = lambda s: jnp.minimum(s, n - 1)
    prev = lambda s: (jnp.maximum(s - 1, 0), 0)
    return pl.pallas_call(
        functools.partial(_out_proj_kernel, len(cast)),
        grid=(n + 1,),
        in_specs=[pl.BlockSpec((SSM_GROUPS, tm // CHUNK, CHUNK_LANES), lambda s: (0, cur(s), 0)),
                  pl.BlockSpec((tm, POOL_WIDTH), lambda s: (cur(s), 0)),
                  pl.BlockSpec((HALO, POOL_WIDTH), lambda s: (jnp.maximum(cur(s) * halo_blocks - 1, 0), 0)),
                  pl.BlockSpec((tm, D_MODEL), prev),
                  const(SSM_WIDTH, SSM_WIDTH), const(1, SSM_WIDTH),
                  const(len(POOL_WINDOWS), POOL_GROUP, POOL_GROUP), const(1, POOL_WIDTH),
                  const(1, POOL_WIDTH),
                  pl.BlockSpec((SSM_WIDTH, D_MODEL), lambda s: (0, 0)),
                  pl.BlockSpec((POOL_WIDTH, D_MODEL), lambda s: (1, 0))] + c_in,
        out_specs=[pl.BlockSpec((tm, D_MODEL), prev)] + c_out,
        out_shape=[jax.ShapeDtypeStruct((L, D_MODEL), F32)] + c_shapes,
        scratch_shapes=[pltpu.VMEM((tm, SSM_WIDTH + POOL_WIDTH), BF16)] * 2
                       + [pltpu.VMEM((SSM_WIDTH // LANES, tm // CHUNK * PARK_PITCH, LANES), F32),
                          pltpu.VMEM((POOL_WIDTH // LANES, POOL_PAD + HALO + tm, LANES), F32),
                          pltpu.VMEM((2 * POOL_GROUP // LANES, POOL_PAD + HALO + tm, LANES), F32)],
        compiler_params=pltpu.CompilerParams(dimension_semantics=("arbitrary",),
                                             vmem_limit_bytes=VMEM_LIMIT_BYTES),
        name="out_proj",
    )(yg, v, v, x, w_glu, b_glu, w_pool, b_pool, pool_scale, w_out, w_out, *cast)


def _ffn_kernel(x_ref, gain_ref, wg_ref, wu_ref, wd_ref, gfin_ref, o_ref, h_ref, r_ref):
    j = pl.program_id(1)
    last = pl.num_programs(1) - 1
    lane_tiles = wg_ref.shape[1] // LANES

    def slab(first, final):
        if first:
            x = x_ref[...]
            h = (x * gain_ref[...]).astype(BF16)
            h_ref[...] = h
            r = lax.rsqrt(jnp.mean(x * x, axis=-1, keepdims=True) + EPS)
            r = jnp.broadcast_to(r, r_ref.shape)
            r_ref[...] = r
        else:
            h, r = h_ref[...], r_ref[...]
        rt = jnp.concatenate([r] * lane_tiles, axis=1)
        a = jnp.dot(h, wg_ref[...], preferred_element_type=F32) * rt
        b = jnp.dot(h, wu_ref[...], preferred_element_type=F32) * rt
        ff = (a * (1.0 / (1.0 + jnp.exp(-a))) * b).astype(BF16)
        acc = (x if first else o_ref[...]) + jnp.dot(ff, wd_ref[...], preferred_element_type=F32)
        o_ref[...] = _rms_norm(acc, gfin_ref[...]) if final else acc

    pl.when(j == 0)(lambda: slab(True, False))
    pl.when(jnp.logical_and(j > 0, j < last))(lambda: slab(False, False))
    pl.when(j == last)(lambda: slab(False, True))


def _ffn(x, gain, w_gate, w_up, w_down, gain_final, tm, tf):
    L = x.shape[0]
    d_ff = w_gate.shape[1]
    assert d_ff // tf >= 2
    return pl.pallas_call(
        _ffn_kernel,
        grid=(L // tm, d_ff // tf),
        in_specs=[pl.BlockSpec((tm, D_MODEL), lambda i, j: (i, 0)),
                  pl.BlockSpec((1, D_MODEL), lambda i, j: (0, 0)),
                  pl.BlockSpec((D_MODEL, tf), lambda i, j: (0, j)),
                  pl.BlockSpec((D_MODEL, tf), lambda i, j: (0, j)),
                  pl.BlockSpec((tf, D_MODEL), lambda i, j: (j, 0)),
                  pl.BlockSpec((1, D_MODEL), lambda i, j: (0, 0))],
        out_specs=pl.BlockSpec((tm, D_MODEL), lambda i, j: (i, 0)),
        out_shape=jax.ShapeDtypeStruct((L, D_MODEL), F32),
        scratch_shapes=[pltpu.VMEM((tm, D_MODEL), BF16), pltpu.VMEM((tm, LANES), F32)],
        compiler_params=pltpu.CompilerParams(dimension_semantics=("parallel", "arbitrary"),
                                             vmem_limit_bytes=VMEM_LIMIT_BYTES),
        name="ffn",
    )(x, gain, w_gate, w_up, w_down, gain_final)


def kernel(x, norm_mix, w_in, lambda_re, lambda_im, log_step, b_re, b_im, c_re, c_im, d_skip,
           w_glu, b_glu, w_pool, b_pool, pool_scale, w_out, norm_ffn, w_gate, w_up, w_down,
           norm_final):
    bsz, L, D = x.shape
    assert bsz == 1 and D == D_MODEL and w_in.shape[0] == 1 and L % 1024 == 0
    G, P = SSM_GROUPS, SSM_STATE
    xs = x.reshape(L, D).astype(F32)

    n_pool = len(POOL_WINDOWS)
    m, bw_pair, cw_pair, w_in_b, w_out_b, w_glu_b, w_pool_b = _ssm_prep(
        lambda_re[0].astype(F32), lambda_im[0].astype(F32), log_step[0].astype(F32),
        b_re[0].astype(F32), b_im[0].astype(F32), c_re[0].astype(F32), c_im[0].astype(F32),
        d_skip[0].astype(F32),
        cast=[w_in[0], w_out[0], w_glu[0], w_pool[0].reshape(n_pool * POOL_GROUP, POOL_GROUP)])
    lre_row = lambda_re[0].astype(F32).reshape(G // 2, 1, 2 * P)
    lim_row = lambda_im[0].astype(F32).reshape(G // 2, 1, 2 * P)
    ls_row = jnp.repeat(log_step[0].astype(F32), P).reshape(G // 2, 1, 2 * P)

    xg, v, w_gate_b = _in_proj(xs, norm_mix[0].astype(F32).reshape(1, D), w_in_b, cast=[w_gate[0]],
                               tm=PROJ_ROW_TILE)

    (yg,) = _ssm(xg, m, bw_pair, cw_pair, lre_row, lim_row, ls_row, cast=[])

    x1, w_up_b, w_down_b = _out_proj(
        yg, v, xs, w_glu_b, b_glu[0].astype(F32).reshape(1, SSM_WIDTH),
        w_pool_b.reshape(n_pool, POOL_GROUP, POOL_GROUP), b_pool[0].astype(F32).reshape(1, POOL_WIDTH),
        pool_scale[0].astype(F32).reshape(1, POOL_WIDTH), w_out_b, cast=[w_up[0], w_down[0]],
        tm=PROJ_ROW_TILE)

    out = _ffn(x1, norm_ffn[0].astype(F32).reshape(1, D), w_gate_b, w_up_b, w_down_b,
               norm_final.astype(F32).reshape(1, D), tm=FFN_ROW_TILE, tf=FFN_FF_TILE)
    return out.reshape(bsz, L, D).astype(x.dtype)
```

```python
import functools
import math

import jax
import jax.numpy as jnp
from jax import lax
from jax.experimental import pallas as pl
from jax.experimental.pallas import tpu as pltpu

F32 = jnp.float32
BF16 = jnp.bfloat16

D_MODEL = 2048
SSM_WIDTH = 1024
POOL_WIDTH = 1024
SSM_GROUP = 16
SSM_GROUPS = 64
SSM_STATE = 64
POOL_WINDOWS = (2, 4, 8, 16)
POOL_GROUP = 256
EPS = 1e-6

CHUNK = 16
CHUNK_LANES = CHUNK * SSM_GROUP
HALO = 16
POOL_PAD = 8
LANES = 128
GROUPS_PER_BLOCK = LANES // SSM_GROUP
RELAYOUT_ROWS = 16
RELAYOUT_UNITS = 4
SCAN_BLOCK = 8
PARK_PITCH = 20

VMEM_LIMIT_BYTES = 60 * 1024 * 1024
PROJ_ROW_TILE = 512
FFN_ROW_TILE = 1024
FFN_FF_TILE = 512

HIGHEST = lax.Precision.HIGHEST


def _rms_norm(x, gain):
    ms = jnp.mean(x * x, axis=-1, keepdims=True)
    return x * lax.rsqrt(ms + EPS) * gain


def _cast_specs(weights, n_steps):
    in_specs, out_specs, out_shapes = [], [], []
    for w in weights:
        rows, cols = w.shape
        slab = pl.BlockSpec((rows // n_steps, cols), lambda i: (jnp.minimum(i, n_steps - 1), 0))
        in_specs.append(slab)
        out_specs.append(slab)
        out_shapes.append(jax.ShapeDtypeStruct((rows, cols), BF16))
    return in_specs, out_specs, out_shapes


def _cast_slabs(src_refs, dst_refs):
    for src, dst in zip(src_refs, dst_refs):
        dst[...] = src[...].astype(dst.dtype)


def _ssm_prep_kernel(n_cast, lre_ref, lim_ref, ls_ref, b_re_ref, b_im_ref, c_re_ref, c_im_ref, d_ref, *refs):
    cast_src, (m_ref, bw_ref, cw_ref, a_ref), cast_dst = refs[:n_cast], refs[n_cast:n_cast + 4], refs[n_cast + 4:]
    _cast_slabs(cast_src, cast_dst)
    P, H, T, GB = SSM_STATE, SSM_GROUP, CHUNK, GROUPS_PER_BLOCK
    lre, lim = lre_ref[...], lim_ref[...]
    step = jnp.exp(ls_ref[...])
    mag = jnp.exp(lre * step)
    lbr, lbi = mag * jnp.cos(lim * step), mag * jnp.sin(lim * step)
    den = lre * lre + lim * lim
    cfr = ((lbr - 1.0) * lre + lbi * lim) / den
    cfi = (lbi * lre - (lbr - 1.0) * lim) / den
    eye = (lax.broadcasted_iota(jnp.int32, (P, P), 0)
           == lax.broadcasted_iota(jnp.int32, (P, P), 1)).astype(F32)
    rows = jnp.concatenate([lbr, lbi, cfr, cfi], axis=0)
    cols = lax.dot_general(eye, rows, (((1,), (1,)), ((), ())), precision=HIGHEST,
                           preferred_element_type=F32)

    lane = lax.broadcasted_iota(jnp.int32, (1, CHUNK_LANES), 1)
    k_idx = lane // H
    sub = lax.broadcasted_iota(jnp.int32, (H, CHUNK_LANES), 0)
    lane_h = lax.broadcasted_iota(jnp.int32, (H, CHUNK_LANES), 1)
    rep = (lane_h % H == sub).astype(F32)

    def tile(a):
        return jnp.dot(a, rep, precision=HIGHEST, preferred_element_type=F32)

    def tile_t(a):
        return lax.dot_general(a, rep, (((0,), (0,)), ((), ())), precision=HIGHEST,
                               preferred_element_type=F32)

    def dot_t(a, b):
        return lax.dot_general(a, b, (((0,), (0,)), ((), ())), precision=HIGHEST,
                               preferred_element_type=F32)

    def cmul(ar, ai, br, bi):
        return ar * br - ai * bi, ar * bi + ai * br

    bw_ref[...] = jnp.zeros_like(bw_ref)
    cw_ref[...] = jnp.zeros_like(cw_ref)
    ar, ai = lbr, lbi
    for _ in range(T.bit_length() - 1):
        ar, ai = cmul(ar, ai, ar, ai)
    for q in range(GB // 2):
        a_ref[q, 0:1, :] = jnp.concatenate([ar[2 * q:2 * q + 1, :], ar[2 * q + 1:2 * q + 2, :]], axis=1)
        a_ref[q, 1:2, :] = jnp.concatenate([ai[2 * q:2 * q + 1, :], ai[2 * q + 1:2 * q + 2, :]], axis=1)
    st = [dict() for _ in range(GB)]
    for g in range(GB):
        col = lambda quantity: cols[:, quantity * GB + g:quantity * GB + g + 1]
        l1r, l1i, cr, ci = col(0), col(1), col(2), col(3)
        sq = [(l1r, l1i)]
        for _ in range(3):
            sq.append(cmul(*sq[-1], *sq[-1]))
        pkr = pki = prr = pri = None
        for bit, (fr, fi) in enumerate(sq):
            on = ((k_idx >> bit) & 1) == 1
            ur, ui = jnp.where(on, fr, 1.0), jnp.where(on, fi, 0.0)
            dr, di = jnp.where(on, 1.0, fr), jnp.where(on, 0.0, fi)
            if bit == 0:
                pkr, pki, prr, pri = ur, ui, dr, di
            else:
                pkr, pki = cmul(pkr, pki, ur, ui)
                prr, pri = cmul(prr, pri, dr, di)
        st[g].update(l1=(l1r, l1i), pk=(pkr, pki), pr=(prr, pri),
                     bb=cmul(cr, ci, b_re_ref[g], b_im_ref[g]))
    for g in range(GB):
        st[g]["tb"] = (tile(st[g]["bb"][0]), tile(st[g]["bb"][1]))
        st[g]["tc"] = (tile_t(c_re_ref[g]), tile_t(c_im_ref[g]))
    for g in range(GB):
        v = st[g]
        q, j = g // 2, g % 2
        r_re = slice(j * P, (j + 1) * P)
        r_im = slice((2 + j) * P, (3 + j) * P)
        c_g = slice(j * CHUNK_LANES, (j + 1) * CHUNK_LANES)
        w_re, w_im = cmul(*v["pr"], *v["tb"])
        bw_ref[q, r_re, c_g] = w_re.astype(bw_ref.dtype)
        bw_ref[q, r_im, c_g] = w_im.astype(bw_ref.dtype)
        g_re, g_im = cmul(*v["pk"], *v["tc"])
        g1_re, g1_im = cmul(*v["l1"], g_re, g_im)
        cw_ref[q, r_re, c_g] = g1_re.astype(cw_ref.dtype)
        cw_ref[q, r_im, c_g] = (-g1_im).astype(cw_ref.dtype)
        v["g"] = (g_re, g_im)
    for g in range(GB):
        v = st[g]
        kt = dot_t(v["bb"][0], v["g"][0]) - dot_t(v["bb"][1], v["g"][1])
        v["kt"] = kt + jnp.where(lane_h == sub, d_ref[g], 0.0)
    for g in range(GB):
        kt = st[g]["kt"]
        for tau in range(T):
            blk = kt if tau == 0 else jnp.where(lane_h >= tau * H, pltpu.roll(kt, tau * H, axis=1), 0.0)
            m_ref[g, tau * H:(tau + 1) * H, :] = blk.astype(m_ref.dtype)


def _ssm_prep(lambda_re, lambda_im, log_step, b_re, b_im, c_re, c_im, d_skip, cast):
    G, P, H, GB = SSM_GROUPS, SSM_STATE, SSM_GROUP, GROUPS_PER_BLOCK
    n_steps = G // GB
    blk = lambda *shape: pl.BlockSpec(shape, lambda i: (i,) + (0,) * (len(shape) - 1))
    c_in, c_out, c_shapes = _cast_specs(cast, n_steps)
    return pl.pallas_call(
        functools.partial(_ssm_prep_kernel, len(cast)),
        grid=(n_steps,),
        in_specs=[blk(GB, P), blk(GB, P), blk(GB, 1),
                  blk(GB, P, H), blk(GB, P, H), blk(GB, H, P), blk(GB, H, P), blk(GB, H, 1)] + c_in,
        out_specs=[blk(GB, CHUNK_LANES, CHUNK_LANES), blk(GB // 2, 4 * P, 2 * CHUNK_LANES),
                   blk(GB // 2, 4 * P, 2 * CHUNK_LANES), blk(GB // 2, 2, 2 * P)] + c_out,
        out_shape=[jax.ShapeDtypeStruct((G, CHUNK_LANES, CHUNK_LANES), BF16),
                   jax.ShapeDtypeStruct((G // 2, 4 * P, 2 * CHUNK_LANES), BF16),
                   jax.ShapeDtypeStruct((G // 2, 4 * P, 2 * CHUNK_LANES), BF16),
                   jax.ShapeDtypeStruct((G // 2, 2, 2 * P), F32)] + c_shapes,
        compiler_params=pltpu.CompilerParams(dimension_semantics=("parallel",),
                                             vmem_limit_bytes=VMEM_LIMIT_BYTES),
        name="ssm_prep",
    )(lambda_re, lambda_im, log_step.reshape(G, 1),
      b_re, b_im, c_re, c_im, d_skip.reshape(G, H, 1), *cast)


def _granule_transpose(units):
    lane = lax.broadcasted_iota(jnp.int32, units[0][0].shape, 1)
    units = [list(a) for a in units]
    for s in (4, 2, 1):
        upper = ((lane // SSM_GROUP) & s) != 0
        for a in units:
            for i in range(GROUPS_PER_BLOCK):
                if i & s:
                    continue
                lo, hi = a[i], a[i + s]
                a[i] = jnp.where(upper, pltpu.roll(hi, s * SSM_GROUP, axis=1), lo)
                a[i + s] = jnp.where(upper, hi, pltpu.roll(lo, LANES - s * SSM_GROUP, axis=1))
    return units


def _relayout_units(n_chunks):
    return [(b, cb, hf) for b in range(SSM_WIDTH // LANES)
            for cb in range(n_chunks // RELAYOUT_ROWS) for hf in range(2)]


def _relayout_to_groups(src_ref, dst_ref, n_chunks):
    half = CHUNK // 2
    rb = RELAYOUT_ROWS
    where = _relayout_units(n_chunks)
    batch = 2 * RELAYOUT_UNITS
    for i in range(0, len(where), batch):
        units = [[src_ref[b, pl.ds(cb * rb * PARK_PITCH + hf * half + t, rb, stride=PARK_PITCH), :].astype(BF16)
                  for t in range(half)] for (b, cb, hf) in where[i:i + batch]]
        units = _granule_transpose(units)
        for (b, cb, hf), unit in zip(where[i:i + batch], units):
            for g in range(GROUPS_PER_BLOCK):
                dst_ref[b * GROUPS_PER_BLOCK + g, cb * rb:(cb + 1) * rb, hf * LANES:(hf + 1) * LANES] = unit[g]


def _relayout_from_groups(src_ref, dst_ref, n_chunks, blocks=range(SSM_WIDTH // LANES)):
    half = CHUNK // 2
    rb = RELAYOUT_ROWS
    where = [u for u in _relayout_units(n_chunks) if u[0] in blocks]
    batch = 2 * RELAYOUT_UNITS
    for i in range(0, len(where), batch):
        units = [[src_ref[b * GROUPS_PER_BLOCK + g, cb * rb:(cb + 1) * rb, hf * LANES:(hf + 1) * LANES]
                  for g in range(GROUPS_PER_BLOCK)] for (b, cb, hf) in where[i:i + batch]]
        units = _granule_transpose(units)
        for (b, cb, hf), unit in zip(where[i:i + batch], units):
            for t in range(half):
                dst_ref[b, pl.ds(cb * rb * PARK_PITCH + hf * half + t, rb, stride=PARK_PITCH), :] = unit[t].astype(F32)


def _in_proj_kernel(n_cast, n, x_ref, gain_ref, w_ref, *refs):
    cast_src, (xg_ref, v_ref), cast_dst = refs[:n_cast], refs[n_cast:n_cast + 2], refs[n_cast + 2:-2]
    park = refs[-2:]
    s = pl.program_id(0)

    def step(p_prev, p_next):
        if p_prev is not None:
            _relayout_to_groups(p_prev, xg_ref, x_ref.shape[0] // CHUNK)
        if p_next is not None:
            _cast_slabs(cast_src, cast_dst)
            h = _rms_norm(x_ref[...], gain_ref[...]).astype(BF16)
            p = jnp.dot(h, w_ref[...], preferred_element_type=F32)
            v_ref[...] = p[:, SSM_WIDTH:].astype(v_ref.dtype)
            for b in range(SSM_WIDTH // LANES):
                for c in range(x_ref.shape[0] // CHUNK):
                    p_next[b, c * PARK_PITCH:c * PARK_PITCH + CHUNK, :] = (
                        p[c * CHUNK:(c + 1) * CHUNK, b * LANES:(b + 1) * LANES])

    middle = jnp.logical_and(s > 0, s < n)
    pl.when(s == 0)(lambda: step(None, park[0]))
    pl.when(jnp.logical_and(middle, s % 2 == 0))(lambda: step(park[1], park[0]))
    pl.when(jnp.logical_and(middle, s % 2 == 1))(lambda: step(park[0], park[1]))
    pl.when(s == n)(lambda: step(park[(n - 1) % 2], None))


def _in_proj(x, gain, w_in, cast, tm):
    L = x.shape[0]
    n = L // tm
    c_in, c_out, c_shapes = _cast_specs(cast, n)
    cur = lambda s: (jnp.minimum(s, n - 1), 0)
    return pl.pallas_call(
        functools.partial(_in_proj_kernel, len(cast), n),
        grid=(n + 1,),
        in_specs=[pl.BlockSpec((tm, D_MODEL), cur),
                  pl.BlockSpec((1, D_MODEL), lambda s: (0, 0)),
                  pl.BlockSpec((D_MODEL, D_MODEL), lambda s: (0, 0))] + c_in,
        out_specs=[pl.BlockSpec((SSM_GROUPS, tm // CHUNK, CHUNK_LANES), lambda s: (0, jnp.maximum(s - 1, 0), 0)),
                   pl.BlockSpec((tm, POOL_WIDTH), cur)] + c_out,
        out_shape=[jax.ShapeDtypeStruct((SSM_GROUPS, L // CHUNK, CHUNK_LANES), BF16),
                   jax.ShapeDtypeStruct((L, POOL_WIDTH), BF16)] + c_shapes,
        scratch_shapes=[pltpu.VMEM((SSM_WIDTH // LANES, tm // CHUNK * PARK_PITCH, LANES), F32)] * 2,
        compiler_params=pltpu.CompilerParams(dimension_semantics=("arbitrary",),
                                             vmem_limit_bytes=VMEM_LIMIT_BYTES),
        name="in_proj",
    )(x, gain, w_in, *cast)


def _cmul(ar, ai, br, bi):
    return ar * br - ai * bi, ar * bi + ai * br


def _ssm_kernel(n_cast, xg_ref, m_ref, bw_ref, cw_ref, a_ref, *refs):
    cast_src, yg_ref, cast_dst = refs[:n_cast], refs[n_cast], refs[n_cast + 1:-4]
    fr_ref, fi_ref, cr_ref, ci_ref = refs[-4:]
    _cast_slabs(cast_src, cast_dst)
    n_chunks = xg_ref.shape[1]
    sb = SCAN_BLOCK
    n_sb = n_chunks // sb

    blk3 = (n_sb, sb, 2 * SSM_STATE)
    row_in = lax.broadcasted_iota(jnp.int32, blk3, 1)
    brow = lax.broadcasted_iota(jnp.int32, (n_sb, 2 * SSM_STATE), 0)
    for q in range(GROUPS_PER_BLOCK // 2):
        x0, x1 = xg_ref[2 * q], xg_ref[2 * q + 1]
        xp = jnp.concatenate([x0, x1], axis=1)
        e = lax.dot_general(xp, bw_ref[q], (((1,), (1,)), ((), ())), preferred_element_type=F32)
        a1r, a1i = a_ref[q, 0:1, :], a_ref[q, 1:2, :]

        fr = e[:, :2 * SSM_STATE].reshape(blk3)
        fi = e[:, 2 * SSM_STATE:].reshape(blk3)
        ar, ai = a1r, a1i
        d = 1
        while d < sb:
            tr = jnp.where(row_in >= d, pltpu.roll(fr, d, axis=1), 0.0)
            ti = jnp.where(row_in >= d, pltpu.roll(fi, d, axis=1), 0.0)
            fr, fi = fr + ar * tr - ai * ti, fi + ar * ti + ai * tr
            ar, ai = _cmul(ar, ai, ar, ai)
            d *= 2
        fr_ref[...] = fr.reshape(n_chunks, 2 * SSM_STATE)
        fi_ref[...] = fi.reshape(n_chunks, 2 * SSM_STATE)
        gr = fr_ref[pl.ds(sb - 1, n_sb, stride=sb), :]
        gi = fi_ref[pl.ds(sb - 1, n_sb, stride=sb), :]
        gr = jnp.where(brow >= 1, pltpu.roll(gr, 1, axis=0), 0.0)
        gi = jnp.where(brow >= 1, pltpu.roll(gi, 1, axis=0), 0.0)
        d = 1
        while d < n_sb:
            tr = jnp.where(brow >= d, pltpu.roll(gr, d, axis=0), 0.0)
            ti = jnp.where(brow >= d, pltpu.roll(gi, d, axis=0), 0.0)
            gr, gi = gr + ar * tr - ai * ti, gi + ar * ti + ai * tr
            ar, ai = _cmul(ar, ai, ar, ai)
            d *= 2
        pr, pi = jnp.ones_like(a1r), jnp.zeros_like(a1i)
        for r in range(sb):
            cr_ref[pl.ds(r, n_sb, stride=sb), :] = pr * gr - pi * gi
            ci_ref[pl.ds(r, n_sb, stride=sb), :] = pr * gi + pi * gr
            pr, pi = _cmul(pr, pi, a1r, a1i)
        sr = jnp.where(row_in >= 1, pltpu.roll(fr, 1, axis=1), 0.0).reshape(n_chunks, 2 * SSM_STATE)
        si = jnp.where(row_in >= 1, pltpu.roll(fi, 1, axis=1), 0.0).reshape(n_chunks, 2 * SSM_STATE)
        sr = sr + cr_ref[...]
        si = si + ci_ref[...]
        s = jnp.concatenate([sr, si], axis=1).astype(BF16)
        cw = cw_ref[q]
        for j, xj in enumerate((x0, x1)):
            lhs = jnp.concatenate([xj, s], axis=1)
            rhs = jnp.concatenate([m_ref[2 * q + j], cw[:, j * CHUNK_LANES:(j + 1) * CHUNK_LANES]], axis=0)
            yg_ref[2 * q + j] = jnp.dot(lhs, rhs, preferred_element_type=F32).astype(yg_ref.dtype)


def _ssm(xg, m, bw_pair, cw_pair, a_pair, cast):
    n_chunks = xg.shape[1]
    n_blocks = SSM_WIDTH // LANES
    ppb = GROUPS_PER_BLOCK // 2
    c_in, c_out, c_shapes = _cast_specs(cast, n_blocks)
    return pl.pallas_call(
        functools.partial(_ssm_kernel, len(cast)),
        grid=(n_blocks,),
        in_specs=[pl.BlockSpec((GROUPS_PER_BLOCK, n_chunks, CHUNK_LANES), lambda b: (b, 0, 0)),
                  pl.BlockSpec((GROUPS_PER_BLOCK, CHUNK_LANES, CHUNK_LANES), lambda b: (b, 0, 0)),
                  pl.BlockSpec((ppb, 4 * SSM_STATE, 2 * CHUNK_LANES), lambda b: (b, 0, 0)),
                  pl.BlockSpec((ppb, 4 * SSM_STATE, 2 * CHUNK_LANES), lambda b: (b, 0, 0)),
                  pl.BlockSpec((ppb, 2, 2 * SSM_STATE), lambda b: (b, 0, 0))] + c_in,
        out_specs=[pl.BlockSpec((GROUPS_PER_BLOCK, n_chunks, CHUNK_LANES), lambda b: (b, 0, 0))] + c_out,
        out_shape=[jax.ShapeDtypeStruct(xg.shape, BF16)] + c_shapes,
        scratch_shapes=[pltpu.VMEM((n_chunks, 2 * SSM_STATE), F32)] * 4,
        compiler_params=pltpu.CompilerParams(dimension_semantics=("parallel",),
                                             vmem_limit_bytes=VMEM_LIMIT_BYTES),
        name="ssm",
    )(xg, m, bw_pair, cw_pair, a_pair, *cast)


def _out_proj_kernel(n_cast, yg_ref, v_ref, vh_ref, x_ref, wglu_ref, bglu_ref, wpool_ref, bpool_ref,
                     pscale_ref, wtop_ref, wbot_ref, *refs):
    cast_src, o_ref, cast_dst = refs[:n_cast], refs[n_cast], refs[n_cast + 1:-5]
    mix_park, y_ref, vbuf_ref, psum_ref = refs[-5:-3], refs[-3], refs[-2], refs[-1]
    _cast_slabs(cast_src, cast_dst)
    s = pl.program_id(0)
    tm = x_ref.shape[0]

    @pl.when(s == 0)
    def _():
        mix_park[1][...] = jnp.zeros_like(mix_park[1])

    def step(mix_prev, mix_next):
        n_tile = 2 * LANES

        def project(n):
            c = slice(n * n_tile, (n + 1) * n_tile)
            o_ref[:, c] = (x_ref[:, c]
                           + jnp.dot(mix_prev[:, :SSM_WIDTH], wtop_ref[:, c], preferred_element_type=F32)
                           + jnp.dot(mix_prev[:, SSM_WIDTH:], wbot_ref[:, c], preferred_element_type=F32))

        t0 = POOL_PAD + HALO
        n_lb = POOL_GROUP // LANES
        vh = jnp.where(s > 0, vh_ref[...].astype(F32), 0.0)
        vt = v_ref[...].astype(F32)
        for j in range(POOL_WIDTH // LANES):
            vbuf_ref[j, 0:POOL_PAD, :] = jnp.zeros((POOL_PAD, LANES), F32)
            vbuf_ref[j, POOL_PAD:t0, :] = vh[:, j * LANES:(j + 1) * LANES]
            vbuf_ref[j, t0:, :] = vt[:, j * LANES:(j + 1) * LANES]
        t1 = (lax.broadcasted_iota(jnp.int32, (tm, 1), 0) + (s * tm + 1)).astype(F32)

        def window_sum(slab, first, rows, terms, step):
            acc = slab[first:first + rows, :]
            for i in range(1, terms):
                acc = acc + slab[first - i * step:first - i * step + rows, :]
            return acc

        def pool_window(k):
            w = POOL_WINDOWS[k]
            out = []
            for j in range(k * n_lb, (k + 1) * n_lb):
                slab = vbuf_ref.at[j]
                if w <= 4:
                    total = window_sum(slab, t0, tm, w, 1)
                else:
                    part = psum_ref.at[j % (2 * n_lb)]
                    part[POOL_PAD:, :] = window_sum(slab, POOL_PAD, HALO + tm, w // 4, 1)
                    total = window_sum(part, t0, tm, 4, w // 4)
                mean = total / jnp.minimum(t1, float(w))
                out.append(mean - slab[t0:, :])
            return jnp.concatenate(out, axis=1).astype(BF16)

        def gelu(blocks):
            y = jnp.concatenate(
                [jnp.concatenate([y_ref[b, c * PARK_PITCH:c * PARK_PITCH + CHUNK, :] for c in range(tm // CHUNK)], axis=0)
                 for b in blocks], axis=1)
            return 0.5 * y * (1.0 + lax.erf(y * (1.0 / math.sqrt(2.0))))

        half_blocks = SSM_WIDTH // LANES // 2
        pk = [pool_window(0), pool_window(1)]
        project(0)
        pk.append(pool_window(2))
        project(1)
        pk.append(pool_window(3))
        project(2)
        pooled = [jnp.dot(pk[k], wpool_ref[k], preferred_element_type=F32) for k in range(len(POOL_WINDOWS))]
        yp = (jnp.concatenate(pooled, axis=1) + bpool_ref[...]) * pscale_ref[...]
        mix_next[:, SSM_WIDTH:] = yp.astype(BF16)
        _relayout_from_groups(yg_ref, y_ref, tm // CHUNK, blocks=range(0, half_blocks))
        project(3)
        _relayout_from_groups(yg_ref, y_ref, tm // CHUNK, blocks=range(half_blocks, 2 * half_blocks))
        project(4)
        g_a = gelu(range(0, half_blocks))
        project(5)
        g_b = gelu(range(half_blocks, 2 * half_blocks))
        project(6)
        g = jnp.concatenate([g_a, g_b], axis=1)
        z = jnp.dot(g.astype(BF16), wglu_ref[...], preferred_element_type=F32) + bglu_ref[...]
        mix_next[:, :SSM_WIDTH] = (g * (1.0 / (1.0 + jnp.exp(-z)))).astype(BF16)
        project(7)

    pl.when(s % 2 == 0)(lambda: step(mix_park[1], mix_park[0]))
    pl.when(s % 2 == 1)(lambda: step(mix_park[0], mix_park[1]))


def _out_proj(yg, v, x, w_glu, b_glu, w_pool, b_pool, pool_scale, w_out, cast, tm):
    L = x.shape[0]
    n = L // tm
    halo_blocks = tm // HALO
    const = lambda *shape: pl.BlockSpec(shape, lambda s: (0,) * len(shape))
    c_in, c_out, c_shapes = _cast_specs(cast, n)
    cur = lambda s: jnp.minimum(s, n - 1)
    prev = lambda s: (jnp.maximum(s - 1, 0), 0)
    return pl.pallas_call(
        functools.partial(_out_proj_kernel, len(cast)),
        grid=(n + 1,),
        in_specs=[pl.BlockSpec((SSM_GROUPS, tm // CHUNK, CHUNK_LANES), lambda s: (0, cur(s), 0)),
                  pl.BlockSpec((tm, POOL_WIDTH), lambda s: (cur(s), 0)),
                  pl.BlockSpec((HALO, POOL_WIDTH), lambda s: (jnp.maximum(cur(s) * halo_blocks - 1, 0), 0)),
                  pl.BlockSpec((tm, D_MODEL), prev),
                  const(SSM_WIDTH, SSM_WIDTH), const(1, SSM_WIDTH),
                  const(len(POOL_WINDOWS), POOL_GROUP, POOL_GROUP), const(1, POOL_WIDTH),
                  const(1, POOL_WIDTH),
                  pl.BlockSpec((SSM_WIDTH, D_MODEL), lambda s: (0, 0)),
                  pl.BlockSpec((POOL_WIDTH, D_MODEL), lambda s: (1, 0))] + c_in,
        out_specs=[pl.BlockSpec((tm, D_MODEL), prev)] + c_out,
        out_shape=[jax.ShapeDtypeStruct((L, D_MODEL), F32)] + c_shapes,
        scratch_shapes=[pltpu.VMEM((tm, SSM_WIDTH + POOL_WIDTH), BF16)] * 2
                       + [pltpu.VMEM((SSM_WIDTH // LANES, tm // CHUNK * PARK_PITCH, LANES), F32),
                          pltpu.VMEM((POOL_WIDTH // LANES, POOL_PAD + HALO + tm, LANES), F32),
                          pltpu.VMEM((2 * POOL_GROUP // LANES, POOL_PAD + HALO + tm, LANES), F32)],
        compiler_params=pltpu.CompilerParams(dimension_semantics=("arbitrary",),
                                             vmem_limit_bytes=VMEM_LIMIT_BYTES),
        name="out_proj",
    )(yg, v, v, x, w_glu, b_glu, w_pool, b_pool, pool_scale, w_out, w_out, *cast)


def _ffn_kernel(x_ref, gain_ref, wg_ref, wu_ref, wd_ref, gfin_ref, o_ref, h_ref, r_ref):
    j = pl.program_id(1)
    last = pl.num_programs(1) - 1
    lane_tiles = wg_ref.shape[1] // LANES

    def slab(first, final):
        if first:
            x = x_ref[...]
            h = (x * gain_ref[...]).astype(BF16)
            h_ref[...] = h
            r = lax.rsqrt(jnp.mean(x * x, axis=-1, keepdims=True) + EPS)
            r = jnp.broadcast_to(r, r_ref.shape)
            r_ref[...] = r
        else:
            h, r = h_ref[...], r_ref[...]
        rt = jnp.concatenate([r] * lane_tiles, axis=1)
        a = jnp.dot(h, wg_ref[...], preferred_element_type=F32) * rt
        b = jnp.dot(h, wu_ref[...], preferred_element_type=F32) * rt
        ff = (a * (1.0 / (1.0 + jnp.exp(-a))) * b).astype(BF16)
        acc = (x if first else o_ref[...]) + jnp.dot(ff, wd_ref[...], preferred_element_type=F32)
        o_ref[...] = _rms_norm(acc, gfin_ref[...]) if final else acc

    pl.when(j == 0)(lambda: slab(True, False))
    pl.when(jnp.logical_and(j > 0, j < last))(lambda: slab(False, False))
    pl.when(j == last)(lambda: slab(False, True))


def _ffn(x, gain, w_gate, w_up, w_down, gain_final, tm, tf):
    L = x.shape[0]
    d_ff = w_gate.shape[1]
    assert d_ff // tf >= 2
    return pl.pallas_call(
        _ffn_kernel,
        grid=(L // tm, d_ff // tf),
        in_specs=[pl.BlockSpec((tm, D_MODEL), lambda i, j: (i, 0)),
                  pl.BlockSpec((1, D_MODEL), lambda i, j: (0, 0)),
                  pl.BlockSpec((D_MODEL, tf), lambda i, j: (0, j)),
                  pl.BlockSpec((D_MODEL, tf), lambda i, j: (0, j)),
                  pl.BlockSpec((tf, D_MODEL), lambda i, j: (j, 0)),
                  pl.BlockSpec((1, D_MODEL), lambda i, j: (0, 0))],
        out_specs=pl.BlockSpec((tm, D_MODEL), lambda i, j: (i, 0)),
        out_shape=jax.ShapeDtypeStruct((L, D_MODEL), F32),
        scratch_shapes=[pltpu.VMEM((tm, D_MODEL), BF16), pltpu.VMEM((tm, LANES), F32)],
        compiler_params=pltpu.CompilerParams(dimension_semantics=("parallel", "arbitrary"),
                                             vmem_limit_bytes=VMEM_LIMIT_BYTES),
        name="ffn",
    )(x, gain, w_gate, w_up, w_down, gain_final)


def kernel(x, norm_mix, w_in, lambda_re, lambda_im, log_step, b_re, b_im, c_re, c_im, d_skip,
           w_glu, b_glu, w_pool, b_pool, pool_scale, w_out, norm_ffn, w_gate, w_up, w_down,
           norm_final):
    bsz, L, D = x.shape
    assert bsz == 1 and D == D_MODEL and w_in.shape[0] == 1 and L % 1024 == 0
    G, P = SSM_GROUPS, SSM_STATE
    xs = x.reshape(L, D).astype(F32)

    n_pool = len(POOL_WINDOWS)
    m, bw_pair, cw_pair, a_pair, w_in_b, w_out_b, w_glu_b, w_pool_b = _ssm_prep(
        lambda_re[0].astype(F32), lambda_im[0].astype(F32), log_step[0].astype(F32),
        b_re[0].astype(F32), b_im[0].astype(F32), c_re[0].astype(F32), c_im[0].astype(F32),
        d_skip[0].astype(F32),
        cast=[w_in[0], w_out[0], w_glu[0], w_pool[0].reshape(n_pool * POOL_GROUP, POOL_GROUP)])

    xg, v, w_gate_b = _in_proj(xs, norm_mix[0].astype(F32).reshape(1, D), w_in_b, cast=[w_gate[0]],
                               tm=PROJ_ROW_TILE)

    (yg,) = _ssm(xg, m, bw_pair, cw_pair, a_pair, cast=[])

    x1, w_up_b, w_down_b = _out_proj(
        yg, v, xs, w_glu_b, b_glu[0].astype(F32).reshape(1, SSM_WIDTH),
        w_pool_b.reshape(n_pool, POOL_GROUP, POOL_GROUP), b_pool[0].astype(F32).reshape(1, POOL_WIDTH),
        pool_scale[0].astype(F32).reshape(1, POOL_WIDTH), w_out_b, cast=[w_up[0], w_down[0]],
        tm=PROJ_ROW_TILE)

    out = _ffn(x1, norm_ffn[0].astype(F32).reshape(1, D), w_gate_b, w_up_b, w_down_b,
               norm_final.astype(F32).reshape(1, D), tm=FFN_ROW_TILE, tf=FFN_FF_TILE)
    return out.reshape(bsz, L, D).astype(x.dtype)
```

```python
import functools
import math

import jax
import jax.numpy as jnp
from jax import lax
from jax.experimental import pallas as pl
from jax.experimental.pallas import tpu as pltpu

F32 = jnp.float32
BF16 = jnp.bfloat16

D_MODEL = 2048
SSM_WIDTH = 1024
POOL_WIDTH = 1024
SSM_GROUP = 16
SSM_GROUPS = 64
SSM_STATE = 64
POOL_WINDOWS = (2, 4, 8, 16)
POOL_GROUP = 256
EPS = 1e-6

CHUNK = 16
CHUNK_LANES = CHUNK * SSM_GROUP
HALO = 16
POOL_PAD = 8
LANES = 128
GROUPS_PER_BLOCK = LANES // SSM_GROUP
RELAYOUT_ROWS = 16
RELAYOUT_UNITS = 4
SCAN_BLOCK = 8
PARK_PITCH = 20

VMEM_LIMIT_BYTES = 60 * 1024 * 1024
PROJ_ROW_TILE = 512
FFN_ROW_TILE = 1024
FFN_FF_TILE = 512

HIGHEST = lax.Precision.HIGHEST


def _rms_norm(x, gain):
    ms = jnp.mean(x * x, axis=-1, keepdims=True)
    return x * lax.rsqrt(ms + EPS) * gain


def _cast_specs(weights, n_steps):
    in_specs, out_specs, out_shapes = [], [], []
    for w in weights:
        rows, cols = w.shape
        slab = pl.BlockSpec((rows // n_steps, cols), lambda i: (jnp.minimum(i, n_steps - 1), 0))
        in_specs.append(slab)
        out_specs.append(slab)
        out_shapes.append(jax.ShapeDtypeStruct((rows, cols), BF16))
    return in_specs, out_specs, out_shapes


def _cast_slabs(src_refs, dst_refs):
    for src, dst in zip(src_refs, dst_refs):
        dst[...] = src[...].astype(dst.dtype)


def _ssm_prep_kernel(n_cast, lre_ref, lim_ref, ls_ref, b_re_ref, b_im_ref, c_re_ref, c_im_ref, d_ref, *refs):
    cast_src, (m_ref, bw_ref, cw_ref, a_ref), cast_dst = refs[:n_cast], refs[n_cast:n_cast + 4], refs[n_cast + 4:]
    _cast_slabs(cast_src, cast_dst)
    P, H, T, GB = SSM_STATE, SSM_GROUP, CHUNK, GROUPS_PER_BLOCK
    lre, lim = lre_ref[...], lim_ref[...]
    step = jnp.exp(ls_ref[...])
    mag = jnp.exp(lre * step)
    lbr, lbi = mag * jnp.cos(lim * step), mag * jnp.sin(lim * step)
    den = lre * lre + lim * lim
    cfr = ((lbr - 1.0) * lre + lbi * lim) / den
    cfi = (lbi * lre - (lbr - 1.0) * lim) / den
    eye = (lax.broadcasted_iota(jnp.int32, (P, P), 0)
           == lax.broadcasted_iota(jnp.int32, (P, P), 1)).astype(F32)
    rows = jnp.concatenate([lbr, lbi, cfr, cfi], axis=0)
    cols = lax.dot_general(eye, rows, (((1,), (1,)), ((), ())), precision=HIGHEST,
                           preferred_element_type=F32)

    lane = lax.broadcasted_iota(jnp.int32, (1, CHUNK_LANES), 1)
    k_idx = lane // H
    sub = lax.broadcasted_iota(jnp.int32, (H, CHUNK_LANES), 0)
    lane_h = lax.broadcasted_iota(jnp.int32, (H, CHUNK_LANES), 1)
    rep = (lane_h % H == sub).astype(F32)

    def tile(a):
        return jnp.dot(a, rep, precision=HIGHEST, preferred_element_type=F32)

    def tile_t(a):
        return lax.dot_general(a, rep, (((0,), (0,)), ((), ())), precision=HIGHEST,
                               preferred_element_type=F32)

    def dot_t(a, b):
        return lax.dot_general(a, b, (((0,), (0,)), ((), ())), precision=HIGHEST,
                               preferred_element_type=F32)

    def cmul(ar, ai, br, bi):
        return ar * br - ai * bi, ar * bi + ai * br

    bw_ref[...] = jnp.zeros_like(bw_ref)
    cw_ref[...] = jnp.zeros_like(cw_ref)
    ar, ai = lbr, lbi
    for _ in range(T.bit_length() - 1):
        ar, ai = cmul(ar, ai, ar, ai)
    for q in range(GB // 2):
        a_ref[q, 0:1, :] = jnp.concatenate([ar[2 * q:2 * q + 1, :], ar[2 * q + 1:2 * q + 2, :]], axis=1)
        a_ref[q, 1:2, :] = jnp.concatenate([ai[2 * q:2 * q + 1, :], ai[2 * q + 1:2 * q + 2, :]], axis=1)
    st = [dict() for _ in range(GB)]
    for g in range(GB):
        col = lambda quantity: cols[:, quantity * GB + g:quantity * GB + g + 1]
        l1r, l1i, cr, ci = col(0), col(1), col(2), col(3)
        sq = [(l1r, l1i)]
        for _ in range(3):
            sq.append(cmul(*sq[-1], *sq[-1]))
        pkr = pki = prr = pri = None
        for bit, (fr, fi) in enumerate(sq):
            on = ((k_idx >> bit) & 1) == 1
            ur, ui = jnp.where(on, fr, 1.0), jnp.where(on, fi, 0.0)
            dr, di = jnp.where(on, 1.0, fr), jnp.where(on, 0.0, fi)
            if bit == 0:
                pkr, pki, prr, pri = ur, ui, dr, di
            else:
                pkr, pki = cmul(pkr, pki, ur, ui)
                prr, pri = cmul(prr, pri, dr, di)
        st[g].update(l1=(l1r, l1i), pk=(pkr, pki), pr=(prr, pri),
                     bb=cmul(cr, ci, b_re_ref[g], b_im_ref[g]))
    for g in range(GB):
        st[g]["tb"] = (tile(st[g]["bb"][0]), tile(st[g]["bb"][1]))
        st[g]["tc"] = (tile_t(c_re_ref[g]), tile_t(c_im_ref[g]))
    for g in range(GB):
        v = st[g]
        q, j = g // 2, g % 2
        r_re = slice(j * P, (j + 1) * P)
        r_im = slice((2 + j) * P, (3 + j) * P)
        c_g = slice(j * CHUNK_LANES, (j + 1) * CHUNK_LANES)
        w_re, w_im = cmul(*v["pr"], *v["tb"])
        bw_ref[q, r_re, c_g] = w_re.astype(bw_ref.dtype)
        bw_ref[q, r_im, c_g] = w_im.astype(bw_ref.dtype)
        g_re, g_im = cmul(*v["pk"], *v["tc"])
        g1_re, g1_im = cmul(*v["l1"], g_re, g_im)
        cw_ref[q, r_re, c_g] = g1_re.astype(cw_ref.dtype)
        cw_ref[q, r_im, c_g] = (-g1_im).astype(cw_ref.dtype)
        v["g"] = (g_re, g_im)
    for g in range(GB):
        v = st[g]
        kt = dot_t(v["bb"][0], v["g"][0]) - dot_t(v["bb"][1], v["g"][1])
        v["kt"] = kt + jnp.where(lane_h == sub, d_ref[g], 0.0)
    for g in range(GB):
        kt = st[g]["kt"]
        for tau in range(T):
            blk = kt if tau == 0 else jnp.where(lane_h >= tau * H, pltpu.roll(kt, tau * H, axis=1), 0.0)
            m_ref[g, tau * H:(tau + 1) * H, :] = blk.astype(m_ref.dtype)


def _ssm_prep(lambda_re, lambda_im, log_step, b_re, b_im, c_re, c_im, d_skip, cast):
    G, P, H, GB = SSM_GROUPS, SSM_STATE, SSM_GROUP, GROUPS_PER_BLOCK
    n_steps = G // GB
    blk = lambda *shape: pl.BlockSpec(shape, lambda i: (i,) + (0,) * (len(shape) - 1))
    c_in, c_out, c_shapes = _cast_specs(cast, n_steps)
    return pl.pallas_call(
        functools.partial(_ssm_prep_kernel, len(cast)),
        grid=(n_steps,),
        in_specs=[blk(GB, P), blk(GB, P), blk(GB, 1),
                  blk(GB, P, H), blk(GB, P, H), blk(GB, H, P), blk(GB, H, P), blk(GB, H, 1)] + c_in,
        out_specs=[blk(GB, CHUNK_LANES, CHUNK_LANES), blk(GB // 2, 4 * P, 2 * CHUNK_LANES),
                   blk(GB // 2, 4 * P, 2 * CHUNK_LANES), blk(GB // 2, 2, 2 * P)] + c_out,
        out_shape=[jax.ShapeDtypeStruct((G, CHUNK_LANES, CHUNK_LANES), BF16),
                   jax.ShapeDtypeStruct((G // 2, 4 * P, 2 * CHUNK_LANES), BF16),
                   jax.ShapeDtypeStruct((G // 2, 4 * P, 2 * CHUNK_LANES), BF16),
                   jax.ShapeDtypeStruct((G // 2, 2, 2 * P), F32)] + c_shapes,
        compiler_params=pltpu.CompilerParams(dimension_semantics=("parallel",),
                                             vmem_limit_bytes=VMEM_LIMIT_BYTES),
        name="ssm_prep",
    )(lambda_re, lambda_im, log_step.reshape(G, 1),
      b_re, b_im, c_re, c_im, d_skip.reshape(G, H, 1), *cast)


def _granule_transpose(units):
    lane = lax.broadcasted_iota(jnp.int32, units[0][0].shape, 1)
    units = [list(a) for a in units]
    for s in (4, 2, 1):
        upper = ((lane // SSM_GROUP) & s) != 0
        for a in units:
            for i in range(GROUPS_PER_BLOCK):
                if i & s:
                    continue
                lo, hi = a[i], a[i + s]
                a[i] = jnp.where(upper, pltpu.roll(hi, s * SSM_GROUP, axis=1), lo)
                a[i + s] = jnp.where(upper, hi, pltpu.roll(lo, LANES - s * SSM_GROUP, axis=1))
    return units


def _relayout_units(n_chunks):
    return [(b, cb, hf) for b in range(SSM_WIDTH // LANES)
            for cb in range(n_chunks // RELAYOUT_ROWS) for hf in range(2)]


def _relayout_to_groups(src_ref, dst_ref, n_chunks):
    half = CHUNK // 2
    rb = RELAYOUT_ROWS
    where = _relayout_units(n_chunks)
    batch = 2 * RELAYOUT_UNITS
    for i in range(0, len(where), batch):
        units = [[src_ref[b, pl.ds(cb * rb * PARK_PITCH + hf * half + t, rb, stride=PARK_PITCH), :].astype(BF16)
                  for t in range(half)] for (b, cb, hf) in where[i:i + batch]]
        units = _granule_transpose(units)
        for (b, cb, hf), unit in zip(where[i:i + batch], units):
            for g in range(GROUPS_PER_BLOCK):
                dst_ref[b * GROUPS_PER_BLOCK + g, cb * rb:(cb + 1) * rb, hf * LANES:(hf + 1) * LANES] = unit[g]


def _relayout_from_groups(src_ref, dst_ref, n_chunks, blocks=range(SSM_WIDTH // LANES)):
    half = CHUNK // 2
    rb = RELAYOUT_ROWS
    where = [u for u in _relayout_units(n_chunks) if u[0] in blocks]
    batch = 2 * RELAYOUT_UNITS
    for i in range(0, len(where), batch):
        units = [[src_ref[b * GROUPS_PER_BLOCK + g, cb * rb:(cb + 1) * rb, hf * LANES:(hf + 1) * LANES]
                  for g in range(GROUPS_PER_BLOCK)] for (b, cb, hf) in where[i:i + batch]]
        units = _granule_transpose(units)
        for (b, cb, hf), unit in zip(where[i:i + batch], units):
            for t in range(half):
                dst_ref[b, pl.ds(cb * rb * PARK_PITCH + hf * half + t, rb, stride=PARK_PITCH), :] = unit[t].astype(F32)


def _in_proj_kernel(n_cast, n, x_ref, gain_ref, w_ref, *refs):
    cast_src, (xg_ref, v_ref), cast_dst = refs[:n_cast], refs[n_cast:n_cast + 2], refs[n_cast + 2:-2]
    park = refs[-2:]
    s = pl.program_id(0)

    def step(p_prev, p_next):
        if p_prev is not None:
            _relayout_to_groups(p_prev, xg_ref, x_ref.shape[0] // CHUNK)
        if p_next is not None:
            _cast_slabs(cast_src, cast_dst)
            h = _rms_norm(x_ref[...], gain_ref[...]).astype(BF16)
            p = jnp.dot(h, w_ref[...], preferred_element_type=F32)
            v_ref[...] = p[:, SSM_WIDTH:].astype(v_ref.dtype)
            for b in range(SSM_WIDTH // LANES):
                for c in range(x_ref.shape[0] // CHUNK):
                    p_next[b, c * PARK_PITCH:c * PARK_PITCH + CHUNK, :] = (
                        p[c * CHUNK:(c + 1) * CHUNK, b * LANES:(b + 1) * LANES])

    middle = jnp.logical_and(s > 0, s < n)
    pl.when(s == 0)(lambda: step(None, park[0]))
    pl.when(jnp.logical_and(middle, s % 2 == 0))(lambda: step(park[1], park[0]))
    pl.when(jnp.logical_and(middle, s % 2 == 1))(lambda: step(park[0], park[1]))
    pl.when(s == n)(lambda: step(park[(n - 1) % 2], None))


def _in_proj(x, gain, w_in, cast, tm):
    L = x.shape[0]
    n = L // tm
    c_in, c_out, c_shapes = _cast_specs(cast, n)
    cur = lambda s: (jnp.minimum(s, n - 1), 0)
    return pl.pallas_call(
        functools.partial(_in_proj_kernel, len(cast), n),
        grid=(n + 1,),
        in_specs=[pl.BlockSpec((tm, D_MODEL), cur),
                  pl.BlockSpec((1, D_MODEL), lambda s: (0, 0)),
                  pl.BlockSpec((D_MODEL, D_MODEL), lambda s: (0, 0))] + c_in,
        out_specs=[pl.BlockSpec((None, SSM_GROUPS, tm // CHUNK, CHUNK_LANES), lambda s: (jnp.maximum(s - 1, 0), 0, 0, 0)),
                   pl.BlockSpec((tm, POOL_WIDTH), cur)] + c_out,
        out_shape=[jax.ShapeDtypeStruct((n, SSM_GROUPS, tm // CHUNK, CHUNK_LANES), BF16),
                   jax.ShapeDtypeStruct((L, POOL_WIDTH), BF16)] + c_shapes,
        scratch_shapes=[pltpu.VMEM((SSM_WIDTH // LANES, tm // CHUNK * PARK_PITCH, LANES), F32)] * 2,
        compiler_params=pltpu.CompilerParams(dimension_semantics=("arbitrary",),
                                             vmem_limit_bytes=VMEM_LIMIT_BYTES),
        name="in_proj",
    )(x, gain, w_in, *cast)


def _cmul(ar, ai, br, bi):
    return ar * br - ai * bi, ar * bi + ai * br


def _ssm_kernel(n_cast, xg_ref, m_ref, bw_ref, cw_ref, a_ref, *refs):
    cast_src, yg_ref, cast_dst = refs[:n_cast], refs[n_cast], refs[n_cast + 1:-4]
    fr_ref, fi_ref, cr_ref, ci_ref = refs[-4:]
    _cast_slabs(cast_src, cast_dst)
    n_tiles, _, tile_chunks, _ = xg_ref.shape
    n_chunks = n_tiles * tile_chunks
    sb = SCAN_BLOCK
    n_sb = n_chunks // sb

    blk3 = (n_sb, sb, 2 * SSM_STATE)
    row_in = lax.broadcasted_iota(jnp.int32, blk3, 1)
    brow = lax.broadcasted_iota(jnp.int32, (n_sb, 2 * SSM_STATE), 0)
    for q in range(GROUPS_PER_BLOCK // 2):
        x0 = xg_ref[:, 2 * q].reshape(n_chunks, CHUNK_LANES)
        x1 = xg_ref[:, 2 * q + 1].reshape(n_chunks, CHUNK_LANES)
        xp = jnp.concatenate([x0, x1], axis=1)
        e = lax.dot_general(xp, bw_ref[q], (((1,), (1,)), ((), ())), preferred_element_type=F32)
        a1r, a1i = a_ref[q, 0:1, :], a_ref[q, 1:2, :]

        fr = e[:, :2 * SSM_STATE].reshape(blk3)
        fi = e[:, 2 * SSM_STATE:].reshape(blk3)
        ar, ai = a1r, a1i
        d = 1
        while d < sb:
            tr = jnp.where(row_in >= d, pltpu.roll(fr, d, axis=1), 0.0)
            ti = jnp.where(row_in >= d, pltpu.roll(fi, d, axis=1), 0.0)
            fr, fi = fr + ar * tr - ai * ti, fi + ar * ti + ai * tr
            ar, ai = _cmul(ar, ai, ar, ai)
            d *= 2
        fr_ref[...] = fr.reshape(n_chunks, 2 * SSM_STATE)
        fi_ref[...] = fi.reshape(n_chunks, 2 * SSM_STATE)
        gr = fr_ref[pl.ds(sb - 1, n_sb, stride=sb), :]
        gi = fi_ref[pl.ds(sb - 1, n_sb, stride=sb), :]
        gr = jnp.where(brow >= 1, pltpu.roll(gr, 1, axis=0), 0.0)
        gi = jnp.where(brow >= 1, pltpu.roll(gi, 1, axis=0), 0.0)
        d = 1
        while d < n_sb:
            tr = jnp.where(brow >= d, pltpu.roll(gr, d, axis=0), 0.0)
            ti = jnp.where(brow >= d, pltpu.roll(gi, d, axis=0), 0.0)
            gr, gi = gr + ar * tr - ai * ti, gi + ar * ti + ai * tr
            ar, ai = _cmul(ar, ai, ar, ai)
            d *= 2
        pr, pi = jnp.ones_like(a1r), jnp.zeros_like(a1i)
        for r in range(sb):
            cr_ref[pl.ds(r, n_sb, stride=sb), :] = pr * gr - pi * gi
            ci_ref[pl.ds(r, n_sb, stride=sb), :] = pr * gi + pi * gr
            pr, pi = _cmul(pr, pi, a1r, a1i)
        sr = jnp.where(row_in >= 1, pltpu.roll(fr, 1, axis=1), 0.0).reshape(n_chunks, 2 * SSM_STATE)
        si = jnp.where(row_in >= 1, pltpu.roll(fi, 1, axis=1), 0.0).reshape(n_chunks, 2 * SSM_STATE)
        sr = sr + cr_ref[...]
        si = si + ci_ref[...]
        s = jnp.concatenate([sr, si], axis=1).astype(BF16)
        cw = cw_ref[q]
        for j, xj in enumerate((x0, x1)):
            lhs = jnp.concatenate([xj, s], axis=1)
            rhs = jnp.concatenate([m_ref[2 * q + j], cw[:, j * CHUNK_LANES:(j + 1) * CHUNK_LANES]], axis=0)
            y = jnp.dot(lhs, rhs, preferred_element_type=F32).astype(yg_ref.dtype)
            yg_ref[:, 2 * q + j] = y.reshape(n_tiles, tile_chunks, CHUNK_LANES)


def _ssm(xg, m, bw_pair, cw_pair, a_pair, cast):
    n_tiles, _, tile_chunks, _ = xg.shape
    n_chunks = n_tiles * tile_chunks
    n_blocks = SSM_WIDTH // LANES
    ppb = GROUPS_PER_BLOCK // 2
    c_in, c_out, c_shapes = _cast_specs(cast, n_blocks)
    return pl.pallas_call(
        functools.partial(_ssm_kernel, len(cast)),
        grid=(n_blocks,),
        in_specs=[pl.BlockSpec((n_tiles, GROUPS_PER_BLOCK, tile_chunks, CHUNK_LANES), lambda b: (0, b, 0, 0)),
                  pl.BlockSpec((GROUPS_PER_BLOCK, CHUNK_LANES, CHUNK_LANES), lambda b: (b, 0, 0)),
                  pl.BlockSpec((ppb, 4 * SSM_STATE, 2 * CHUNK_LANES), lambda b: (b, 0, 0)),
                  pl.BlockSpec((ppb, 4 * SSM_STATE, 2 * CHUNK_LANES), lambda b: (b, 0, 0)),
                  pl.BlockSpec((ppb, 2, 2 * SSM_STATE), lambda b: (b, 0, 0))] + c_in,
        out_specs=[pl.BlockSpec((n_tiles, GROUPS_PER_BLOCK, tile_chunks, CHUNK_LANES), lambda b: (0, b, 0, 0))] + c_out,
        out_shape=[jax.ShapeDtypeStruct(xg.shape, BF16)] + c_shapes,
        scratch_shapes=[pltpu.VMEM((n_chunks, 2 * SSM_STATE), F32)] * 4,
        compiler_params=pltpu.CompilerParams(dimension_semantics=("parallel",),
                                             vmem_limit_bytes=VMEM_LIMIT_BYTES),
        name="ssm",
    )(xg, m, bw_pair, cw_pair, a_pair, *cast)


def _out_proj_kernel(n_cast, yg_ref, v_ref, vh_ref, x_ref, wglu_ref, bglu_ref, wpool_ref, bpool_ref,
                     pscale_ref, wtop_ref, wbot_ref, *refs):
    cast_src, o_ref, cast_dst = refs[:n_cast], refs[n_cast], refs[n_cast + 1:-5]
    mix_park, y_ref, vbuf_ref, psum_ref = refs[-5:-3], refs[-3], refs[-2], refs[-1]
    _cast_slabs(cast_src, cast_dst)
    s = pl.program_id(0)
    tm = x_ref.shape[0]

    @pl.when(s == 0)
    def _():
        mix_park[1][...] = jnp.zeros_like(mix_park[1])

    def step(mix_prev, mix_next):
        n_tile = 2 * LANES

        def project(n):
            c = slice(n * n_tile, (n + 1) * n_tile)
            o_ref[:, c] = (x_ref[:, c]
                           + jnp.dot(mix_prev[:, :SSM_WIDTH], wtop_ref[:, c], preferred_element_type=F32)
                           + jnp.dot(mix_prev[:, SSM_WIDTH:], wbot_ref[:, c], preferred_element_type=F32))

        t0 = POOL_PAD + HALO
        n_lb = POOL_GROUP // LANES
        vh = jnp.where(s > 0, vh_ref[...].astype(F32), 0.0)
        vt = v_ref[...].astype(F32)
        for j in range(POOL_WIDTH // LANES):
            vbuf_ref[j, 0:POOL_PAD, :] = jnp.zeros((POOL_PAD, LANES), F32)
            vbuf_ref[j, POOL_PAD:t0, :] = vh[:, j * LANES:(j + 1) * LANES]
            vbuf_ref[j, t0:, :] = vt[:, j * LANES:(j + 1) * LANES]
        t1 = (lax.broadcasted_iota(jnp.int32, (tm, 1), 0) + (s * tm + 1)).astype(F32)

        def window_sum(slab, first, rows, terms, step):
            acc = slab[first:first + rows, :]
            for i in range(1, terms):
                acc = acc + slab[first - i * step:first - i * step + rows, :]
            return acc

        def pool_window(k):
            w = POOL_WINDOWS[k]
            out = []
            for j in range(k * n_lb, (k + 1) * n_lb):
                slab = vbuf_ref.at[j]
                if w <= 4:
                    total = window_sum(slab, t0, tm, w, 1)
                else:
                    part = psum_ref.at[j % (2 * n_lb)]
                    part[POOL_PAD:, :] = window_sum(slab, POOL_PAD, HALO + tm, w // 4, 1)
                    total = window_sum(part, t0, tm, 4, w // 4)
                mean = total / jnp.minimum(t1, float(w))
                out.append(mean - slab[t0:, :])
            return jnp.concatenate(out, axis=1).astype(BF16)

        def gelu(blocks):
            y = jnp.concatenate(
                [jnp.concatenate([y_ref[b, c * PARK_PITCH:c * PARK_PITCH + CHUNK, :] for c in range(tm // CHUNK)], axis=0)
                 for b in blocks], axis=1)
            return 0.5 * y * (1.0 + lax.erf(y * (1.0 / math.sqrt(2.0))))

        half_blocks = SSM_WIDTH // LANES // 2
        pk = [pool_window(0), pool_window(1)]
        project(0)
        pk.append(pool_window(2))
        project(1)
        pk.append(pool_window(3))
        project(2)
        pooled = [jnp.dot(pk[k], wpool_ref[k], preferred_element_type=F32) for k in range(len(POOL_WINDOWS))]
        yp = (jnp.concatenate(pooled, axis=1) + bpool_ref[...]) * pscale_ref[...]
        mix_next[:, SSM_WIDTH:] = yp.astype(BF16)
        _relayout_from_groups(yg_ref, y_ref, tm // CHUNK, blocks=range(0, half_blocks))
        project(3)
        _relayout_from_groups(yg_ref, y_ref, tm // CHUNK, blocks=range(half_blocks, 2 * half_blocks))
        project(4)
        g_a = gelu(range(0, half_blocks))
        project(5)
        g_b = gelu(range(half_blocks, 2 * half_blocks))
        project(6)
        g = jnp.concatenate([g_a, g_b], axis=1)
        z = jnp.dot(g.astype(BF16), wglu_ref[...], preferred_element_type=F32) + bglu_ref[...]
        mix_next[:, :SSM_WIDTH] = (g * (1.0 / (1.0 + jnp.exp(-z)))).astype(BF16)
        project(7)

    pl.when(s % 2 == 0)(lambda: step(mix_park[1], mix_park[0]))
    pl.when(s % 2 == 1)(lambda: step(mix_park[0], mix_park[1]))


def _out_proj(yg, v, x, w_glu, b_glu, w_pool, b_pool, pool_scale, w_out, cast, tm):
    L = x.shape[0]
    n = L // tm
    halo_blocks = tm // HALO
    const = lambda *shape: pl.BlockSpec(shape, lambda s: (0,) * len(shape))
    c_in, c_out, c_shapes = _cast_specs(cast, n)
    cur = lambda s: jnp.minimum(s, n - 1)
    prev = lambda s: (jnp.maximum(s - 1, 0), 0)
    return pl.pallas_call(
        functools.partial(_out_proj_kernel, len(cast)),
        grid=(n + 1,),
        in_specs=[pl.BlockSpec((None, SSM_GROUPS, tm // CHUNK, CHUNK_LANES), lambda s: (cur(s), 0, 0, 0)),
                  pl.BlockSpec((tm, POOL_WIDTH), lambda s: (cur(s), 0)),
                  pl.BlockSpec((HALO, POOL_WIDTH), lambda s: (jnp.maximum(cur(s) * halo_blocks - 1, 0), 0)),
                  pl.BlockSpec((tm, D_MODEL), prev),
                  const(SSM_WIDTH, SSM_WIDTH), const(1, SSM_WIDTH),
                  const(len(POOL_WINDOWS), POOL_GROUP, POOL_GROUP), const(1, POOL_WIDTH),
                  const(1, POOL_WIDTH),
                  pl.BlockSpec((SSM_WIDTH, D_MODEL), lambda s: (0, 0)),
                  pl.BlockSpec((POOL_WIDTH, D_MODEL), lambda s: (1, 0))] + c_in,
        out_specs=[pl.BlockSpec((tm, D_MODEL), prev)] + c_out,
        out_shape=[jax.ShapeDtypeStruct((L, D_MODEL), F32)] + c_shapes,
        scratch_shapes=[pltpu.VMEM((tm, SSM_WIDTH + POOL_WIDTH), BF16)] * 2
                       + [pltpu.VMEM((SSM_WIDTH // LANES, tm // CHUNK * PARK_PITCH, LANES), F32),
                          pltpu.VMEM((POOL_WIDTH // LANES, POOL_PAD + HALO + tm, LANES), F32),
                          pltpu.VMEM((2 * POOL_GROUP // LANES, POOL_PAD + HALO + tm, LANES), F32)],
        compiler_params=pltpu.CompilerParams(dimension_semantics=("arbitrary",),
                                             vmem_limit_bytes=VMEM_LIMIT_BYTES),
        name="out_proj",
    )(yg, v, v, x, w_glu, b_glu, w_pool, b_pool, pool_scale, w_out, w_out, *cast)


def _ffn_kernel(x_ref, gain_ref, wg_ref, wu_ref, wd_ref, gfin_ref, o_ref, h_ref, r_ref):
    j = pl.program_id(1)
    last = pl.num_programs(1) - 1
    lane_tiles = wg_ref.shape[1] // LANES

    def slab(first, final):
        if first:
            x = x_ref[...]
            h = (x * gain_ref[...]).astype(BF16)
            h_ref[...] = h
            r = lax.rsqrt(jnp.mean(x * x, axis=-1, keepdims=True) + EPS)
            r = jnp.broadcast_to(r, r_ref.shape)
            r_ref[...] = r
        else:
            h, r = h_ref[...], r_ref[...]
        rt = jnp.concatenate([r] * lane_tiles, axis=1)
        a = jnp.dot(h, wg_ref[...], preferred_element_type=F32) * rt
        b = jnp.dot(h, wu_ref[...], preferred_element_type=F32) * rt
        ff = (a * (1.0 / (1.0 + jnp.exp(-a))) * b).astype(BF16)
        acc = (x if first else o_ref[...]) + jnp.dot(ff, wd_ref[...], preferred_element_type=F32)
        o_ref[...] = _rms_norm(acc, gfin_ref[...]) if final else acc

    pl.when(j == 0)(lambda: slab(True, False))
    pl.when(jnp.logical_and(j > 0, j < last))(lambda: slab(False, False))
    pl.when(j == last)(lambda: slab(False, True))


def _ffn(x, gain, w_gate, w_up, w_down, gain_final, tm, tf):
    L = x.shape[0]
    d_ff = w_gate.shape[1]
    assert d_ff // tf >= 2
    return pl.pallas_call(
        _ffn_kernel,
        grid=(L // tm, d_ff // tf),
        in_specs=[pl.BlockSpec((tm, D_MODEL), lambda i, j: (i, 0)),
                  pl.BlockSpec((1, D_MODEL), lambda i, j: (0, 0)),
                  pl.BlockSpec((D_MODEL, tf), lambda i, j: (0, j)),
                  pl.BlockSpec((D_MODEL, tf), lambda i, j: (0, j)),
                  pl.BlockSpec((tf, D_MODEL), lambda i, j: (j, 0)),
                  pl.BlockSpec((1, D_MODEL), lambda i, j: (0, 0))],
        out_specs=pl.BlockSpec((tm, D_MODEL), lambda i, j: (i, 0)),
        out_shape=jax.ShapeDtypeStruct((L, D_MODEL), F32),
        scratch_shapes=[pltpu.VMEM((tm, D_MODEL), BF16), pltpu.VMEM((tm, LANES), F32)],
        compiler_params=pltpu.CompilerParams(dimension_semantics=("parallel", "arbitrary"),
                                             vmem_limit_bytes=VMEM_LIMIT_BYTES),
        name="ffn",
    )(x, gain, w_gate, w_up, w_down, gain_final)


def kernel(x, norm_mix, w_in, lambda_re, lambda_im, log_step, b_re, b_im, c_re, c_im, d_skip,
           w_glu, b_glu, w_pool, b_pool, pool_scale, w_out, norm_ffn, w_gate, w_up, w_down,
           norm_final):
    bsz, L, D = x.shape
    assert bsz == 1 and D == D_MODEL and w_in.shape[0] == 1 and L % 1024 == 0
    G, P = SSM_GROUPS, SSM_STATE
    xs = x.reshape(L, D).astype(F32)

    n_pool = len(POOL_WINDOWS)
    m, bw_pair, cw_pair, a_pair, w_in_b, w_out_b, w_glu_b, w_pool_b = _ssm_prep(
        lambda_re[0].astype(F32), lambda_im[0].astype(F32), log_step[0].astype(F32),
        b_re[0].astype(F32), b_im[0].astype(F32), c_re[0].astype(F32), c_im[0].astype(F32),
        d_skip[0].astype(F32),
        cast=[w_in[0], w_out[0], w_glu[0], w_pool[0].reshape(n_pool * POOL_GROUP, POOL_GROUP)])

    xg, v, w_gate_b = _in_proj(xs, norm_mix[0].astype(F32).reshape(1, D), w_in_b, cast=[w_gate[0]],
                               tm=PROJ_ROW_TILE)

    (yg,) = _ssm(xg, m, bw_pair, cw_pair, a_pair, cast=[])

    x1, w_up_b, w_down_b = _out_proj(
        yg, v, xs, w_glu_b, b_glu[0].astype(F32).reshape(1, SSM_WIDTH),
        w_pool_b.reshape(n_pool, POOL_GROUP, POOL_GROUP), b_pool[0].astype(F32).reshape(1, POOL_WIDTH),
        pool_scale[0].astype(F32).reshape(1, POOL_WIDTH), w_out_b, cast=[w_up[0], w_down[0]],
        tm=PROJ_ROW_TILE)

    out = _ffn(x1, norm_ffn[0].astype(F32).reshape(1, D), w_gate_b, w_up_b, w_down_b,
               norm_final.astype(F32).reshape(1, D), tm=FFN_ROW_TILE, tf=FFN_FF_TILE)
    return out.reshape(bsz, L, D).astype(x.dtype)
```

```python
import functools
import math

import jax
import jax.numpy as jnp
from jax import lax
from jax.experimental import pallas as pl
from jax.experimental.pallas import tpu as pltpu

F32 = jnp.float32
BF16 = jnp.bfloat16

D_MODEL = 2048
SSM_WIDTH = 1024
POOL_WIDTH = 1024
SSM_GROUP = 16
SSM_GROUPS = 64
SSM_STATE = 64
POOL_WINDOWS = (2, 4, 8, 16)
POOL_GROUP = 256
EPS = 1e-6

CHUNK = 16
CHUNK_LANES = CHUNK * SSM_GROUP
HALO = 16
POOL_PAD = 8
LANES = 128
GROUPS_PER_BLOCK = LANES // SSM_GROUP
RELAYOUT_ROWS = 16
RELAYOUT_UNITS = 4
SCAN_BLOCK = 8
X_RING = 3
PARK_PITCH = 20

VMEM_LIMIT_BYTES = 60 * 1024 * 1024
PROJ_ROW_TILE = 512
FFN_ROW_TILE = 1024
FFN_FF_TILE = 512

HIGHEST = lax.Precision.HIGHEST


def _rms_norm(x, gain):
    ms = jnp.mean(x * x, axis=-1, keepdims=True)
    return x * lax.rsqrt(ms + EPS) * gain


def _cast_specs(weights, n_steps):
    in_specs, out_specs, out_shapes = [], [], []
    for w in weights:
        rows, cols = w.shape
        slab = pl.BlockSpec((rows // n_steps, cols), lambda i: (jnp.minimum(i, n_steps - 1), 0))
        in_specs.append(slab)
        out_specs.append(slab)
        out_shapes.append(jax.ShapeDtypeStruct((rows, cols), BF16))
    return in_specs, out_specs, out_shapes


def _cast_slabs(src_refs, dst_refs):
    for src, dst in zip(src_refs, dst_refs):
        dst[...] = src[...].astype(dst.dtype)


def _ssm_prep_kernel(n_cast, lre_ref, lim_ref, ls_ref, b_re_ref, b_im_ref, c_re_ref, c_im_ref, d_ref, *refs):
    cast_src, (m_ref, bw_ref, cw_ref, a_ref), cast_dst = refs[:n_cast], refs[n_cast:n_cast + 4], refs[n_cast + 4:]
    _cast_slabs(cast_src, cast_dst)
    P, H, T, GB = SSM_STATE, SSM_GROUP, CHUNK, GROUPS_PER_BLOCK
    lre, lim = lre_ref[...], lim_ref[...]
    step = jnp.exp(ls_ref[...])
    mag = jnp.exp(lre * step)
    lbr, lbi = mag * jnp.cos(lim * step), mag * jnp.sin(lim * step)
    den = lre * lre + lim * lim
    cfr = ((lbr - 1.0) * lre + lbi * lim) / den
    cfi = (lbi * lre - (lbr - 1.0) * lim) / den
    eye = (lax.broadcasted_iota(jnp.int32, (P, P), 0)
           == lax.broadcasted_iota(jnp.int32, (P, P), 1)).astype(F32)
    rows = jnp.concatenate([lbr, lbi, cfr, cfi], axis=0)
    cols = lax.dot_general(eye, rows, (((1,), (1,)), ((), ())), precision=HIGHEST,
                           preferred_element_type=F32)

    lane = lax.broadcasted_iota(jnp.int32, (1, CHUNK_LANES), 1)
    k_idx = lane // H
    sub = lax.broadcasted_iota(jnp.int32, (H, CHUNK_LANES), 0)
    lane_h = lax.broadcasted_iota(jnp.int32, (H, CHUNK_LANES), 1)
    rep = (lane_h % H == sub).astype(F32)

    def tile(a):
        return jnp.dot(a, rep, precision=HIGHEST, preferred_element_type=F32)

    def tile_t(a):
        return lax.dot_general(a, rep, (((0,), (0,)), ((), ())), precision=HIGHEST,
                               preferred_element_type=F32)

    def dot_t(a, b):
        return lax.dot_general(a, b, (((0,), (0,)), ((), ())), precision=HIGHEST,
                               preferred_element_type=F32)

    def cmul(ar, ai, br, bi):
        return ar * br - ai * bi, ar * bi + ai * br

    bw_ref[...] = jnp.zeros_like(bw_ref)
    cw_ref[...] = jnp.zeros_like(cw_ref)
    ar, ai = lbr, lbi
    for _ in range(T.bit_length() - 1):
        ar, ai = cmul(ar, ai, ar, ai)
    for q in range(GB // 2):
        a_ref[q, 0:1, :] = jnp.concatenate([ar[2 * q:2 * q + 1, :], ar[2 * q + 1:2 * q + 2, :]], axis=1)
        a_ref[q, 1:2, :] = jnp.concatenate([ai[2 * q:2 * q + 1, :], ai[2 * q + 1:2 * q + 2, :]], axis=1)
    st = [dict() for _ in range(GB)]
    for g in range(GB):
        col = lambda quantity: cols[:, quantity * GB + g:quantity * GB + g + 1]
        l1r, l1i, cr, ci = col(0), col(1), col(2), col(3)
        sq = [(l1r, l1i)]
        for _ in range(3):
            sq.append(cmul(*sq[-1], *sq[-1]))
        pkr = pki = prr = pri = None
        for bit, (fr, fi) in enumerate(sq):
            on = ((k_idx >> bit) & 1) == 1
            ur, ui = jnp.where(on, fr, 1.0), jnp.where(on, fi, 0.0)
            dr, di = jnp.where(on, 1.0, fr), jnp.where(on, 0.0, fi)
            if bit == 0:
                pkr, pki, prr, pri = ur, ui, dr, di
            else:
                pkr, pki = cmul(pkr, pki, ur, ui)
                prr, pri = cmul(prr, pri, dr, di)
        st[g].update(l1=(l1r, l1i), pk=(pkr, pki), pr=(prr, pri),
                     bb=cmul(cr, ci, b_re_ref[g], b_im_ref[g]))
    for g in range(GB):
        st[g]["tb"] = (tile(st[g]["bb"][0]), tile(st[g]["bb"][1]))
        st[g]["tc"] = (tile_t(c_re_ref[g]), tile_t(c_im_ref[g]))
    for g in range(GB):
        v = st[g]
        q, j = g // 2, g % 2
        r_re = slice(j * P, (j + 1) * P)
        r_im = slice((2 + j) * P, (3 + j) * P)
        c_g = slice(j * CHUNK_LANES, (j + 1) * CHUNK_LANES)
        w_re, w_im = cmul(*v["pr"], *v["tb"])
        bw_ref[q, r_re, c_g] = w_re.astype(bw_ref.dtype)
        bw_ref[q, r_im, c_g] = w_im.astype(bw_ref.dtype)
        g_re, g_im = cmul(*v["pk"], *v["tc"])
        g1_re, g1_im = cmul(*v["l1"], g_re, g_im)
        cw_ref[q, r_re, c_g] = g1_re.astype(cw_ref.dtype)
        cw_ref[q, r_im, c_g] = (-g1_im).astype(cw_ref.dtype)
        v["g"] = (g_re, g_im)
    for g in range(GB):
        v = st[g]
        kt = dot_t(v["bb"][0], v["g"][0]) - dot_t(v["bb"][1], v["g"][1])
        v["kt"] = kt + jnp.where(lane_h == sub, d_ref[g], 0.0)
    for g in range(GB):
        kt = st[g]["kt"]
        for tau in range(T):
            blk = kt if tau == 0 else jnp.where(lane_h >= tau * H, pltpu.roll(kt, tau * H, axis=1), 0.0)
            m_ref[g, tau * H:(tau + 1) * H, :] = blk.astype(m_ref.dtype)


def _ssm_prep(lambda_re, lambda_im, log_step, b_re, b_im, c_re, c_im, d_skip, cast):
    G, P, H, GB = SSM_GROUPS, SSM_STATE, SSM_GROUP, GROUPS_PER_BLOCK
    n_steps = G // GB
    blk = lambda *shape: pl.BlockSpec(shape, lambda i: (i,) + (0,) * (len(shape) - 1))
    c_in, c_out, c_shapes = _cast_specs(cast, n_steps)
    return pl.pallas_call(
        functools.partial(_ssm_prep_kernel, len(cast)),
        grid=(n_steps,),
        in_specs=[blk(GB, P), blk(GB, P), blk(GB, 1),
                  blk(GB, P, H), blk(GB, P, H), blk(GB, H, P), blk(GB, H, P), blk(GB, H, 1)] + c_in,
        out_specs=[blk(GB, CHUNK_LANES, CHUNK_LANES), blk(GB // 2, 4 * P, 2 * CHUNK_LANES),
                   blk(GB // 2, 4 * P, 2 * CHUNK_LANES), blk(GB // 2, 2, 2 * P)] + c_out,
        out_shape=[jax.ShapeDtypeStruct((G, CHUNK_LANES, CHUNK_LANES), BF16),
                   jax.ShapeDtypeStruct((G // 2, 4 * P, 2 * CHUNK_LANES), BF16),
                   jax.ShapeDtypeStruct((G // 2, 4 * P, 2 * CHUNK_LANES), BF16),
                   jax.ShapeDtypeStruct((G // 2, 2, 2 * P), F32)] + c_shapes,
        compiler_params=pltpu.CompilerParams(dimension_semantics=("parallel",),
                                             vmem_limit_bytes=VMEM_LIMIT_BYTES),
        name="ssm_prep",
    )(lambda_re, lambda_im, log_step.reshape(G, 1),
      b_re, b_im, c_re, c_im, d_skip.reshape(G, H, 1), *cast)


def _granule_transpose(units):
    lane = lax.broadcasted_iota(jnp.int32, units[0][0].shape, 1)
    units = [list(a) for a in units]
    for s in (4, 2, 1):
        upper = ((lane // SSM_GROUP) & s) != 0
        for a in units:
            for i in range(GROUPS_PER_BLOCK):
                if i & s:
                    continue
                lo, hi = a[i], a[i + s]
                a[i] = jnp.where(upper, pltpu.roll(hi, s * SSM_GROUP, axis=1), lo)
                a[i + s] = jnp.where(upper, hi, pltpu.roll(lo, LANES - s * SSM_GROUP, axis=1))
    return units


def _relayout_units(n_chunks):
    return [(b, cb, hf) for b in range(SSM_WIDTH // LANES)
            for cb in range(n_chunks // RELAYOUT_ROWS) for hf in range(2)]


def _relayout_to_groups(src_ref, dst_ref, n_chunks):
    half = CHUNK // 2
    rb = RELAYOUT_ROWS
    where = _relayout_units(n_chunks)
    batch = 2 * RELAYOUT_UNITS
    for i in range(0, len(where), batch):
        units = [[src_ref[b, pl.ds(cb * rb * PARK_PITCH + hf * half + t, rb, stride=PARK_PITCH), :].astype(BF16)
                  for t in range(half)] for (b, cb, hf) in where[i:i + batch]]
        units = _granule_transpose(units)
        for (b, cb, hf), unit in zip(where[i:i + batch], units):
            for g in range(GROUPS_PER_BLOCK):
                dst_ref[b * GROUPS_PER_BLOCK + g, cb * rb:(cb + 1) * rb, hf * LANES:(hf + 1) * LANES] = unit[g]


def _relayout_from_groups(src_ref, dst_ref, n_chunks, blocks=range(SSM_WIDTH // LANES)):
    half = CHUNK // 2
    rb = RELAYOUT_ROWS
    where = [u for u in _relayout_units(n_chunks) if u[0] in blocks]
    batch = 2 * RELAYOUT_UNITS
    for i in range(0, len(where), batch):
        units = [[src_ref[b * GROUPS_PER_BLOCK + g, cb * rb:(cb + 1) * rb, hf * LANES:(hf + 1) * LANES]
                  for g in range(GROUPS_PER_BLOCK)] for (b, cb, hf) in where[i:i + batch]]
        units = _granule_transpose(units)
        for (b, cb, hf), unit in zip(where[i:i + batch], units):
            for t in range(half):
                dst_ref[b, pl.ds(cb * rb * PARK_PITCH + hf * half + t, rb, stride=PARK_PITCH), :] = unit[t].astype(F32)


def _in_proj_kernel(n_cast, n, x_ref, gain_ref, w_ref, *refs):
    cast_src, (xg_ref, v_ref), cast_dst = refs[:n_cast], refs[n_cast:n_cast + 2], refs[n_cast + 2:-2]
    park = refs[-2:]
    s = pl.program_id(0)

    def step(p_prev, p_next):
        if p_prev is not None:
            _relayout_to_groups(p_prev, xg_ref, x_ref.shape[0] // CHUNK)
        if p_next is not None:
            _cast_slabs(cast_src, cast_dst)
            h = _rms_norm(x_ref[...], gain_ref[...]).astype(BF16)
            p = jnp.dot(h, w_ref[...], preferred_element_type=F32)
            v_ref[...] = p[:, SSM_WIDTH:].astype(v_ref.dtype)
            for b in range(SSM_WIDTH // LANES):
                for c in range(x_ref.shape[0] // CHUNK):
                    p_next[b, c * PARK_PITCH:c * PARK_PITCH + CHUNK, :] = (
                        p[c * CHUNK:(c + 1) * CHUNK, b * LANES:(b + 1) * LANES])

    middle = jnp.logical_and(s > 0, s < n)
    pl.when(s == 0)(lambda: step(None, park[0]))
    pl.when(jnp.logical_and(middle, s % 2 == 0))(lambda: step(park[1], park[0]))
    pl.when(jnp.logical_and(middle, s % 2 == 1))(lambda: step(park[0], park[1]))
    pl.when(s == n)(lambda: step(park[(n - 1) % 2], None))


def _in_proj(x, gain, w_in, cast, tm):
    L = x.shape[0]
    n = L // tm
    c_in, c_out, c_shapes = _cast_specs(cast, n)
    cur = lambda s: (jnp.minimum(s, n - 1), 0)
    return pl.pallas_call(
        functools.partial(_in_proj_kernel, len(cast), n),
        grid=(n + 1,),
        in_specs=[pl.BlockSpec((tm, D_MODEL), cur),
                  pl.BlockSpec((1, D_MODEL), lambda s: (0, 0)),
                  pl.BlockSpec((D_MODEL, D_MODEL), lambda s: (0, 0))] + c_in,
        out_specs=[pl.BlockSpec((SSM_GROUPS, tm // CHUNK, CHUNK_LANES), lambda s: (0, jnp.maximum(s - 1, 0), 0)),
                   pl.BlockSpec((tm, POOL_WIDTH), cur)] + c_out,
        out_shape=[jax.ShapeDtypeStruct((SSM_GROUPS, L // CHUNK, CHUNK_LANES), BF16),
                   jax.ShapeDtypeStruct((L, POOL_WIDTH), BF16)] + c_shapes,
        scratch_shapes=[pltpu.VMEM((SSM_WIDTH // LANES, tm // CHUNK * PARK_PITCH, LANES), F32)] * 2,
        compiler_params=pltpu.CompilerParams(dimension_semantics=("arbitrary",),
                                             vmem_limit_bytes=VMEM_LIMIT_BYTES),
        name="in_proj",
    )(x, gain, w_in, *cast)


def _cmul(ar, ai, br, bi):
    return ar * br - ai * bi, ar * bi + ai * br


def _ssm_kernel(n_cast, xg_ref, m_ref, bw_ref, cw_ref, a_ref, *refs):
    cast_src, yg_ref, cast_dst = refs[:n_cast], refs[n_cast], refs[n_cast + 1:-4]
    fr_ref, fi_ref, cr_ref, ci_ref = refs[-4:]
    _cast_slabs(cast_src, cast_dst)
    n_chunks = xg_ref.shape[1]
    sb = SCAN_BLOCK
    n_sb = n_chunks // sb

    blk3 = (n_sb, sb, 2 * SSM_STATE)
    row_in = lax.broadcasted_iota(jnp.int32, blk3, 1)
    brow = lax.broadcasted_iota(jnp.int32, (n_sb, 2 * SSM_STATE), 0)
    for q in range(GROUPS_PER_BLOCK // 2):
        x0, x1 = xg_ref[2 * q], xg_ref[2 * q + 1]
        xp = jnp.concatenate([x0, x1], axis=1)
        e = lax.dot_general(xp, bw_ref[q], (((1,), (1,)), ((), ())), preferred_element_type=F32)
        a1r, a1i = a_ref[q, 0:1, :], a_ref[q, 1:2, :]

        fr = e[:, :2 * SSM_STATE].reshape(blk3)
        fi = e[:, 2 * SSM_STATE:].reshape(blk3)
        ar, ai = a1r, a1i
        d = 1
        while d < sb:
            tr = jnp.where(row_in >= d, pltpu.roll(fr, d, axis=1), 0.0)
            ti = jnp.where(row_in >= d, pltpu.roll(fi, d, axis=1), 0.0)
            fr, fi = fr + ar * tr - ai * ti, fi + ar * ti + ai * tr
            ar, ai = _cmul(ar, ai, ar, ai)
            d *= 2
        fr_ref[...] = fr.reshape(n_chunks, 2 * SSM_STATE)
        fi_ref[...] = fi.reshape(n_chunks, 2 * SSM_STATE)
        gr = fr_ref[pl.ds(sb - 1, n_sb, stride=sb), :]
        gi = fi_ref[pl.ds(sb - 1, n_sb, stride=sb), :]
        gr = jnp.where(brow >= 1, pltpu.roll(gr, 1, axis=0), 0.0)
        gi = jnp.where(brow >= 1, pltpu.roll(gi, 1, axis=0), 0.0)
        d = 1
        while d < n_sb:
            tr = jnp.where(brow >= d, pltpu.roll(gr, d, axis=0), 0.0)
            ti = jnp.where(brow >= d, pltpu.roll(gi, d, axis=0), 0.0)
            gr, gi = gr + ar * tr - ai * ti, gi + ar * ti + ai * tr
            ar, ai = _cmul(ar, ai, ar, ai)
            d *= 2
        pr, pi = jnp.ones_like(a1r), jnp.zeros_like(a1i)
        for r in range(sb):
            cr_ref[pl.ds(r, n_sb, stride=sb), :] = pr * gr - pi * gi
            ci_ref[pl.ds(r, n_sb, stride=sb), :] = pr * gi + pi * gr
            pr, pi = _cmul(pr, pi, a1r, a1i)
        sr = jnp.where(row_in >= 1, pltpu.roll(fr, 1, axis=1), 0.0).reshape(n_chunks, 2 * SSM_STATE)
        si = jnp.where(row_in >= 1, pltpu.roll(fi, 1, axis=1), 0.0).reshape(n_chunks, 2 * SSM_STATE)
        sr = sr + cr_ref[...]
        si = si + ci_ref[...]
        s = jnp.concatenate([sr, si], axis=1).astype(BF16)
        cw = cw_ref[q]
        for j, xj in enumerate((x0, x1)):
            lhs = jnp.concatenate([xj, s], axis=1)
            rhs = jnp.concatenate([m_ref[2 * q + j], cw[:, j * CHUNK_LANES:(j + 1) * CHUNK_LANES]], axis=0)
            yg_ref[2 * q + j] = jnp.dot(lhs, rhs, preferred_element_type=F32).astype(yg_ref.dtype)


def _ssm(xg, m, bw_pair, cw_pair, a_pair, cast):
    n_chunks = xg.shape[1]
    n_blocks = SSM_WIDTH // LANES
    ppb = GROUPS_PER_BLOCK // 2
    c_in, c_out, c_shapes = _cast_specs(cast, n_blocks)
    return pl.pallas_call(
        functools.partial(_ssm_kernel, len(cast)),
        grid=(n_blocks,),
        in_specs=[pl.BlockSpec((GROUPS_PER_BLOCK, n_chunks, CHUNK_LANES), lambda b: (b, 0, 0)),
                  pl.BlockSpec((GROUPS_PER_BLOCK, CHUNK_LANES, CHUNK_LANES), lambda b: (b, 0, 0)),
                  pl.BlockSpec((ppb, 4 * SSM_STATE, 2 * CHUNK_LANES), lambda b: (b, 0, 0)),
                  pl.BlockSpec((ppb, 4 * SSM_STATE, 2 * CHUNK_LANES), lambda b: (b, 0, 0)),
                  pl.BlockSpec((ppb, 2, 2 * SSM_STATE), lambda b: (b, 0, 0))] + c_in,
        out_specs=[pl.BlockSpec((GROUPS_PER_BLOCK, n_chunks, CHUNK_LANES), lambda b: (b, 0, 0))] + c_out,
        out_shape=[jax.ShapeDtypeStruct(xg.shape, BF16)] + c_shapes,
        scratch_shapes=[pltpu.VMEM((n_chunks, 2 * SSM_STATE), F32)] * 4,
        compiler_params=pltpu.CompilerParams(dimension_semantics=("parallel",),
                                             vmem_limit_bytes=VMEM_LIMIT_BYTES),
        name="ssm",
    )(xg, m, bw_pair, cw_pair, a_pair, *cast)


def _out_proj_kernel(n_cast, n_tiles, yg_ref, v_ref, vh_ref, x_hbm, wglu_ref, bglu_ref, wpool_ref, bpool_ref,
                     pscale_ref, wtop_ref, wbot_ref, *refs):
    cast_src, o_ref, cast_dst = refs[:n_cast], refs[n_cast], refs[n_cast + 1:-7]
    mix_park, y_ref, vbuf_ref, psum_ref = refs[-7:-5], refs[-5], refs[-4], refs[-3]
    xbuf_ref, xsem = refs[-2], refs[-1]
    _cast_slabs(cast_src, cast_dst)
    s = pl.program_id(0)
    tm = o_ref.shape[0]

    def x_copy(t):
        slot = t % X_RING
        return pltpu.make_async_copy(x_hbm.at[pl.ds(pl.multiple_of(t * tm, tm), tm), :], xbuf_ref.at[slot],
                                     xsem.at[slot])

    @pl.when(s == 0)
    def _():
        x_copy(0).start()
        if n_tiles > 1:
            x_copy(1).start()
        x_copy(0).wait()

    @pl.when(jnp.logical_and(s >= 1, s + 1 <= n_tiles - 1))
    def _():
        x_copy(s + 1).start()

    @pl.when(s >= 2)
    def _():
        x_copy(s - 1).wait()

    x_ref = xbuf_ref.at[jnp.maximum(s - 1, 0) % X_RING]

    @pl.when(s == 0)
    def _():
        mix_park[1][...] = jnp.zeros_like(mix_park[1])

    def step(mix_prev, mix_next):
        n_tile = 2 * LANES

        def project(n):
            c = slice(n * n_tile, (n + 1) * n_tile)
            o_ref[:, c] = (x_ref[:, c]
                           + jnp.dot(mix_prev[:, :SSM_WIDTH], wtop_ref[:, c], preferred_element_type=F32)
                           + jnp.dot(mix_prev[:, SSM_WIDTH:], wbot_ref[:, c], preferred_element_type=F32))

        t0 = POOL_PAD + HALO
        n_lb = POOL_GROUP // LANES
        vh = jnp.where(s > 0, vh_ref[...].astype(F32), 0.0)
        vt = v_ref[...].astype(F32)
        for j in range(POOL_WIDTH // LANES):
            vbuf_ref[j, 0:POOL_PAD, :] = jnp.zeros((POOL_PAD, LANES), F32)
            vbuf_ref[j, POOL_PAD:t0, :] = vh[:, j * LANES:(j + 1) * LANES]
            vbuf_ref[j, t0:, :] = vt[:, j * LANES:(j + 1) * LANES]
        t1 = (lax.broadcasted_iota(jnp.int32, (tm, 1), 0) + (s * tm + 1)).astype(F32)

        def window_sum(slab, first, rows, terms, step):
            acc = slab[first:first + rows, :]
            for i in range(1, terms):
                acc = acc + slab[first - i * step:first - i * step + rows, :]
            return acc

        def pool_window(k):
            w = POOL_WINDOWS[k]
            out = []
            for j in range(k * n_lb, (k + 1) * n_lb):
                slab = vbuf_ref.at[j]
                if w <= 4:
                    total = window_sum(slab, t0, tm, w, 1)
                else:
                    part = psum_ref.at[j % (2 * n_lb)]
                    part[POOL_PAD:, :] = window_sum(slab, POOL_PAD, HALO + tm, w // 4, 1)
                    total = window_sum(part, t0, tm, 4, w // 4)
                mean = total / jnp.minimum(t1, float(w))
                out.append(mean - slab[t0:, :])
            return jnp.concatenate(out, axis=1).astype(BF16)

        def gelu(blocks):
            y = jnp.concatenate(
                [jnp.concatenate([y_ref[b, c * PARK_PITCH:c * PARK_PITCH + CHUNK, :] for c in range(tm // CHUNK)], axis=0)
                 for b in blocks], axis=1)
            return 0.5 * y * (1.0 + lax.erf(y * (1.0 / math.sqrt(2.0))))

        half_blocks = SSM_WIDTH // LANES // 2
        pk = [pool_window(0), pool_window(1)]
        project(0)
        pk.append(pool_window(2))
        project(1)
        pk.append(pool_window(3))
        project(2)
        pooled = [jnp.dot(pk[k], wpool_ref[k], preferred_element_type=F32) for k in range(len(POOL_WINDOWS))]
        yp = (jnp.concatenate(pooled, axis=1) + bpool_ref[...]) * pscale_ref[...]
        mix_next[:, SSM_WIDTH:] = yp.astype(BF16)
        _relayout_from_groups(yg_ref, y_ref, tm // CHUNK, blocks=range(0, half_blocks))
        project(3)
        _relayout_from_groups(yg_ref, y_ref, tm // CHUNK, blocks=range(half_blocks, 2 * half_blocks))
        project(4)
        g_a = gelu(range(0, half_blocks))
        project(5)
        g_b = gelu(range(half_blocks, 2 * half_blocks))
        project(6)
        g = jnp.concatenate([g_a, g_b], axis=1)
        z = jnp.dot(g.astype(BF16), wglu_ref[...], preferred_element_type=F32) + bglu_ref[...]
        mix_next[:, :SSM_WIDTH] = (g * (1.0 / (1.0 + jnp.exp(-z)))).astype(BF16)
        project(7)

    pl.when(s % 2 == 0)(lambda: step(mix_park[1], mix_park[0]))
    pl.when(s % 2 == 1)(lambda: step(mix_park[0], mix_park[1]))


def _out_proj(yg, v, x, w_glu, b_glu, w_pool, b_pool, pool_scale, w_out, cast, tm):
    L = x.shape[0]
    n = L // tm
    halo_blocks = tm // HALO
    const = lambda *shape: pl.BlockSpec(shape, lambda s: (0,) * len(shape))
    c_in, c_out, c_shapes = _cast_specs(cast, n)
    cur = lambda s: jnp.minimum(s, n - 1)
    prev = lambda s: (jnp.maximum(s - 1, 0), 0)
    return pl.pallas_call(
        functools.partial(_out_proj_kernel, len(cast), n),
        grid=(n + 1,),
        in_specs=[pl.BlockSpec((SSM_GROUPS, tm // CHUNK, CHUNK_LANES), lambda s: (0, cur(s), 0)),
                  pl.BlockSpec((tm, POOL_WIDTH), lambda s: (cur(s), 0)),
                  pl.BlockSpec((HALO, POOL_WIDTH), lambda s: (jnp.maximum(cur(s) * halo_blocks - 1, 0), 0)),
                  pl.BlockSpec(memory_space=pl.ANY),
                  const(SSM_WIDTH, SSM_WIDTH), const(1, SSM_WIDTH),
                  const(len(POOL_WINDOWS), POOL_GROUP, POOL_GROUP), const(1, POOL_WIDTH),
                  const(1, POOL_WIDTH),
                  pl.BlockSpec((SSM_WIDTH, D_MODEL), lambda s: (0, 0)),
                  pl.BlockSpec((POOL_WIDTH, D_MODEL), lambda s: (1, 0))] + c_in,
        out_specs=[pl.BlockSpec((tm, D_MODEL), prev)] + c_out,
        out_shape=[jax.ShapeDtypeStruct((L, D_MODEL), F32)] + c_shapes,
        scratch_shapes=[pltpu.VMEM((tm, SSM_WIDTH + POOL_WIDTH), BF16)] * 2
                       + [pltpu.VMEM((SSM_WIDTH // LANES, tm // CHUNK * PARK_PITCH, LANES), F32),
                          pltpu.VMEM((POOL_WIDTH // LANES, POOL_PAD + HALO + tm, LANES), F32),
                          pltpu.VMEM((2 * POOL_GROUP // LANES, POOL_PAD + HALO + tm, LANES), F32),
                          pltpu.VMEM((X_RING, tm, D_MODEL), F32), pltpu.SemaphoreType.DMA((X_RING,))],
        compiler_params=pltpu.CompilerParams(dimension_semantics=("arbitrary",),
                                             vmem_limit_bytes=VMEM_LIMIT_BYTES),
        name="out_proj",
    )(yg, v, v, x, w_glu, b_glu, w_pool, b_pool, pool_scale, w_out, w_out, *cast)


def _ffn_kernel(x_ref, gain_ref, wg_ref, wu_ref, wd_ref, gfin_ref, o_ref, h_ref, r_ref):
    j = pl.program_id(1)
    last = pl.num_programs(1) - 1
    lane_tiles = wg_ref.shape[1] // LANES

    def slab(first, final):
        if first:
            x = x_ref[...]
            h = (x * gain_ref[...]).astype(BF16)
            h_ref[...] = h
            r = lax.rsqrt(jnp.mean(x * x, axis=-1, keepdims=True) + EPS)
            r = jnp.broadcast_to(r, r_ref.shape)
            r_ref[...] = r
        else:
            h, r = h_ref[...], r_ref[...]
        rt = jnp.concatenate([r] * lane_tiles, axis=1)
        a = jnp.dot(h, wg_ref[...], preferred_element_type=F32) * rt
        b = jnp.dot(h, wu_ref[...], preferred_element_type=F32) * rt
        ff = (a * (1.0 / (1.0 + jnp.exp(-a))) * b).astype(BF16)
        acc = (x if first else o_ref[...]) + jnp.dot(ff, wd_ref[...], preferred_element_type=F32)
        o_ref[...] = _rms_norm(acc, gfin_ref[...]) if final else acc

    pl.when(j == 0)(lambda: slab(True, False))
    pl.when(jnp.logical_and(j > 0, j < last))(lambda: slab(False, False))
    pl.when(j == last)(lambda: slab(False, True))


def _ffn(x, gain, w_gate, w_up, w_down, gain_final, tm, tf):
    L = x.shape[0]
    d_ff = w_gate.shape[1]
    assert d_ff // tf >= 2
    return pl.pallas_call(
        _ffn_kernel,
        grid=(L // tm, d_ff // tf),
        in_specs=[pl.BlockSpec((tm, D_MODEL), lambda i, j: (i, 0)),
                  pl.BlockSpec((1, D_MODEL), lambda i, j: (0, 0)),
                  pl.BlockSpec((D_MODEL, tf), lambda i, j: (0, j)),
                  pl.BlockSpec((D_MODEL, tf), lambda i, j: (0, j)),
                  pl.BlockSpec((tf, D_MODEL), lambda i, j: (j, 0)),
                  pl.BlockSpec((1, D_MODEL), lambda i, j: (0, 0))],
        out_specs=pl.BlockSpec((tm, D_MODEL), lambda i, j: (i, 0)),
        out_shape=jax.ShapeDtypeStruct((L, D_MODEL), F32),
        scratch_shapes=[pltpu.VMEM((tm, D_MODEL), BF16), pltpu.VMEM((tm, LANES), F32)],
        compiler_params=pltpu.CompilerParams(dimension_semantics=("parallel", "arbitrary"),
                                             vmem_limit_bytes=VMEM_LIMIT_BYTES),
        name="ffn",
    )(x, gain, w_gate, w_up, w_down, gain_final)


def kernel(x, norm_mix, w_in, lambda_re, lambda_im, log_step, b_re, b_im, c_re, c_im, d_skip,
           w_glu, b_glu, w_pool, b_pool, pool_scale, w_out, norm_ffn, w_gate, w_up, w_down,
           norm_final):
    bsz, L, D = x.shape
    assert bsz == 1 and D == D_MODEL and w_in.shape[0] == 1 and L % 1024 == 0
    G, P = SSM_GROUPS, SSM_STATE
    xs = x.reshape(L, D).astype(F32)

    n_pool = len(POOL_WINDOWS)
    m, bw_pair, cw_pair, a_pair, w_in_b, w_out_b, w_glu_b, w_pool_b = _ssm_prep(
        lambda_re[0].astype(F32), lambda_im[0].astype(F32), log_step[0].astype(F32),
        b_re[0].astype(F32), b_im[0].astype(F32), c_re[0].astype(F32), c_im[0].astype(F32),
        d_skip[0].astype(F32),
        cast=[w_in[0], w_out[0], w_glu[0], w_pool[0].reshape(n_pool * POOL_GROUP, POOL_GROUP)])

    xg, v, w_gate_b = _in_proj(xs, norm_mix[0].astype(F32).reshape(1, D), w_in_b, cast=[w_gate[0]],
                               tm=PROJ_ROW_TILE)

    (yg,) = _ssm(xg, m, bw_pair, cw_pair, a_pair, cast=[])

    x1, w_up_b, w_down_b = _out_proj(
        yg, v, xs, w_glu_b, b_glu[0].astype(F32).reshape(1, SSM_WIDTH),
        w_pool_b.reshape(n_pool, POOL_GROUP, POOL_GROUP), b_pool[0].astype(F32).reshape(1, POOL_WIDTH),
        pool_scale[0].astype(F32).reshape(1, POOL_WIDTH), w_out_b, cast=[w_up[0], w_down[0]],
        tm=PROJ_ROW_TILE)

    out = _ffn(x1, norm_ffn[0].astype(F32).reshape(1, D), w_gate_b, w_up_b, w_down_b,
               norm_final.astype(F32).reshape(1, D), tm=FFN_ROW_TILE, tf=FFN_FF_TILE)
    return out.reshape(bsz, L, D).astype(x.dtype)
```

```python
import functools
import math

import jax
import jax.numpy as jnp
from jax import lax
from jax.experimental import pallas as pl
from jax.experimental.pallas import tpu as pltpu

F32 = jnp.float32
BF16 = jnp.bfloat16

D_MODEL = 2048
SSM_WIDTH = 1024
POOL_WIDTH = 1024
SSM_GROUP = 16
SSM_GROUPS = 64
SSM_STATE = 64
POOL_WINDOWS = (2, 4, 8, 16)
POOL_GROUP = 256
EPS = 1e-6

CHUNK = 16
CHUNK_LANES = CHUNK * SSM_GROUP
HALO = 16
POOL_PAD = 8
LANES = 128
GROUPS_PER_BLOCK = LANES // SSM_GROUP
RELAYOUT_ROWS = 16
RELAYOUT_UNITS = 4
SCAN_BLOCK = 8
PARK_PITCH = 20

VMEM_LIMIT_BYTES = 60 * 1024 * 1024
PROJ_ROW_TILE = 512
FFN_ROW_TILE = 1024
FFN_FF_TILE = 512

HIGHEST = lax.Precision.HIGHEST


def _rms_norm(x, gain):
    ms = jnp.mean(x * x, axis=-1, keepdims=True)
    return x * lax.rsqrt(ms + EPS) * gain


def _cast_specs(weights, n_steps):
    in_specs, out_specs, out_shapes = [], [], []
    for w in weights:
        rows, cols = w.shape
        slab = pl.BlockSpec((rows // n_steps, cols), lambda i: (jnp.minimum(i, n_steps - 1), 0))
        in_specs.append(slab)
        out_specs.append(slab)
        out_shapes.append(jax.ShapeDtypeStruct((rows, cols), BF16))
    return in_specs, out_specs, out_shapes


def _cast_slabs(src_refs, dst_refs):
    for src, dst in zip(src_refs, dst_refs):
        dst[...] = src[...].astype(dst.dtype)


def _ssm_prep_kernel(n_cast, lre_ref, lim_ref, ls_ref, b_re_ref, b_im_ref, c_re_ref, c_im_ref, d_ref, *refs):
    cast_src, (m_ref, bw_ref, cw_ref, a_ref), cast_dst = refs[:n_cast], refs[n_cast:n_cast + 4], refs[n_cast + 4:]
    _cast_slabs(cast_src, cast_dst)
    P, H, T, GB = SSM_STATE, SSM_GROUP, CHUNK, GROUPS_PER_BLOCK
    lre, lim = lre_ref[...], lim_ref[...]
    step = jnp.exp(ls_ref[...])
    mag = jnp.exp(lre * step)
    lbr, lbi = mag * jnp.cos(lim * step), mag * jnp.sin(lim * step)
    den = lre * lre + lim * lim
    cfr = ((lbr - 1.0) * lre + lbi * lim) / den
    cfi = (lbi * lre - (lbr - 1.0) * lim) / den
    eye = (lax.broadcasted_iota(jnp.int32, (P, P), 0)
           == lax.broadcasted_iota(jnp.int32, (P, P), 1)).astype(F32)
    rows = jnp.concatenate([lbr, lbi, cfr, cfi], axis=0)
    cols = lax.dot_general(eye, rows, (((1,), (1,)), ((), ())), precision=HIGHEST,
                           preferred_element_type=F32)

    lane = lax.broadcasted_iota(jnp.int32, (1, CHUNK_LANES), 1)
    k_idx = lane // H
    sub = lax.broadcasted_iota(jnp.int32, (H, CHUNK_LANES), 0)
    lane_h = lax.broadcasted_iota(jnp.int32, (H, CHUNK_LANES), 1)
    rep = (lane_h % H == sub).astype(F32)

    def tile(a):
        return jnp.dot(a, rep, precision=HIGHEST, preferred_element_type=F32)

    def tile_t(a):
        return lax.dot_general(a, rep, (((0,), (0,)), ((), ())), precision=HIGHEST,
                               preferred_element_type=F32)

    def dot_t(a, b):
        return lax.dot_general(a, b, (((0,), (0,)), ((), ())), precision=HIGHEST,
                               preferred_element_type=F32)

    def cmul(ar, ai, br, bi):
        return ar * br - ai * bi, ar * bi + ai * br

    bw_ref[...] = jnp.zeros_like(bw_ref)
    cw_ref[...] = jnp.zeros_like(cw_ref)
    ar, ai = lbr, lbi
    for _ in range(T.bit_length() - 1):
        ar, ai = cmul(ar, ai, ar, ai)
    for q in range(GB // 2):
        a_ref[q, 0:1, :] = jnp.concatenate([ar[2 * q:2 * q + 1, :], ar[2 * q + 1:2 * q + 2, :]], axis=1)
        a_ref[q, 1:2, :] = jnp.concatenate([ai[2 * q:2 * q + 1, :], ai[2 * q + 1:2 * q + 2, :]], axis=1)
    st = [dict() for _ in range(GB)]
    for g in range(GB):
        col = lambda quantity: cols[:, quantity * GB + g:quantity * GB + g + 1]
        l1r, l1i, cr, ci = col(0), col(1), col(2), col(3)
        sq = [(l1r, l1i)]
        for _ in range(3):
            sq.append(cmul(*sq[-1], *sq[-1]))
        pkr = pki = prr = pri = None
        for bit, (fr, fi) in enumerate(sq):
            on = ((k_idx >> bit) & 1) == 1
            ur, ui = jnp.where(on, fr, 1.0), jnp.where(on, fi, 0.0)
            dr, di = jnp.where(on, 1.0, fr), jnp.where(on, 0.0, fi)
            if bit == 0:
                pkr, pki, prr, pri = ur, ui, dr, di
            else:
                pkr, pki = cmul(pkr, pki, ur, ui)
                prr, pri = cmul(prr, pri, dr, di)
        st[g].update(l1=(l1r, l1i), pk=(pkr, pki), pr=(prr, pri),
                     bb=cmul(cr, ci, b_re_ref[g], b_im_ref[g]))
    for g in range(GB):
        st[g]["tb"] = (tile(st[g]["bb"][0]), tile(st[g]["bb"][1]))
        st[g]["tc"] = (tile_t(c_re_ref[g]), tile_t(c_im_ref[g]))
    for g in range(GB):
        v = st[g]
        q, j = g // 2, g % 2
        r_re = slice(j * P, (j + 1) * P)
        r_im = slice((2 + j) * P, (3 + j) * P)
        c_g = slice(j * CHUNK_LANES, (j + 1) * CHUNK_LANES)
        w_re, w_im = cmul(*v["pr"], *v["tb"])
        bw_ref[q, r_re, c_g] = w_re.astype(bw_ref.dtype)
        bw_ref[q, r_im, c_g] = w_im.astype(bw_ref.dtype)
        g_re, g_im = cmul(*v["pk"], *v["tc"])
        g1_re, g1_im = cmul(*v["l1"], g_re, g_im)
        cw_ref[q, r_re, c_g] = g1_re.astype(cw_ref.dtype)
        cw_ref[q, r_im, c_g] = (-g1_im).astype(cw_ref.dtype)
        v["g"] = (g_re, g_im)
    for g in range(GB):
        v = st[g]
        kt = dot_t(v["bb"][0], v["g"][0]) - dot_t(v["bb"][1], v["g"][1])
        v["kt"] = kt + jnp.where(lane_h == sub, d_ref[g], 0.0)
    for g in range(GB):
        kt = st[g]["kt"]
        for tau in range(T):
            blk = kt if tau == 0 else jnp.where(lane_h >= tau * H, pltpu.roll(kt, tau * H, axis=1), 0.0)
            m_ref[g, tau * H:(tau + 1) * H, :] = blk.astype(m_ref.dtype)


def _ssm_prep(lambda_re, lambda_im, log_step, b_re, b_im, c_re, c_im, d_skip, cast):
    G, P, H, GB = SSM_GROUPS, SSM_STATE, SSM_GROUP, GROUPS_PER_BLOCK
    n_steps = G // GB
    blk = lambda *shape: pl.BlockSpec(shape, lambda i: (i,) + (0,) * (len(shape) - 1))
    c_in, c_out, c_shapes = _cast_specs(cast, n_steps)
    return pl.pallas_call(
        functools.partial(_ssm_prep_kernel, len(cast)),
        grid=(n_steps,),
        in_specs=[blk(GB, P), blk(GB, P), blk(GB, 1),
                  blk(GB, P, H), blk(GB, P, H), blk(GB, H, P), blk(GB, H, P), blk(GB, H, 1)] + c_in,
        out_specs=[blk(GB, CHUNK_LANES, CHUNK_LANES), blk(GB // 2, 4 * P, 2 * CHUNK_LANES),
                   blk(GB // 2, 4 * P, 2 * CHUNK_LANES), blk(GB // 2, 2, 2 * P)] + c_out,
        out_shape=[jax.ShapeDtypeStruct((G, CHUNK_LANES, CHUNK_LANES), BF16),
                   jax.ShapeDtypeStruct((G // 2, 4 * P, 2 * CHUNK_LANES), BF16),
                   jax.ShapeDtypeStruct((G // 2, 4 * P, 2 * CHUNK_LANES), BF16),
                   jax.ShapeDtypeStruct((G // 2, 2, 2 * P), F32)] + c_shapes,
        compiler_params=pltpu.CompilerParams(dimension_semantics=("parallel",),
                                             vmem_limit_bytes=VMEM_LIMIT_BYTES),
        name="ssm_prep",
    )(lambda_re, lambda_im, log_step.reshape(G, 1),
      b_re, b_im, c_re, c_im, d_skip.reshape(G, H, 1), *cast)


def _granule_transpose(units):
    lane = lax.broadcasted_iota(jnp.int32, units[0][0].shape, 1)
    units = [list(a) for a in units]
    for s in (4, 2, 1):
        upper = ((lane // SSM_GROUP) & s) != 0
        for a in units:
            for i in range(GROUPS_PER_BLOCK):
                if i & s:
                    continue
                lo, hi = a[i], a[i + s]
                a[i] = jnp.where(upper, pltpu.roll(hi, s * SSM_GROUP, axis=1), lo)
                a[i + s] = jnp.where(upper, hi, pltpu.roll(lo, LANES - s * SSM_GROUP, axis=1))
    return units


def _relayout_units(n_chunks):
    return [(b, cb, hf) for b in range(SSM_WIDTH // LANES)
            for cb in range(n_chunks // RELAYOUT_ROWS) for hf in range(2)]


def _relayout_to_groups(src_ref, dst_ref, n_chunks):
    half = CHUNK // 2
    rb = RELAYOUT_ROWS
    where = _relayout_units(n_chunks)
    batch = 2 * RELAYOUT_UNITS
    for i in range(0, len(where), batch):
        units = [[src_ref[b, pl.ds(cb * rb * PARK_PITCH + hf * half + t, rb, stride=PARK_PITCH), :].astype(BF16)
                  for t in range(half)] for (b, cb, hf) in where[i:i + batch]]
        units = _granule_transpose(units)
        for (b, cb, hf), unit in zip(where[i:i + batch], units):
            for g in range(GROUPS_PER_BLOCK):
                dst_ref[b * GROUPS_PER_BLOCK + g, cb * rb:(cb + 1) * rb, hf * LANES:(hf + 1) * LANES] = unit[g]


def _relayout_from_groups(src_ref, dst_ref, n_chunks, blocks=range(SSM_WIDTH // LANES)):
    half = CHUNK // 2
    rb = RELAYOUT_ROWS
    where = [u for u in _relayout_units(n_chunks) if u[0] in blocks]
    batch = 2 * RELAYOUT_UNITS
    for i in range(0, len(where), batch):
        units = [[src_ref[b * GROUPS_PER_BLOCK + g, cb * rb:(cb + 1) * rb, hf * LANES:(hf + 1) * LANES]
                  for g in range(GROUPS_PER_BLOCK)] for (b, cb, hf) in where[i:i + batch]]
        units = _granule_transpose(units)
        for (b, cb, hf), unit in zip(where[i:i + batch], units):
            for t in range(half):
                dst_ref[b, pl.ds(cb * rb * PARK_PITCH + hf * half + t, rb, stride=PARK_PITCH), :] = unit[t].astype(F32)


def _in_proj_kernel(n_cast, n, x_ref, gain_ref, w_ref, *refs):
    cast_src, (xg_ref, v_ref), cast_dst = refs[:n_cast], refs[n_cast:n_cast + 2], refs[n_cast + 2:-2]
    park = refs[-2:]
    s = pl.program_id(0)

    def step(p_prev, p_next):
        if p_prev is not None:
            _relayout_to_groups(p_prev, xg_ref, x_ref.shape[0] // CHUNK)
        if p_next is not None:
            _cast_slabs(cast_src, cast_dst)
            h = _rms_norm(x_ref[...], gain_ref[...]).astype(BF16)
            p = jnp.dot(h, w_ref[...], preferred_element_type=F32)
            v_ref[...] = p[:, SSM_WIDTH:].astype(v_ref.dtype)
            for b in range(SSM_WIDTH // LANES):
                for c in range(x_ref.shape[0] // CHUNK):
                    p_next[b, c * PARK_PITCH:c * PARK_PITCH + CHUNK, :] = (
                        p[c * CHUNK:(c + 1) * CHUNK, b * LANES:(b + 1) * LANES])

    middle = jnp.logical_and(s > 0, s < n)
    pl.when(s == 0)(lambda: step(None, park[0]))
    pl.when(jnp.logical_and(middle, s % 2 == 0))(lambda: step(park[1], park[0]))
    pl.when(jnp.logical_and(middle, s % 2 == 1))(lambda: step(park[0], park[1]))
    pl.when(s == n)(lambda: step(park[(n - 1) % 2], None))


def _in_proj(x, gain, w_in, cast, tm):
    L = x.shape[0]
    n = L // tm
    c_in, c_out, c_shapes = _cast_specs(cast, n)
    cur = lambda s: (jnp.minimum(s, n - 1), 0)
    return pl.pallas_call(
        functools.partial(_in_proj_kernel, len(cast), n),
        grid=(n + 1,),
        in_specs=[pl.BlockSpec((tm, D_MODEL), cur),
                  pl.BlockSpec((1, D_MODEL), lambda s: (0, 0)),
                  pl.BlockSpec((D_MODEL, D_MODEL), lambda s: (0, 0))] + c_in,
        out_specs=[pl.BlockSpec((SSM_GROUPS, tm // CHUNK, CHUNK_LANES), lambda s: (0, jnp.maximum(s - 1, 0), 0)),
                   pl.BlockSpec((tm, POOL_WIDTH), cur)] + c_out,
        out_shape=[jax.ShapeDtypeStruct((SSM_GROUPS, L // CHUNK, CHUNK_LANES), BF16),
                   jax.ShapeDtypeStruct((L, POOL_WIDTH), BF16)] + c_shapes,
        scratch_shapes=[pltpu.VMEM((SSM_WIDTH // LANES, tm // CHUNK * PARK_PITCH, LANES), F32)] * 2,
        compiler_params=pltpu.CompilerParams(dimension_semantics=("arbitrary",),
                                             vmem_limit_bytes=VMEM_LIMIT_BYTES),
        name="in_proj",
    )(x, gain, w_in, *cast)


def _cmul(ar, ai, br, bi):
    return ar * br - ai * bi, ar * bi + ai * br


def _ssm_kernel(n_cast, xg_ref, m_ref, bw_ref, cw_ref, a_ref, *refs):
    cast_src, yg_ref, cast_dst = refs[:n_cast], refs[n_cast], refs[n_cast + 1:-4]
    fr_ref, fi_ref, cr_ref, ci_ref = refs[-4:]
    _cast_slabs(cast_src, cast_dst)
    n_chunks = xg_ref.shape[1]
    sb = SCAN_BLOCK
    n_sb = n_chunks // sb

    blk3 = (n_sb, sb, 2 * SSM_STATE)
    row_in = lax.broadcasted_iota(jnp.int32, blk3, 1)
    brow = lax.broadcasted_iota(jnp.int32, (n_sb, 2 * SSM_STATE), 0)
    for q in range(GROUPS_PER_BLOCK // 2):
        x0, x1 = xg_ref[2 * q], xg_ref[2 * q + 1]
        xp = jnp.concatenate([x0, x1], axis=1)
        e = lax.dot_general(xp, bw_ref[q], (((1,), (1,)), ((), ())), preferred_element_type=F32)
        a1r, a1i = a_ref[q, 0:1, :], a_ref[q, 1:2, :]

        fr = e[:, :2 * SSM_STATE].reshape(blk3)
        fi = e[:, 2 * SSM_STATE:].reshape(blk3)
        ar, ai = a1r, a1i
        d = 1
        while d < sb:
            tr = jnp.where(row_in >= d, pltpu.roll(fr, d, axis=1), 0.0)
            ti = jnp.where(row_in >= d, pltpu.roll(fi, d, axis=1), 0.0)
            fr, fi = fr + ar * tr - ai * ti, fi + ar * ti + ai * tr
            ar, ai = _cmul(ar, ai, ar, ai)
            d *= 2
        fr_ref[...] = fr.reshape(n_chunks, 2 * SSM_STATE)
        fi_ref[...] = fi.reshape(n_chunks, 2 * SSM_STATE)
        gr = fr_ref[pl.ds(sb - 1, n_sb, stride=sb), :]
        gi = fi_ref[pl.ds(sb - 1, n_sb, stride=sb), :]
        gr = jnp.where(brow >= 1, pltpu.roll(gr, 1, axis=0), 0.0)
        gi = jnp.where(brow >= 1, pltpu.roll(gi, 1, axis=0), 0.0)
        d = 1
        while d < n_sb:
            tr = jnp.where(brow >= d, pltpu.roll(gr, d, axis=0), 0.0)
            ti = jnp.where(brow >= d, pltpu.roll(gi, d, axis=0), 0.0)
            gr, gi = gr + ar * tr - ai * ti, gi + ar * ti + ai * tr
            ar, ai = _cmul(ar, ai, ar, ai)
            d *= 2
        pr, pi = jnp.ones_like(a1r), jnp.zeros_like(a1i)
        for r in range(sb):
            cr_ref[pl.ds(r, n_sb, stride=sb), :] = pr * gr - pi * gi
            ci_ref[pl.ds(r, n_sb, stride=sb), :] = pr * gi + pi * gr
            pr, pi = _cmul(pr, pi, a1r, a1i)
        sr = jnp.where(row_in >= 1, pltpu.roll(fr, 1, axis=1), 0.0).reshape(n_chunks, 2 * SSM_STATE)
        si = jnp.where(row_in >= 1, pltpu.roll(fi, 1, axis=1), 0.0).reshape(n_chunks, 2 * SSM_STATE)
        sr = sr + cr_ref[...]
        si = si + ci_ref[...]
        s = jnp.concatenate([sr, si], axis=1).astype(BF16)
        cw = cw_ref[q]
        for j, xj in enumerate((x0, x1)):
            lhs = jnp.concatenate([xj, s], axis=1)
            rhs = jnp.concatenate([m_ref[2 * q + j], cw[:, j * CHUNK_LANES:(j + 1) * CHUNK_LANES]], axis=0)
            yg_ref[2 * q + j] = jnp.dot(lhs, rhs, preferred_element_type=F32).astype(yg_ref.dtype)


def _ssm(xg, m, bw_pair, cw_pair, a_pair, cast):
    n_chunks = xg.shape[1]
    n_blocks = SSM_WIDTH // LANES
    ppb = GROUPS_PER_BLOCK // 2
    c_in, c_out, c_shapes = _cast_specs(cast, n_blocks)
    return pl.pallas_call(
        functools.partial(_ssm_kernel, len(cast)),
        grid=(n_blocks,),
        in_specs=[pl.BlockSpec((GROUPS_PER_BLOCK, n_chunks, CHUNK_LANES), lambda b: (b, 0, 0)),
                  pl.BlockSpec((GROUPS_PER_BLOCK, CHUNK_LANES, CHUNK_LANES), lambda b: (b, 0, 0)),
                  pl.BlockSpec((ppb, 4 * SSM_STATE, 2 * CHUNK_LANES), lambda b: (b, 0, 0)),
                  pl.BlockSpec((ppb, 4 * SSM_STATE, 2 * CHUNK_LANES), lambda b: (b, 0, 0)),
                  pl.BlockSpec((ppb, 2, 2 * SSM_STATE), lambda b: (b, 0, 0))] + c_in,
        out_specs=[pl.BlockSpec((GROUPS_PER_BLOCK, n_chunks, CHUNK_LANES), lambda b: (b, 0, 0))] + c_out,
        out_shape=[jax.ShapeDtypeStruct(xg.shape, BF16)] + c_shapes,
        scratch_shapes=[pltpu.VMEM((n_chunks, 2 * SSM_STATE), F32)] * 4,
        compiler_params=pltpu.CompilerParams(dimension_semantics=("parallel",),
                                             vmem_limit_bytes=VMEM_LIMIT_BYTES),
        name="ssm",
    )(xg, m, bw_pair, cw_pair, a_pair, *cast)


def _out_proj_kernel(n_cast, yg_ref, v_ref, vh_ref, x_ref, wglu_ref, bglu_ref, wpool_ref, bpool_ref,
                     pscale_ref, wtop_ref, wbot_ref, *refs):
    cast_src, o_ref, cast_dst = refs[:n_cast], refs[n_cast], refs[n_cast + 1:-5]
    mix_park, y_ref, vbuf_ref, psum_ref = refs[-5:-3], refs[-3], refs[-2], refs[-1]
    _cast_slabs(cast_src, cast_dst)
    s = pl.program_id(0)
    tm = x_ref.shape[0]

    @pl.when(s == 0)
    def _():
        mix_park[1][...] = jnp.zeros_like(mix_park[1])

    def step(mix_prev, mix_next):
        n_tile = 2 * LANES

        def project(n):
            c = slice(n * n_tile, (n + 1) * n_tile)
            o_ref[:, c] = (x_ref[:, c]
                           + jnp.dot(mix_prev[:, :SSM_WIDTH], wtop_ref[:, c], preferred_element_type=F32)
                           + jnp.dot(mix_prev[:, SSM_WIDTH:], wbot_ref[:, c], preferred_element_type=F32))

        t0 = POOL_PAD + HALO
        n_lb = POOL_GROUP // LANES
        vh = jnp.where(s > 0, vh_ref[...].astype(F32), 0.0)
        vt = v_ref[...].astype(F32)
        for j in range(POOL_WIDTH // LANES):
            vbuf_ref[j, 0:POOL_PAD, :] = jnp.zeros((POOL_PAD, LANES), F32)
            vbuf_ref[j, POOL_PAD:t0, :] = vh[:, j * LANES:(j + 1) * LANES]
            vbuf_ref[j, t0:, :] = vt[:, j * LANES:(j + 1) * LANES]
        t1 = (lax.broadcasted_iota(jnp.int32, (tm, 1), 0) + (s * tm + 1)).astype(F32)

        def window_sum(slab, first, rows, terms, step):
            acc = slab[first:first + rows, :]
            for i in range(1, terms):
                acc = acc + slab[first - i * step:first - i * step + rows, :]
            return acc

        def pool_window(k):
            w = POOL_WINDOWS[k]
            out = []
            for j in range(k * n_lb, (k + 1) * n_lb):
                slab = vbuf_ref.at[j]
                if w <= 4:
                    total = window_sum(slab, t0, tm, w, 1)
                else:
                    part = psum_ref.at[j % (2 * n_lb)]
                    part[POOL_PAD:, :] = window_sum(slab, POOL_PAD, HALO + tm, w // 4, 1)
                    total = window_sum(part, t0, tm, 4, w // 4)
                mean = total / jnp.minimum(t1, float(w))
                out.append(mean - slab[t0:, :])
            return jnp.concatenate(out, axis=1).astype(BF16)

        def gelu(blocks):
            y = jnp.concatenate(
                [jnp.concatenate([y_ref[b, c * PARK_PITCH:c * PARK_PITCH + CHUNK, :] for c in range(tm // CHUNK)], axis=0)
                 for b in blocks], axis=1)
            return 0.5 * y * (1.0 + lax.erf(y * (1.0 / math.sqrt(2.0))))

        half_blocks = SSM_WIDTH // LANES // 2
        pk = [pool_window(0), pool_window(1)]
        project(0)
        pk.append(pool_window(2))
        project(1)
        pk.append(pool_window(3))
        project(2)
        pooled = [jnp.dot(pk[k], wpool_ref[k], preferred_element_type=F32) for k in range(len(POOL_WINDOWS))]
        yp = (jnp.concatenate(pooled, axis=1) + bpool_ref[...]) * pscale_ref[...]
        mix_next[:, SSM_WIDTH:] = yp.astype(BF16)
        _relayout_from_groups(yg_ref, y_ref, tm // CHUNK, blocks=range(0, half_blocks))
        project(3)
        _relayout_from_groups(yg_ref, y_ref, tm // CHUNK, blocks=range(half_blocks, 2 * half_blocks))
        project(4)
        g_a = gelu(range(0, half_blocks))
        project(5)
        g_b = gelu(range(half_blocks, 2 * half_blocks))
        project(6)
        g = jnp.concatenate([g_a, g_b], axis=1)
        z = jnp.dot(g.astype(BF16), wglu_ref[...], preferred_element_type=F32) + bglu_ref[...]
        mix_next[:, :SSM_WIDTH] = (g * (1.0 / (1.0 + jnp.exp(-z)))).astype(BF16)
        project(7)

    pl.when(s % 2 == 0)(lambda: step(mix_park[1], mix_park[0]))
    pl.when(s % 2 == 1)(lambda: step(mix_park[0], mix_park[1]))


def _out_proj(yg, v, x, w_glu, b_glu, w_pool, b_pool, pool_scale, w_out, cast, tm):
    L = x.shape[0]
    n = L // tm
    halo_blocks = tm // HALO
    const = lambda *shape: pl.BlockSpec(shape, lambda s: (0,) * len(shape))
    c_in, c_out, c_shapes = _cast_specs(cast, n)
    cur = lambda s: jnp.minimum(s, n - 1)
    prev = lambda s: (jnp.maximum(s - 1, 0), 0)
    return pl.pallas_call(
        functools.partial(_out_proj_kernel, len(cast)),
        grid=(n + 1,),
        in_specs=[pl.BlockSpec((SSM_GROUPS, tm // CHUNK, CHUNK_LANES), lambda s: (0, cur(s), 0)),
                  pl.BlockSpec((tm, POOL_WIDTH), lambda s: (cur(s), 0)),
                  pl.BlockSpec((HALO, POOL_WIDTH), lambda s: (jnp.maximum(cur(s) * halo_blocks - 1, 0), 0)),
                  pl.BlockSpec((tm, D_MODEL), prev),
                  const(SSM_WIDTH, SSM_WIDTH), const(1, SSM_WIDTH),
                  const(len(POOL_WINDOWS), POOL_GROUP, POOL_GROUP), const(1, POOL_WIDTH),
                  const(1, POOL_WIDTH),
                  pl.BlockSpec((SSM_WIDTH, D_MODEL), lambda s: (0, 0)),
                  pl.BlockSpec((POOL_WIDTH, D_MODEL), lambda s: (1, 0))] + c_in,
        out_specs=[pl.BlockSpec((tm, D_MODEL), prev)] + c_out,
        out_shape=[jax.ShapeDtypeStruct((L, D_MODEL), F32)] + c_shapes,
        scratch_shapes=[pltpu.VMEM((tm, SSM_WIDTH + POOL_WIDTH), BF16)] * 2
                       + [pltpu.VMEM((SSM_WIDTH // LANES, tm // CHUNK * PARK_PITCH, LANES), F32),
                          pltpu.VMEM((POOL_WIDTH // LANES, POOL_PAD + HALO + tm, LANES), F32),
                          pltpu.VMEM((2 * POOL_GROUP // LANES, POOL_PAD + HALO + tm, LANES), F32)],
        compiler_params=pltpu.CompilerParams(dimension_semantics=("arbitrary",),
                                             vmem_limit_bytes=VMEM_LIMIT_BYTES),
        name="out_proj",
    )(yg, v, v, x, w_glu, b_glu, w_pool, b_pool, pool_scale, w_out, w_out, *cast)


def _ffn_kernel(x_ref, gain_ref, wg_ref, wu_ref, wd_ref, gfin_ref, o_ref, h_ref, r_ref):
    j = pl.program_id(1)
    last = pl.num_programs(1) - 1
    lane_tiles = wg_ref.shape[1] // LANES

    def slab(first, final):
        if first:
            x = x_ref[...]
            h = (x * gain_ref[...]).astype(BF16)
            h_ref[...] = h
            r = lax.rsqrt(jnp.mean(x * x, axis=-1, keepdims=True) + EPS)
            r = jnp.broadcast_to(r, r_ref.shape)
            r_ref[...] = r
        else:
            h, r = h_ref[...], r_ref[...]
        rt = jnp.concatenate([r] * lane_tiles, axis=1)
        a = jnp.dot(h, wg_ref[...], preferred_element_type=F32) * rt
        b = jnp.dot(h, wu_ref[...], preferred_element_type=F32) * rt
        ff = (a * (1.0 / (1.0 + jnp.exp(-a))) * b).astype(BF16)
        if not final:
            o_ref[...] = (x if first else o_ref[...]) + jnp.dot(ff, wd_ref[...], preferred_element_type=F32)
        else:
            n_tile = 2 * LANES
            ssq = jnp.zeros((o_ref.shape[0], 1), F32)
            for n in range(D_MODEL // n_tile):
                c = slice(n * n_tile, (n + 1) * n_tile)
                acc = o_ref[:, c] + jnp.dot(ff, wd_ref[:, c], preferred_element_type=F32)
                ssq = ssq + jnp.sum(acc * acc, axis=-1, keepdims=True)
                o_ref[:, c] = acc
            scale = lax.rsqrt(ssq * (1.0 / D_MODEL) + EPS)
            o_ref[...] = o_ref[...] * scale * gfin_ref[...]

    pl.when(j == 0)(lambda: slab(True, False))
    pl.when(jnp.logical_and(j > 0, j < last))(lambda: slab(False, False))
    pl.when(j == last)(lambda: slab(False, True))


def _ffn(x, gain, w_gate, w_up, w_down, gain_final, tm, tf):
    L = x.shape[0]
    d_ff = w_gate.shape[1]
    assert d_ff // tf >= 2
    return pl.pallas_call(
        _ffn_kernel,
        grid=(L // tm, d_ff // tf),
        in_specs=[pl.BlockSpec((tm, D_MODEL), lambda i, j: (i, 0)),
                  pl.BlockSpec((1, D_MODEL), lambda i, j: (0, 0)),
                  pl.BlockSpec((D_MODEL, tf), lambda i, j: (0, j)),
                  pl.BlockSpec((D_MODEL, tf), lambda i, j: (0, j)),
                  pl.BlockSpec((tf, D_MODEL), lambda i, j: (j, 0)),
                  pl.BlockSpec((1, D_MODEL), lambda i, j: (0, 0))],
        out_specs=pl.BlockSpec((tm, D_MODEL), lambda i, j: (i, 0)),
        out_shape=jax.ShapeDtypeStruct((L, D_MODEL), F32),
        scratch_shapes=[pltpu.VMEM((tm, D_MODEL), BF16), pltpu.VMEM((tm, LANES), F32)],
        compiler_params=pltpu.CompilerParams(dimension_semantics=("parallel", "arbitrary"),
                                             vmem_limit_bytes=VMEM_LIMIT_BYTES),
        name="ffn",
    )(x, gain, w_gate, w_up, w_down, gain_final)


def kernel(x, norm_mix, w_in, lambda_re, lambda_im, log_step, b_re, b_im, c_re, c_im, d_skip,
           w_glu, b_glu, w_pool, b_pool, pool_scale, w_out, norm_ffn, w_gate, w_up, w_down,
           norm_final):
    bsz, L, D = x.shape
    assert bsz == 1 and D == D_MODEL and w_in.shape[0] == 1 and L % 1024 == 0
    G, P = SSM_GROUPS, SSM_STATE
    xs = x.reshape(L, D).astype(F32)

    n_pool = len(POOL_WINDOWS)
    m, bw_pair, cw_pair, a_pair, w_in_b, w_out_b, w_glu_b, w_pool_b = _ssm_prep(
        lambda_re[0].astype(F32), lambda_im[0].astype(F32), log_step[0].astype(F32),
        b_re[0].astype(F32), b_im[0].astype(F32), c_re[0].astype(F32), c_im[0].astype(F32),
        d_skip[0].astype(F32),
        cast=[w_in[0], w_out[0], w_glu[0], w_pool[0].reshape(n_pool * POOL_GROUP, POOL_GROUP)])

    xg, v, w_gate_b = _in_proj(xs, norm_mix[0].astype(F32).reshape(1, D), w_in_b, cast=[w_gate[0]],
                               tm=PROJ_ROW_TILE)

    (yg,) = _ssm(xg, m, bw_pair, cw_pair, a_pair, cast=[])

    x1, w_up_b, w_down_b = _out_proj(
        yg, v, xs, w_glu_b, b_glu[0].astype(F32).reshape(1, SSM_WIDTH),
        w_pool_b.reshape(n_pool, POOL_GROUP, POOL_GROUP), b_pool[0].astype(F32).reshape(1, POOL_WIDTH),
        pool_scale[0].astype(F32).reshape(1, POOL_WIDTH), w_out_b, cast=[w_up[0], w_down[0]],
        tm=PROJ_ROW_TILE)

    out = _ffn(x1, norm_ffn[0].astype(F32).reshape(1, D), w_gate_b, w_up_b, w_down_b,
               norm_final.astype(F32).reshape(1, D), tm=FFN_ROW_TILE, tf=FFN_FF_TILE)
    return out.reshape(bsz, L, D).astype(x.dtype)
```
